```python
import math
import jax
import jax.numpy as jnp
from jax import lax
import numpy as np


D_MODEL = 2048
BATCH = 4
SEQ = 4096
DEPTH = 2

GRID_W = 64
CTX_LEN = 256
EPS = 1e-6

SSM_WIDTH = D_MODEL // 4
SSM_GROUP = 16
SSM_GROUPS = SSM_WIDTH // SSM_GROUP
SSM_STATE = 64

ATTN_HEAD_DIM = 128
ATTN_HEADS = (D_MODEL // 2) // ATTN_HEAD_DIM
ATTN_KV_HEADS = 2
ATTN_WIDTH = ATTN_HEADS * ATTN_HEAD_DIM
ATTN_KV_WIDTH = ATTN_KV_HEADS * ATTN_HEAD_DIM
Q_BLOCK = 128
ROPE_THETA = 10000.0

MLSTM_HEADS = 4
MLSTM_QK_DIM = 64
MLSTM_V_DIM = 128
MLSTM_QK_WIDTH = MLSTM_HEADS * MLSTM_QK_DIM
MLSTM_WIDTH = MLSTM_HEADS * MLSTM_V_DIM
MLSTM_CHUNK = 64
MLSTM_GATES = 2 * 2 * MLSTM_HEADS

N_BRANCHES = 3

N_EXPERTS = 32
TOP_K = 4
EXPERT_FF = D_MODEL // 2
SWIGLU_LIMIT = 7.0
SWIGLU_ALPHA = 1.702

SPLITS = (SSM_WIDTH, ATTN_WIDTH, ATTN_KV_WIDTH, ATTN_KV_WIDTH,
          MLSTM_QK_WIDTH, MLSTM_QK_WIDTH, MLSTM_WIDTH, MLSTM_WIDTH, MLSTM_GATES,
          N_BRANCHES * D_MODEL)
N_IN = sum(SPLITS)

kernel_name = 'hybrid_s5_gqa_mlstm_moe_ctxprefix_dit'


def rmsnorm(x, g):
    xf = x.astype(jnp.float32)
    y = xf * lax.rsqrt(jnp.mean(xf * xf, axis=-1, keepdims=True) + EPS)
    return (y * g.astype(jnp.float32)).astype(x.dtype)


def modulate(h, shift, scale):
    return h * (1.0 + scale) + shift


def ada_modulation(cond, w_ada, b_ada):
    return jnp.split(jax.nn.silu(cond) @ w_ada + b_ada, 6, axis=-1)


def to_heads(t, n_heads):
    b, l, _ = t.shape
    return t.reshape(b, l, n_heads, -1).transpose(0, 2, 1, 3)


def from_heads(t):
    b, h, l, d = t.shape
    return t.transpose(0, 2, 1, 3).reshape(b, l, h * d)


def split_columns(p):
    idx = np.cumsum(np.array(SPLITS))[:-1].tolist()
    return jnp.split(p, idx, axis=-1)


def _rope_1d(x, pos):
    n = x.shape[-1] // 2
    freq = ROPE_THETA ** (-jnp.arange(n, dtype=jnp.float32) / n)
    ang = pos.astype(jnp.float32)[:, None] * freq[None, :]
    cos = jnp.cos(ang).astype(x.dtype)
    sin = jnp.sin(ang).astype(x.dtype)
    x1, x2 = x[..., :n], x[..., n:]
    return jnp.concatenate([x1 * cos - x2 * sin, x2 * cos + x1 * sin], axis=-1)


def axial_rope(x, rows, cols):
    half = x.shape[-1] // 2
    return jnp.concatenate([_rope_1d(x[..., :half], rows), _rope_1d(x[..., half:], cols)], axis=-1)


def blocked_gqa(q, k, v):
    b, hq, lq, hd = q.shape
    hkv = k.shape[1]
    g = hq // hkv
    nb = lq // Q_BLOCK
    qb = jnp.moveaxis(q.reshape(b, hkv, g, nb, Q_BLOCK, hd), 3, 0)
    scale = hd ** -0.5

    def one_block(q_blk):
        s = jnp.einsum('bkgqd,bksd->bkgqs', q_blk, k).astype(jnp.float32) * scale
        p = jax.nn.softmax(s, axis=-1).astype(v.dtype)
        return jnp.einsum('bkgqs,bksd->bkgqd', p, v)

    o = lax.map(one_block, qb)
    return jnp.moveaxis(o, 0, 3).reshape(b, hq, lq, hd)


def attention_branch(qx, kx, vx, qc, kc, vc, q_gain, k_gain, rows, cols, with_ctx_out):
    qx = axial_rope(rmsnorm(to_heads(qx, ATTN_HEADS), q_gain), rows, cols)
    kx = axial_rope(rmsnorm(to_heads(kx, ATTN_KV_HEADS), k_gain), rows, cols)
    vx = to_heads(vx, ATTN_KV_HEADS)
    kc = rmsnorm(to_heads(kc, ATTN_KV_HEADS), k_gain)
    vc = to_heads(vc, ATTN_KV_HEADS)
    keys = jnp.concatenate([kc, kx], axis=2)
    vals = jnp.concatenate([vc, vx], axis=2)
    out_x = from_heads(blocked_gqa(qx, keys, vals))
    out_c = None
    if with_ctx_out:
        qc = rmsnorm(to_heads(qc, ATTN_HEADS), q_gain)
        out_c = from_heads(blocked_gqa(qc, kc, vc))
    return out_x, out_c


def s5_discretise(a_re, a_im, log_dt, b_re, b_im):
    lam = lax.complex(a_re.astype(jnp.float32), a_im.astype(jnp.float32))
    dt = jnp.exp(log_dt.astype(jnp.float32))[:, None]
    a_bar = jnp.exp(lam * dt)
    b_mat = lax.complex(b_re.astype(jnp.float32), b_im.astype(jnp.float32))
    b_bar = ((a_bar - 1.0) / lam)[..., None] * b_mat
    return a_bar, b_bar


def linear_scan(a_bar, bu, h0, reverse):
    first, last = (-1, 0) if reverse else (0, -1)
    bu = bu.at[:, first].add(a_bar * h0)
    a = jnp.broadcast_to(a_bar, bu.shape)

    def combine(e1, e2):
        a1, b1 = e1
        a2, b2 = e2
        return a2 * a1, a2 * b1 + b2

    _, states = lax.associative_scan(combine, (a, bu), axis=1, reverse=reverse)
    return states, states[:, last]


def half_glu(y, w_glu, b_glu):
    z = jax.nn.gelu(y)
    return z * jax.nn.sigmoid(z @ w_glu.astype(jnp.float32) + b_glu.astype(jnp.float32))


def s5_branch(ux, uc, p, with_ctx_out):
    b = ux.shape[0]
    gx = ux.astype(jnp.float32).reshape(b, ux.shape[1], SSM_GROUPS, SSM_GROUP)
    gc = uc.astype(jnp.float32).reshape(b, uc.shape[1], SSM_GROUPS, SSM_GROUP)
    d_skip = p['ssm_d'].astype(jnp.float32)
    yx = ux.astype(jnp.float32) * d_skip
    yc = uc.astype(jnp.float32) * d_skip
    h0 = jnp.zeros((b, SSM_GROUPS, SSM_STATE), jnp.complex64)
    for direction, reverse in ((0, False), (1, True)):
        a_bar, b_bar = s5_discretise(p['ssm_a_re'][direction], p['ssm_a_im'][direction],
                                     p['ssm_log_dt'][direction], p['ssm_b_re'][direction], p['ssm_b_im'][direction])
        c_mat = lax.complex(p['ssm_c_re'][direction].astype(jnp.float32),
                            p['ssm_c_im'][direction].astype(jnp.float32))
        st_c, h_ctx = linear_scan(a_bar, jnp.einsum('gph,blgh->blgp', b_bar, gc), h0, reverse)
        st_x, _ = linear_scan(a_bar, jnp.einsum('gph,blgh->blgp', b_bar, gx), h_ctx, reverse)
        yx = yx + jnp.real(jnp.einsum('ghp,blgp->blgh', c_mat, st_x)).reshape(yx.shape)
        if with_ctx_out:
            yc = yc + jnp.real(jnp.einsum('ghp,blgp->blgh', c_mat, st_c)).reshape(yc.shape)
    out_x = half_glu(yx, p['ssm_w_glu'], p['ssm_b_glu']).astype(ux.dtype)
    out_c = half_glu(yc, p['ssm_w_glu'], p['ssm_b_glu']).astype(uc.dtype) if with_ctx_out else None
    return out_x, out_c


def mlstm_chunked(q, k, v, log_i, log_f, state):
    b, h, l, dk = q.shape
    dv = v.shape[-1]
    nc = l // MLSTM_CHUNK

    def chunks(t):
        return jnp.moveaxis(t.reshape(b, h, nc, MLSTM_CHUNK, *t.shape[3:]), 2, 0)

    tril = jnp.tril(jnp.ones((MLSTM_CHUNK, MLSTM_CHUNK), dtype=bool))

    def step(carry, xs):
        c_st, n_st, m_st = carry
        qc, kc, vc, ic, fc = xs
        cum_f = jnp.cumsum(fc, axis=-1)
        log_d = jnp.where(tril, cum_f[..., :, None] - cum_f[..., None, :] + ic[..., None, :], -jnp.inf)
        log_inter = cum_f + m_st[..., None]
        m_row = jnp.maximum(log_inter, jnp.max(log_d, axis=-1))
        s = jnp.einsum('bhld,bhsd->bhls', qc, kc) * jnp.exp(log_d - m_row[..., None])
        w_inter = jnp.exp(log_inter - m_row)
        num = jnp.einsum('bhls,bhsv->bhlv', s, vc) + w_inter[..., None] * jnp.einsum('bhvd,bhld->bhlv', c_st, qc)
        den = jnp.sum(s, axis=-1) + w_inter * jnp.einsum('bhd,bhld->bhl', n_st, qc)
        h_out = num / jnp.maximum(jnp.abs(den), jnp.exp(-m_row))[..., None]
        total_f = cum_f[..., -1]
        log_w = total_f[..., None] - cum_f + ic
        m_new = jnp.maximum(total_f + m_st, jnp.max(log_w, axis=-1))
        decay = jnp.exp(total_f + m_st - m_new)
        w = jnp.exp(log_w - m_new[..., None])
        c_new = decay[..., None, None] * c_st + jnp.einsum('bhl,bhlv,bhld->bhvd', w, vc, kc)
        n_new = decay[..., None] * n_st + jnp.einsum('bhl,bhld->bhd', w, kc)
        return (c_new, n_new, m_new), h_out

    state, h_chunks = lax.scan(step, state, (chunks(q), chunks(k), chunks(v), chunks(log_i), chunks(log_f)))
    return jnp.moveaxis(h_chunks, 0, 2).reshape(b, h, l, dv), state


def mlstm_branch(parts_x, parts_c, gate_bias, norm_g, with_ctx_out):
    def prep(parts):
        q, k, v, o, g = parts
        b, l, _ = q.shape
        q = to_heads(q.astype(jnp.float32), MLSTM_HEADS) * MLSTM_QK_DIM ** -0.5
        k = to_heads(k.astype(jnp.float32), MLSTM_HEADS)
        v = to_heads(v.astype(jnp.float32), MLSTM_HEADS)
        g = g.astype(jnp.float32).reshape(b, l, 2, 2, MLSTM_HEADS) + gate_bias.astype(jnp.float32)
        g = jnp.transpose(g, (0, 2, 3, 4, 1))
        return q, k, v, o, g

    qx, kx, vx, ox, gx = prep(parts_x)
    qc, kc, vc, oc, gc = prep(parts_c)
    b = qx.shape[0]
    zero_state = (jnp.zeros((b, MLSTM_HEADS, MLSTM_V_DIM, MLSTM_QK_DIM), jnp.float32),
                  jnp.zeros((b, MLSTM_HEADS, MLSTM_QK_DIM), jnp.float32),
                  jnp.zeros((b, MLSTM_HEADS), jnp.float32))

    def run(q, k, v, g, direction, state):
        log_i = g[:, direction, 0]
        log_f = jax.nn.log_sigmoid(g[:, direction, 1])
        if direction == 1:
            q, k, v, log_i, log_f = (jnp.flip(t, axis=2) for t in (q, k, v, log_i, log_f))
        h, st = mlstm_chunked(q, k, v, log_i, log_f, state)
        if direction == 1:
            h = jnp.flip(h, axis=2)
        return h, st

    hc_f, st_f = run(qc, kc, vc, gc, 0, zero_state)
    hx_f, _ = run(qx, kx, vx, gx, 0, st_f)
    hc_b, st_b = run(qc, kc, vc, gc, 1, zero_state)
    hx_b, _ = run(qx, kx, vx, gx, 1, st_b)

    def finish(h, o):
        bb, _, l, _ = h.shape
        h = rmsnorm(h.transpose(0, 2, 1, 3), norm_g.reshape(MLSTM_HEADS, MLSTM_V_DIM)).reshape(bb, l, MLSTM_WIDTH)
        return (h * jax.nn.sigmoid(o.astype(jnp.float32))).astype(o.dtype)

    out_x = finish(hx_f + hx_b, ox)
    out_c = finish(hc_f + hc_b, oc) if with_ctx_out else None
    return out_x, out_c


def merge_branches(y_ssm, y_attn, y_mlstm, gate_logits, p):
    g_ssm, g_attn, g_mlstm = jnp.split(jax.nn.sigmoid(gate_logits), N_BRANCHES, axis=-1)
    y = (g_ssm * (y_ssm @ p['w_branch_ssm']) + g_attn * (y_attn @ p['w_branch_attn'])
         + g_mlstm * (y_mlstm @ p['w_branch_mlstm']))
    return y @ p['w_out']


def mixer_block(hx, hc, p, rows, cols, with_ctx_out):
    px = split_columns(hx @ p['w_in'])
    pc = split_columns(hc @ p['w_in'])
    ssm_x, ssm_c = s5_branch(px[0], pc[0], p, with_ctx_out)
    att_x, att_c = attention_branch(px[1], px[2], px[3], pc[1], pc[2], pc[3],
                                    p['attn_q_gain'], p['attn_k_gain'], rows, cols, with_ctx_out)
    ml_x, ml_c = mlstm_branch(px[4:9], pc[4:9], p['mlstm_gate_bias'], p['mlstm_norm_g'], with_ctx_out)
    out_x = merge_branches(ssm_x, att_x, ml_x, px[9], p)
    out_c = merge_branches(ssm_c, att_c, ml_c, pc[9], p) if with_ctx_out else None
    return out_x, out_c


def moe_ffn(h, w_router, b_router, w_gate_up, b_gate_up, w_down, b_down):
    b, l, d = h.shape
    t = h.reshape(b * l, d)
    logits = (t @ w_router + b_router).astype(jnp.float32)
    top_logits, top_idx = lax.top_k(logits, TOP_K)
    top_w = jax.nn.softmax(top_logits, axis=-1)
    combine = jnp.sum(jax.nn.one_hot(top_idx, N_EXPERTS, dtype=jnp.float32) * top_w[..., None], axis=1).astype(t.dtype)
    out = jnp.zeros_like(t)
    for e in range(N_EXPERTS):
        gu = t @ w_gate_up[e] + b_gate_up[e]
        gate = jnp.minimum(gu[:, 0::2], SWIGLU_LIMIT)
        lin = jnp.clip(gu[:, 1::2], -SWIGLU_LIMIT, SWIGLU_LIMIT)
        act = gate * jax.nn.sigmoid(SWIGLU_ALPHA * gate) * (lin + 1.0)
        out = out + combine[:, e:e + 1] * (act @ w_down[e] + b_down[e])
    return out.reshape(b, l, d)


def setup_inputs(seed: int = 0) -> dict:
    key = jax.random.key(seed)
    ks = iter(jax.random.split(key, 40))

    def nrm(shape, scale):
        return jax.random.normal(next(ks), shape, jnp.float32) * scale

    L = DEPTH
    H = MLSTM_HEADS
    inputs = {}
    inputs['x'] = nrm((BATCH, SEQ, D_MODEL), 1.0)
    inputs['c'] = nrm((BATCH, D_MODEL), 1.0)
    inputs['ctx'] = nrm((BATCH, CTX_LEN, D_MODEL), 1.0)
    inputs['c_ctx'] = nrm((D_MODEL,), 1.0)
    inputs['w_ada'] = nrm((L, D_MODEL, 6 * D_MODEL), 0.5 * D_MODEL ** -0.5)
    inputs['b_ada'] = nrm((L, 6 * D_MODEL), 0.01)
    inputs['norm_g'] = 1.0 + nrm((L, 2, D_MODEL), 0.02)
    inputs['w_in'] = nrm((L, D_MODEL, N_IN), D_MODEL ** -0.5)
    inputs['ssm_a_re'] = -0.5 + nrm((L, 2, SSM_GROUPS, SSM_STATE), 0.01)
    inputs['ssm_a_im'] = jnp.pi * jnp.arange(SSM_STATE, dtype=jnp.float32) + nrm((L, 2, SSM_GROUPS, SSM_STATE), 0.01)
    inputs['ssm_log_dt'] = jax.random.uniform(next(ks), (L, 2, SSM_GROUPS), jnp.float32,
                                              minval=math.log(1e-3), maxval=math.log(1e-1))
    inputs['ssm_b_re'] = nrm((L, 2, SSM_GROUPS, SSM_STATE, SSM_GROUP), (2 * SSM_GROUP) ** -0.5)
    inputs['ssm_b_im'] = nrm((L, 2, SSM_GROUPS, SSM_STATE, SSM_GROUP), (2 * SSM_GROUP) ** -0.5)
    inputs['ssm_c_re'] = nrm((L, 2, SSM_GROUPS, SSM_GROUP, SSM_STATE), (2 * SSM_STATE) ** -0.5)
    inputs['ssm_c_im'] = nrm((L, 2, SSM_GROUPS, SSM_GROUP, SSM_STATE), (2 * SSM_STATE) ** -0.5)
    inputs['ssm_d'] = nrm((L, SSM_WIDTH), 1.0)
    inputs['ssm_w_glu'] = nrm((L, SSM_WIDTH, SSM_WIDTH), SSM_WIDTH ** -0.5)
    inputs['ssm_b_glu'] = nrm((L, SSM_WIDTH), 0.01)
    inputs['attn_q_gain'] = 1.0 + nrm((L, ATTN_HEAD_DIM), 0.02)
    inputs['attn_k_gain'] = 1.0 + nrm((L, ATTN_HEAD_DIM), 0.02)
    i_bias = nrm((L, 2, 1, H), 0.1)
    f_bias = jnp.linspace(3.0, 6.0, H, dtype=jnp.float32) + nrm((L, 2, 1, H), 0.1)
    inputs['mlstm_gate_bias'] = jnp.concatenate([i_bias, f_bias], axis=2)
    inputs['mlstm_norm_g'] = 1.0 + nrm((L, MLSTM_WIDTH), 0.02)
    inputs['w_branch_ssm'] = nrm((L, SSM_WIDTH, D_MODEL), SSM_WIDTH ** -0.5)
    inputs['w_branch_attn'] = nrm((L, ATTN_WIDTH, D_MODEL), ATTN_WIDTH ** -0.5)
    inputs['w_branch_mlstm'] = nrm((L, MLSTM_WIDTH, D_MODEL), MLSTM_WIDTH ** -0.5)
    inputs['w_out'] = nrm((L, D_MODEL, D_MODEL), D_MODEL ** -0.5)
    inputs['w_router'] = nrm((L, D_MODEL, N_EXPERTS), D_MODEL ** -0.5)
    inputs['b_router'] = nrm((L, N_EXPERTS), 0.01)
    inputs['w_gate_up'] = nrm((L, N_EXPERTS, D_MODEL, 2 * EXPERT_FF), D_MODEL ** -0.5)
    inputs['b_gate_up'] = nrm((L, N_EXPERTS, 2 * EXPERT_FF), 0.01)
    inputs['w_down'] = nrm((L, N_EXPERTS, EXPERT_FF, D_MODEL), EXPERT_FF ** -0.5)
    inputs['b_down'] = nrm((L, N_EXPERTS, D_MODEL), 0.01)
    inputs['final_g'] = 1.0 + nrm((D_MODEL,), 0.02)
    return inputs


def reference(x, c, ctx, c_ctx, w_ada, b_ada, norm_g, w_in, ssm_a_re, ssm_a_im, ssm_log_dt,
              ssm_b_re, ssm_b_im, ssm_c_re, ssm_c_im, ssm_d, ssm_w_glu, ssm_b_glu,
              attn_q_gain, attn_k_gain, mlstm_gate_bias, mlstm_norm_g,
              w_branch_ssm, w_branch_attn, w_branch_mlstm, w_out,
              w_router, b_router, w_gate_up, b_gate_up, w_down, b_down, final_g):
    n_tok = x.shape[1]
    grid_rows = n_tok // GRID_W
    rows = jnp.repeat(jnp.arange(grid_rows, dtype=jnp.int32), GRID_W)
    cols = jnp.tile(jnp.arange(GRID_W, dtype=jnp.int32), grid_rows)
    for l in range(DEPTH):
        last = l == DEPTH - 1
        p = dict(w_in=w_in[l], ssm_a_re=ssm_a_re[l], ssm_a_im=ssm_a_im[l], ssm_log_dt=ssm_log_dt[l],
                 ssm_b_re=ssm_b_re[l], ssm_b_im=ssm_b_im[l], ssm_c_re=ssm_c_re[l], ssm_c_im=ssm_c_im[l],
                 ssm_d=ssm_d[l], ssm_w_glu=ssm_w_glu[l], ssm_b_glu=ssm_b_glu[l],
                 attn_q_gain=attn_q_gain[l], attn_k_gain=attn_k_gain[l],
                 mlstm_gate_bias=mlstm_gate_bias[l], mlstm_norm_g=mlstm_norm_g[l],
                 w_branch_ssm=w_branch_ssm[l], w_branch_attn=w_branch_attn[l],
                 w_branch_mlstm=w_branch_mlstm[l], w_out=w_out[l])
        mx = [m[:, None, :] for m in ada_modulation(c, w_ada[l], b_ada[l])]
        mc = ada_modulation(c_ctx, w_ada[l], b_ada[l])
        hx = modulate(rmsnorm(x, norm_g[l, 0]), mx[0], mx[1])
        hc = modulate(rmsnorm(ctx, norm_g[l, 0]), mc[0], mc[1])
        mix_x, mix_c = mixer_block(hx, hc, p, rows, cols, not last)
        x = x + mx[2] * mix_x
        hx = modulate(rmsnorm(x, norm_g[l, 1]), mx[3], mx[4])
        x = x + mx[5] * moe_ffn(hx, w_router[l], b_router[l], w_gate_up[l], b_gate_up[l], w_down[l], b_down[l])
        if not last:
            ctx = ctx + mc[2] * mix_c
            hc = modulate(rmsnorm(ctx, norm_g[l, 1]), mc[3], mc[4])
            ctx = ctx + mc[5] * moe_ffn(hc, w_router[l], b_router[l], w_gate_up[l], b_gate_up[l], w_down[l], b_down[l])
    return rmsnorm(x, final_g)
```

```python
import functools
import math

import jax
import jax.numpy as jnp
from jax import lax
from jax.experimental import pallas as pl
from jax.experimental.pallas import tpu as pltpu

F32 = jnp.float32
BF16 = jnp.bfloat16

GRID_W = 64
EPS = 1e-6
SSM_GROUP = 16
SSM_STATE = 64
SSM_CHUNK = 16
ATTN_HEAD_DIM = 128
ATTN_KV_HEADS = 2
ATTN_Q_PER_KV = 4
ROPE_THETA = 10000.0
MLSTM_HEADS = 4
MLSTM_QK_DIM = 64
MLSTM_V_DIM = 128
MLSTM_CHUNK = 64
N_EXPERTS = 32
TOP_K = 4
SWIGLU_LIMIT = 7.0
SWIGLU_ALPHA = 1.702

LANES = 128
VMEM_LIMIT_BYTES = 56 * 1024 * 1024
EXPERT_TILE = 256


def _cparams(sem):
    return pltpu.CompilerParams(dimension_semantics=sem, vmem_limit_bytes=VMEM_LIMIT_BYTES)


def _pick(n, cands):
    for c in cands:
        if n % c == 0:
            return c
    raise ValueError(f"no tile for {n} in {cands}")


def _ada_kernel(c_ref, w_ref, b_ref, o_ref):
    c = c_ref[...]
    s = c * jax.nn.sigmoid(c)
    o_ref[0] = jnp.dot(s, w_ref[0], preferred_element_type=F32) + b_ref[0]


def _ada_modulation(cond8, w_ada, b_ada):
    nl, d, n = w_ada.shape
    tn = _pick(n, (1024, 512, 256, 128))
    return pl.pallas_call(
        _ada_kernel,
        grid=(nl, n // tn),
        in_specs=[
            pl.BlockSpec((8, d), lambda l, j: (0, 0)),
            pl.BlockSpec((1, d, tn), lambda l, j: (l, 0, j)),
            pl.BlockSpec((1, 1, tn), lambda l, j: (l, 0, j)),
        ],
        out_specs=pl.BlockSpec((1, 8, tn), lambda l, j: (l, 0, j)),
        out_shape=jax.ShapeDtypeStruct((nl, 8, n), F32),
        compiler_params=_cparams(("parallel", "parallel")),
        name="ada_modulation",
    )(cond8, w_ada, b_ada.reshape(nl, 1, n))


def _row_select(mod_ref, k, is_ctx):
    return jnp.where(is_ctx, mod_ref[0, 0, k:k + 1, :], mod_ref[0, 1, k:k + 1, :])


def _modulated_norm(x, g, mod_ref, k_shift, is_ctx):
    ms = jnp.mean(x * x, axis=-1, keepdims=True)
    y = x * lax.rsqrt(ms + EPS) * g
    return y * (1.0 + _row_select(mod_ref, k_shift + 1, is_ctx)) + _row_select(mod_ref, k_shift, is_ctx)


def _is_ctx_rows(tm, lc):
    pos = pl.program_id(1) * tm + lax.broadcasted_iota(jnp.int32, (tm, 1), 0)
    return pos < lc


def _norm_mod_kernel(x_ref, g_ref, mod_ref, o_ref, *, lc, tm, k_shift):
    h = _modulated_norm(x_ref[0], g_ref[...], mod_ref, k_shift, _is_ctx_rows(tm, lc))
    o_ref[0] = h.astype(o_ref.dtype)


def _norm_mod(xa, g, mod, lc, k_shift):
    b, s, d = xa.shape
    tm = _pick(s, (544, 256, 128))
    return pl.pallas_call(
        functools.partial(_norm_mod_kernel, lc=lc, tm=tm, k_shift=k_shift),
        grid=(b, s // tm),
        in_specs=[
            pl.BlockSpec((1, tm, d), lambda i, j: (i, j, 0)),
            pl.BlockSpec((1, d), lambda i, j: (0, 0)),
            pl.BlockSpec((1, 2, 6, d), lambda i, j: (i, 0, 0, 0)),
        ],
        out_specs=pl.BlockSpec((1, tm, d), lambda i, j: (i, j, 0)),
        out_shape=jax.ShapeDtypeStruct((b, s, d), BF16),
        compiler_params=_cparams(("parallel", "parallel")),
        name="norm_mod",
    )(xa, g.reshape(1, d), mod)


def _norm_router_kernel(x_ref, g_ref, mod_ref, wr_ref, br_ref, h_ref, idx_ref, w_ref, *, lc, tm):
    h = _modulated_norm(x_ref[0], g_ref[...], mod_ref, 3, _is_ctx_rows(tm, lc))
    h_ref[0] = h.astype(h_ref.dtype)
    logits = jnp.dot(h, wr_ref[...], preferred_element_type=F32,
                     precision=lax.Precision.HIGHEST) + br_ref[...]
    lane = lax.broadcasted_iota(jnp.int32, logits.shape, 1).astype(F32)
    vals, idxs = [], []
    cur = logits
    for _ in range(TOP_K):
        mx = jnp.max(cur, axis=-1, keepdims=True)
        ix = jnp.min(jnp.where(cur == mx, lane, float(N_EXPERTS)), axis=-1, keepdims=True)
        vals.append(mx)
        idxs.append(ix)
        cur = jnp.where(lane == ix, -jnp.inf, cur)
    k_iota = lax.broadcasted_iota(jnp.int32, (logits.shape[0], TOP_K), 1)
    top = jnp.zeros((logits.shape[0], TOP_K), F32)
    top_i = jnp.zeros((logits.shape[0], TOP_K), F32)
    for k in range(TOP_K):
        top = jnp.where(k_iota == k, vals[k], top)
        top_i = jnp.where(k_iota == k, idxs[k], top_i)
    e = jnp.exp(top - vals[0])
    w_ref[0] = e / jnp.sum(e, axis=-1, keepdims=True)
    idx_ref[0] = top_i.astype(jnp.int32)


def _norm_router(xa, g, mod, w_router, b_router, lc):
    b, s, d = xa.shape
    tm = _pick(s, (544, 256, 128))
    ne = w_router.shape[1]
    return pl.pallas_call(
        functools.partial(_norm_router_kernel, lc=lc, tm=tm),
        grid=(b, s // tm),
        in_specs=[
            pl.BlockSpec((1, tm, d), lambda i, j: (i, j, 0)),
            pl.BlockSpec((1, d), lambda i, j: (0, 0)),
            pl.BlockSpec((1, 2, 6, d), lambda i, j: (i, 0, 0, 0)),
            pl.BlockSpec((d, ne), lambda i, j: (0, 0)),
            pl.BlockSpec((1, ne), lambda i, j: (0, 0)),
        ],
        out_specs=[
            pl.BlockSpec((1, tm, d), lambda i, j: (i, j, 0)),
            pl.BlockSpec((1, tm, TOP_K), lambda i, j: (i, j, 0)),
            pl.BlockSpec((1, tm, TOP_K), lambda i, j: (i, j, 0)),
        ],
        out_shape=[
            jax.ShapeDtypeStruct((b, s, d), BF16),
            jax.ShapeDtypeStruct((b, s, TOP_K), jnp.int32),
            jax.ShapeDtypeStruct((b, s, TOP_K), F32),
        ],
        compiler_params=_cparams(("parallel", "parallel")),
        name="norm_router",
    )(xa, g.reshape(1, d), mod, w_router, b_router.reshape(1, ne))


def _gemm_kernel(a_ref, w_ref, o_ref, *, act):
    acc = jnp.dot(a_ref[...], w_ref[...], preferred_element_type=F32)
    if act == "sigmoid":
        acc = jax.nn.sigmoid(acc)
    o_ref[...] = acc.astype(o_ref.dtype)


def _gemm(a, w, out_dtype, act=None, tn_cands=(1920, 1024, 512, 256, 128)):
    t, k = a.shape
    n = w.shape[1]
    tm = _pick(t, (1024, 768, 512, 256, 128))
    tn = _pick(n, tn_cands)
    return pl.pallas_call(
        functools.partial(_gemm_kernel, act=act),
        grid=(t // tm, n // tn),
        in_specs=[
            pl.BlockSpec((tm, k), lambda i, j: (i, 0)),
            pl.BlockSpec((k, tn), lambda i, j: (0, j)),
        ],
        out_specs=pl.BlockSpec((tm, tn), lambda i, j: (i, j)),
        out_shape=jax.ShapeDtypeStruct((t, n), out_dtype),
        compiler_params=_cparams(("parallel", "parallel")),
        name="gemm_" + (act or "plain"),
    )(a, w)


def _s5_operators(a_re, a_im, log_dt, b_re, b_im, c_re, c_im, d_skip):
    tc = SSM_CHUNK
    hp = lax.Precision.HIGHEST
    k_idx = jnp.arange(tc + 1, dtype=F32)
    lag = jnp.arange(tc)[None, :] - jnp.arange(tc)[:, None]
    ms, bcs, ccs, a16s = [], [], [], []
    for dirn in (0, 1):
        lam = lax.complex(a_re[dirn].astype(F32), a_im[dirn].astype(F32))
        dt = jnp.exp(log_dt[dirn].astype(F32))[:, None]
        pw = jnp.exp((lam * dt)[None] * k_idx[:, None, None])
        a_bar = pw[1]
        b_bar = ((a_bar - 1.0) / lam)[..., None] * lax.complex(b_re[dirn].astype(F32), b_im[dirn].astype(F32))
        c_mat = lax.complex(c_re[dirn].astype(F32), c_im[dirn].astype(F32))
        k_lag = jnp.real(jnp.einsum("ghp,kgp,gpj->kghj", c_mat, pw[:tc], b_bar, precision=hp))
        if dirn == 0:
            dist, valid = lag, lag >= 0
            end_pow = (tc - 1) - jnp.arange(tc)
            out_pow = jnp.arange(tc) + 1
        else:
            dist, valid = -lag, lag <= 0
            end_pow = jnp.arange(tc)
            out_pow = tc - jnp.arange(tc)
        kk = k_lag[jnp.clip(dist, 0, tc - 1)]
        kk = jnp.where(valid[:, :, None, None, None], kk, 0.0)
        g = kk.shape[2]
        ms.append(jnp.transpose(kk, (2, 0, 4, 1, 3)).reshape(g, tc * SSM_GROUP, tc * SSM_GROUP))
        bc = pw[end_pow][:, :, :, None] * b_bar[None]
        bcs.append(jnp.transpose(bc, (1, 0, 3, 2)).reshape(g, tc * SSM_GROUP, SSM_STATE))
        cw = c_mat[None] * pw[out_pow][:, :, None, :]
        ccs.append(jnp.transpose(cw, (1, 3, 0, 2)).reshape(g, SSM_STATE, tc * SSM_GROUP))
        a16s.append(pw[tc])
    m = ms[0] + ms[1]
    bc = jnp.concatenate([jnp.real(bcs[0]), jnp.real(bcs[1]), jnp.imag(bcs[0]), jnp.imag(bcs[1])], axis=-1)
    cc = jnp.concatenate([jnp.real(ccs[0]), jnp.real(ccs[1]), -jnp.imag(ccs[0]), -jnp.imag(ccs[1])], axis=1)
    a16 = jnp.stack([jnp.concatenate([jnp.real(a16s[0]), jnp.real(a16s[1])], axis=-1),
                     jnp.concatenate([jnp.imag(a16s[0]), jnp.imag(a16s[1])], axis=-1)], axis=1)
    g = m.shape[0]
    dtile = jnp.tile(d_skip.astype(F32).reshape(g, 1, SSM_GROUP), (1, 1, tc))
    return m.astype(BF16), bc.astype(BF16), cc.astype(BF16), a16.astype(F32), dtile


def _s5_kernel(u_ref, m_ref, bc_ref, cc_ref, a_ref, d_ref, y_ref, v_scr, hp_scr, *, nb, nchunk, nctx):
    p = SSM_STATE
    u = u_ref[0]
    ub = u.astype(BF16)
    v_scr[...] = jnp.dot(ub, bc_ref[0], preferred_element_type=F32)
    ar = a_ref[0, 0:1, :]
    ai = a_ref[0, 1:2, :]
    is_fwd = lax.broadcasted_iota(jnp.int32, (nb, 2 * p), 1) < p
    hre = jnp.zeros((nb, 2 * p), F32)
    him = jnp.zeros((nb, 2 * p), F32)
    for s in range(nchunk):
        cf = s
        cb = (nctx - 1 - s) if s < nctx else (nchunk - 1 - (s - nctx))
        rf = slice(cf * nb, (cf + 1) * nb)
        rb = slice(cb * nb, (cb + 1) * nb)
        hp_scr[rf, 0:p] = hre[:, 0:p]
        hp_scr[rb, p:2 * p] = hre[:, p:2 * p]
        hp_scr[rf, 2 * p:3 * p] = him[:, 0:p]
        hp_scr[rb, 3 * p:4 * p] = him[:, p:2 * p]
        vre = jnp.where(is_fwd, v_scr[rf, 0:2 * p], v_scr[rb, 0:2 * p])
        vim = jnp.where(is_fwd, v_scr[rf, 2 * p:4 * p], v_scr[rb, 2 * p:4 * p])
        hre, him = ar * hre - ai * him + vre, ar * him + ai * hre + vim
    y = jnp.dot(ub, m_ref[0], preferred_element_type=F32)
    y = y + jnp.dot(hp_scr[...].astype(BF16), cc_ref[0], preferred_element_type=F32)
    y_ref[0] = y + u * d_ref[0]


def _s5_mix(u, ops, nb, lc):
    m, bc, cc, a16, dtile = ops
    b, s, w = u.shape
    g = w // SSM_GROUP
    tc = SSM_CHUNK
    nchunk = s // tc
    r = nchunk * nb
    k = tc * SSM_GROUP
    ug = jnp.transpose(u.reshape(b, nchunk, tc, g, SSM_GROUP), (3, 1, 0, 2, 4)).reshape(g, r, k)
    yg = pl.pallas_call(
        functools.partial(_s5_kernel, nb=nb, nchunk=nchunk, nctx=lc // tc),
        grid=(g,),
        in_specs=[
            pl.BlockSpec((1, r, k), lambda i: (i, 0, 0)),
            pl.BlockSpec((1, k, k), lambda i: (i, 0, 0)),
            pl.BlockSpec((1, k, 4 * SSM_STATE), lambda i: (i, 0, 0)),
            pl.BlockSpec((1, 4 * SSM_STATE, k), lambda i: (i, 0, 0)),
            pl.BlockSpec((1, 2, 2 * SSM_STATE), lambda i: (i, 0, 0)),
            pl.BlockSpec((1, 1, k), lambda i: (i, 0, 0)),
        ],
        out_specs=pl.BlockSpec((1, r, k), lambda i: (i, 0, 0)),
        out_shape=jax.ShapeDtypeStruct((g, r, k), F32),
        scratch_shapes=[pltpu.VMEM((r, 4 * SSM_STATE), F32), pltpu.VMEM((r, 4 * SSM_STATE), F32)],
        compiler_params=_cparams(("parallel",)),
        name="s5_mix",
    )(ug, m, bc, cc, a16, dtile)
    return jnp.transpose(yg.reshape(g, nchunk, b, tc, SSM_GROUP), (2, 1, 3, 0, 4)).reshape(b, s, w)


def _rope_tables(lc, lx):
    n = ATTN_HEAD_DIM // 4
    freq = ROPE_THETA ** (-jnp.arange(n, dtype=F32) / n)
    t = jnp.arange(lx, dtype=jnp.int32)
    rows = (t // GRID_W).astype(F32)[:, None] * freq[None, :]
    cols = (t % GRID_W).astype(F32)[:, None] * freq[None, :]
    cos = jnp.concatenate([jnp.cos(rows), jnp.cos(rows), jnp.cos(cols), jnp.cos(cols)], axis=-1)
    sin = jnp.concatenate([-jnp.sin(rows), jnp.sin(rows), -jnp.sin(cols), jnp.sin(cols)], axis=-1)
    cos = jnp.concatenate([jnp.ones((lc, ATTN_HEAD_DIM), F32), cos], axis=0)
    sin = jnp.concatenate([jnp.zeros((lc, ATTN_HEAD_DIM), F32), sin], axis=0)
    return cos, sin


def _qk_prep_kernel(q_ref, k_ref, v_ref, cos_ref, sin_ref, qg_ref, kg_ref, qo_ref, ko_ref, vo_ref):
    hd = ATTN_HEAD_DIM
    cos = cos_ref[...]
    sin = sin_ref[...]
    lane = lax.broadcasted_iota(jnp.int32, cos.shape, 1)
    first = (lane % (hd // 2)) < (hd // 4)

    def prep(x, g):
        y = x * lax.rsqrt(jnp.mean(x * x, axis=-1, keepdims=True) + EPS) * g
        partner = jnp.where(first, pltpu.roll(y, hd - hd // 4, 1), pltpu.roll(y, hd // 4, 1))
        return y * cos + partner * sin

    for h in range(ATTN_Q_PER_KV):
        sl = slice(h * hd, (h + 1) * hd)
        qo_ref[0, :, sl] = prep(q_ref[0, :, sl], qg_ref[...]).astype(qo_ref.dtype)
    ko_ref[0] = prep(k_ref[0], kg_ref[...]).astype(ko_ref.dtype)
    vo_ref[0] = v_ref[0].astype(vo_ref.dtype)


def _qk_prep(p3, cos, sin, q_gain, k_gain, col_q, col_k, col_v):
    b, s, _ = p3.shape
    hd = ATTN_HEAD_DIM
    qw = ATTN_Q_PER_KV * hd
    tq = _pick(s, (544, 256, 128))
    return pl.pallas_call(
        _qk_prep_kernel,
        grid=(b, s // tq, ATTN_KV_HEADS),
        in_specs=[
            pl.BlockSpec((1, tq, qw), lambda i, j, kv: (i, j, col_q // qw + kv)),
            pl.BlockSpec((1, tq, hd), lambda i, j, kv: (i, j, col_k // hd + kv)),
            pl.BlockSpec((1, tq, hd), lambda i, j, kv: (i, j, col_v // hd + kv)),
            pl.BlockSpec((tq, hd), lambda i, j, kv: (j, 0)),
            pl.BlockSpec((tq, hd), lambda i, j, kv: (j, 0)),
            pl.BlockSpec((1, hd), lambda i, j, kv: (0, 0)),
            pl.BlockSpec((1, hd), lambda i, j, kv: (0, 0)),
        ],
        out_specs=[
            pl.BlockSpec((1, tq, qw), lambda i, j, kv: (i, j, kv)),
            pl.BlockSpec((1, tq, hd), lambda i, j, kv: (i, j, kv)),
            pl.BlockSpec((1, tq, hd), lambda i, j, kv: (i, j, kv)),
        ],
        out_shape=[
            jax.ShapeDtypeStruct((b, s, ATTN_KV_HEADS * qw), BF16),
            jax.ShapeDtypeStruct((b, s, ATTN_KV_HEADS * hd), BF16),
            jax.ShapeDtypeStruct((b, s, ATTN_KV_HEADS * hd), BF16),
        ],
        compiler_params=_cparams(("parallel", "parallel", "parallel")),
        name="qk_prep",
    )(p3, p3, p3, cos, sin, q_gain.reshape(1, hd), k_gain.reshape(1, hd))


def _attn_kernel(q_ref, k_ref, v_ref, o_ref, *, lc, tq):
    hd = ATTN_HEAD_DIM
    c = (hd ** -0.5) * math.log2(math.e)

    def run(kk, vv):
        for h in range(ATTN_Q_PER_KV):
            sl = slice(h * hd, (h + 1) * hd)
            s = lax.dot_general(q_ref[0, :, sl], kk, (((1,), (1,)), ((), ())), preferred_element_type=F32)
            m = jnp.max(s, axis=-1, keepdims=True)
            p = jnp.exp2((s - m) * c)
            l = jnp.sum(p, axis=-1, keepdims=True)
            o = jnp.dot(p.astype(BF16), vv, preferred_element_type=F32)
            o_ref[0, :, sl] = (o / l).astype(o_ref.dtype)

    is_ctx_tile = pl.program_id(2) * tq < lc

    @pl.when(jnp.logical_not(is_ctx_tile))
    def _():
        run(k_ref[0], v_ref[0])

    @pl.when(is_ctx_tile)
    def _():
        run(k_ref[0, :lc, :], v_ref[0, :lc, :])


def _attention(qh, kh, vh, lc):
    b, s, _ = qh.shape
    hd = ATTN_HEAD_DIM
    qw = ATTN_Q_PER_KV * hd
    tq = _pick(lc, (256, 128))
    assert s % tq == 0
    return pl.pallas_call(
        functools.partial(_attn_kernel, lc=lc, tq=tq),
        grid=(b, ATTN_KV_HEADS, s // tq),
        in_specs=[
            pl.BlockSpec((1, tq, qw), lambda i, kv, j: (i, j, kv)),
            pl.BlockSpec((1, s, hd), lambda i, kv, j: (i, 0, kv)),
            pl.BlockSpec((1, s, hd), lambda i, kv, j: (i, 0, kv)),
        ],
        out_specs=pl.BlockSpec((1, tq, qw), lambda i, kv, j: (i, j, kv)),
        out_shape=jax.ShapeDtypeStruct((b, s, ATTN_KV_HEADS * qw), BF16),
        compiler_params=_cparams(("parallel", "parallel", "parallel")),
        name="attention",
    )(qh, kh, vh)


def _split3(x):
    hi = x.astype(BF16)
    r = x - hi.astype(F32)
    mid = r.astype(BF16)
    lo = (r - mid.astype(F32)).astype(BF16)
    return hi, mid, lo


def _log_sigmoid(x):
    return jnp.minimum(x, 0.0) - jnp.log(1.0 + jnp.exp(-jnp.abs(x)))


def _mlstm_kernel(q_ref, k_ref, v_ref, gc_ref, gr_ref, bc_ref, br_ref, o_ref, c_scr, n_scr, m_scr):
    nh, dk, dv, cl = MLSTM_HEADS, MLSTM_QK_DIM, MLSTM_V_DIM, MLSTM_CHUNK
    d = pl.program_id(0)
    c = pl.program_id(2)

    @pl.when(c == 0)
    def _():
        c_scr[...] = jnp.zeros_like(c_scr)
        n_scr[...] = jnp.zeros_like(n_scr)
        m_scr[...] = jnp.zeros_like(m_scr)

    row = lax.broadcasted_iota(jnp.int32, (cl, cl), 0)
    col = lax.broadcasted_iota(jnp.int32, (cl, cl), 1)
    sign = jnp.where(d == 0, 1, -1)
    mask = sign * (row - col) >= 0
    tm = jnp.where(mask, 1.0, 0.0).astype(BF16)
    tmt = jnp.where(sign * (col - row) >= 0, 1.0, 0.0).astype(BF16)

    gc = gc_ref[0] + bc_ref[...]
    gr = gr_ref[0, 0] + br_ref[...]
    lf_c = _log_sigmoid(gc)
    lf_r = _log_sigmoid(gr)
    cum_c_all = sum(jnp.dot(tm, part, preferred_element_type=F32) for part in _split3(lf_c))
    cum_r_all = sum(jnp.dot(part, tmt, preferred_element_type=F32) for part in _split3(lf_r))
    total_all = jnp.sum(lf_c, axis=0, keepdims=True)

    def pick_c(a, k):
        return jnp.where(d == 0, a[:, k:k + 1], a[:, 2 * nh + k:2 * nh + k + 1])

    def pick_r(a, k):
        return jnp.where(d == 0, a[k:k + 1, :], a[2 * nh + k:2 * nh + k + 1, :])

    for h in range(nh):
        q = q_ref[0][:, h * dk:(h + 1) * dk] * (dk ** -0.5)
        k = k_ref[0][:, h * dk:(h + 1) * dk]
        v = v_ref[0][:, h * dv:(h + 1) * dv]
        qb, kb, vb = q.astype(BF16), k.astype(BF16), v.astype(BF16)
        cf_c = pick_c(cum_c_all, nh + h)
        cf_r = pick_r(cum_r_all, nh + h)
        li_c = pick_c(gc, h)
        li_r = pick_r(gr, h)
        m_st = m_scr[h][:, 0:1]
        c_st = c_scr[h]
        n_st = n_scr[h]

        log_d = jnp.where(mask, cf_c - cf_r + li_r, -jnp.inf)
        log_inter = cf_c + m_st
        m_row = jnp.maximum(log_inter, jnp.max(log_d, axis=-1, keepdims=True))
        s = lax.dot_general(qb, kb, (((1,), (1,)), ((), ())), preferred_element_type=F32)
        s = s * jnp.exp(log_d - m_row)
        w_inter = jnp.exp(log_inter - m_row)
        cq = lax.dot_general(qb, c_st.astype(BF16), (((1,), (1,)), ((), ())), preferred_element_type=F32)
        num = jnp.dot(s.astype(BF16), vb, preferred_element_type=F32) + w_inter * cq
        den = jnp.sum(s, axis=-1, keepdims=True) + w_inter * jnp.sum(q * n_st, axis=-1, keepdims=True)
        o_ref[0, 0, :, h * dv:(h + 1) * dv] = num / jnp.maximum(jnp.abs(den), jnp.exp(-m_row))

        tot = pick_c(total_all, nh + h)
        log_w = tot - cf_c + li_c
        m_new = jnp.maximum(tot + m_st, jnp.max(log_w, axis=0, keepdims=True))
        decay = jnp.exp(tot + m_st - m_new)
        w = jnp.exp(log_w - m_new)
        wv = (w * v).astype(BF16)
        c_scr[h] = decay * c_st + lax.dot_general(wv, kb, (((0,), (0,)), ((), ())),
                                                  preferred_element_type=F32)
        n_scr[h] = decay * n_st + jnp.sum(w * k, axis=0, keepdims=True)
        m_scr[h] = jnp.broadcast_to(m_new, (1, LANES))


def _mlstm(p3, gate_bias, lc, col_q, col_k, col_v, col_g):
    b, s, _ = p3.shape
    nh, dk, dv, cl = MLSTM_HEADS, MLSTM_QK_DIM, MLSTM_V_DIM, MLSTM_CHUNK
    nc = s // cl
    nctx = lc // cl
    ng = 4 * nh
    g_rows = jnp.transpose(p3[:, :, col_g:col_g + ng].reshape(b, nc, cl, ng), (0, 1, 3, 2))
    bias = gate_bias.astype(F32).reshape(ng)

    def chunk(d, c):
        bwd = jnp.where(c < nctx, nctx - 1 - c, nc - 1 - (c - nctx))
        return jnp.where(d == 0, c, bwd)

    return pl.pallas_call(
        _mlstm_kernel,
        grid=(2, b, nc),
        in_specs=[
            pl.BlockSpec((1, cl, nh * dk), lambda d, i, c: (i, chunk(d, c), col_q // (nh * dk))),
            pl.BlockSpec((1, cl, nh * dk), lambda d, i, c: (i, chunk(d, c), col_k // (nh * dk))),
            pl.BlockSpec((1, cl, nh * dv), lambda d, i, c: (i, chunk(d, c), col_v // (nh * dv))),
            pl.BlockSpec((1, cl, LANES), lambda d, i, c: (i, chunk(d, c), col_g // LANES)),
            pl.BlockSpec((1, 1, ng, cl), lambda d, i, c: (i, chunk(d, c), 0, 0)),
            pl.BlockSpec((1, LANES), lambda d, i, c: (0, 0)),
            pl.BlockSpec((ng, 1), lambda d, i, c: (0, 0)),
        ],
        out_specs=pl.BlockSpec((1, 1, cl, nh * dv), lambda d, i, c: (d, i, chunk(d, c), 0)),
        out_shape=jax.ShapeDtypeStruct((2, b, s, nh * dv), F32),
        scratch_shapes=[pltpu.VMEM((nh, dv, dk), F32), pltpu.VMEM((nh, 1, dk), F32),
                        pltpu.VMEM((nh, 1, LANES), F32)],
        compiler_params=_cparams(("arbitrary", "arbitrary", "arbitrary")),
        name="mlstm",
    )(p3, p3, p3, p3, g_rows, jnp.pad(bias, (0, LANES - ng)).reshape(1, LANES), bias.reshape(ng, 1))


def _gelu_tanh(x):
    return 0.5 * x * (1.0 + jnp.tanh(math.sqrt(2.0 / math.pi) * (x + 0.044715 * (x * x * x))))


def _merge_kernel(ys_ref, att_ref, hf_ref, hb_ref, og_ref, gl_ref, wglu_ref, bglu_ref, ng_ref,
                  wbs_ref, wba_ref, wbm_ref, y_ref):
    d = y_ref.shape[-1]
    z = _gelu_tanh(ys_ref[...])
    s5 = z * jax.nn.sigmoid(jnp.dot(z.astype(BF16), wglu_ref[...], preferred_element_type=F32) + bglu_ref[...])
    hsum = hf_ref[0] + hb_ref[0]
    dv = MLSTM_V_DIM
    parts = []
    for h in range(MLSTM_HEADS):
        blk = hsum[:, h * dv:(h + 1) * dv]
        parts.append(blk * lax.rsqrt(jnp.mean(blk * blk, axis=-1, keepdims=True) + EPS))
    ml = jnp.concatenate(parts, axis=-1) * ng_ref[...] * jax.nn.sigmoid(og_ref[...])
    y = gl_ref[:, 0:d].astype(F32) * jnp.dot(s5.astype(BF16), wbs_ref[...], preferred_element_type=F32)
    y = y + gl_ref[:, d:2 * d].astype(F32) * jnp.dot(att_ref[...], wba_ref[...], preferred_element_type=F32)
    y = y + gl_ref[:, 2 * d:3 * d].astype(F32) * jnp.dot(ml.astype(BF16), wbm_ref[...], preferred_element_type=F32)
    y_ref[...] = y.astype(y_ref.dtype)


def _merge(ys, att, hfb, p_main, gates, wglu, bglu, ng, wbs, wba, wbm, col_o):
    t, ws = ys.shape
    wa = att.shape[1]
    wm = hfb.shape[-1]
    d = wbs.shape[1]
    tm = _pick(t, (256, 128))
    full = lambda shape: pl.BlockSpec(shape, lambda i: (0,) * len(shape))
    return pl.pallas_call(
        _merge_kernel,
        grid=(t // tm,),
        in_specs=[
            pl.BlockSpec((tm, ws), lambda i: (i, 0)),
            pl.BlockSpec((tm, wa), lambda i: (i, 0)),
            pl.BlockSpec((1, tm, wm), lambda i: (0, i, 0)),
            pl.BlockSpec((1, tm, wm), lambda i: (1, i, 0)),
            pl.BlockSpec((tm, wm), lambda i: (i, col_o // wm)),
            pl.BlockSpec((tm, 3 * d), lambda i: (i, 0)),
            full((ws, ws)), full((1, ws)), full((1, wm)),
            full((ws, d)), full((wa, d)), full((wm, d)),
        ],
        out_specs=pl.BlockSpec((tm, d), lambda i: (i, 0)),
        out_shape=jax.ShapeDtypeStruct((t, d), BF16),
        compiler_params=_cparams(("parallel",)),
        name="branch_merge",
    )(ys, att, hfb, hfb, p_main, gates, wglu, bglu.reshape(1, ws), ng.reshape(1, wm), wbs, wba, wbm)


def _out_proj_kernel(y_ref, w_ref, x_ref, mod_ref, o_ref, *, lc, tm):
    out = jnp.dot(y_ref[0], w_ref[...], preferred_element_type=F32)
    gate = _row_select(mod_ref, 2, _is_ctx_rows(tm, lc))
    o_ref[0] = x_ref[0] + gate * out


def _out_proj(y3, w_out, xa, mod, lc):
    b, s, d = xa.shape
    tm = _pick(s, (544, 256, 128))
    return pl.pallas_call(
        functools.partial(_out_proj_kernel, lc=lc, tm=tm),
        grid=(b, s // tm),
        in_specs=[
            pl.BlockSpec((1, tm, d), lambda i, j: (i, j, 0)),
            pl.BlockSpec((d, d), lambda i, j: (0, 0)),
            pl.BlockSpec((1, tm, d), lambda i, j: (i, j, 0)),
            pl.BlockSpec((1, 2, 6, d), lambda i, j: (i, 0, 0, 0)),
        ],
        out_specs=pl.BlockSpec((1, tm, d), lambda i, j: (i, j, 0)),
        out_shape=jax.ShapeDtypeStruct((b, s, d), F32),
        compiler_params=_cparams(("parallel", "parallel")),
        name="out_proj_residual",
    )(y3, w_out, xa, mod)


def _expert_kernel(te_ref, nu_ref, x_ref, wgl_ref, bgl_ref, wd_ref, bd_ref, cw_ref, o_ref):
    i = pl.program_id(0)
    ff = wd_ref.shape[1]

    @pl.when(i < nu_ref[0])
    def _():
        gu = jnp.dot(x_ref[...], wgl_ref[0], preferred_element_type=F32) + bgl_ref[0]
        gate = jnp.minimum(gu[:, 0:ff], SWIGLU_LIMIT)
        lin = jnp.clip(gu[:, ff:2 * ff], -SWIGLU_LIMIT, SWIGLU_LIMIT)
        act = gate * jax.nn.sigmoid(SWIGLU_ALPHA * gate) * (lin + 1.0)
        y = jnp.dot(act.astype(BF16), wd_ref[0], preferred_element_type=F32) + bd_ref[0]
        o_ref[...] = (cw_ref[...] * y).astype(o_ref.dtype)

    @pl.when(i >= nu_ref[0])
    def _():
        o_ref[...] = jnp.zeros_like(o_ref)


def _experts(xs, cw, tile_expert, n_used, wgl, bgl, wd, bd):
    p, d = xs.shape
    ne, _, ff2 = wgl.shape
    ff = ff2 // 2
    tm = EXPERT_TILE
    n_tiles = p // tm
    grid_spec = pltpu.PrefetchScalarGridSpec(
        num_scalar_prefetch=2,
        grid=(n_tiles,),
        in_specs=[
            pl.BlockSpec((tm, d), lambda i, te, nu: (jnp.minimum(i, nu[0] - 1), 0)),
            pl.BlockSpec((1, d, ff2), lambda i, te, nu: (te[i], 0, 0)),
            pl.BlockSpec((1, 1, ff2), lambda i, te, nu: (te[i], 0, 0)),
            pl.BlockSpec((1, ff, d), lambda i, te, nu: (te[i], 0, 0)),
            pl.BlockSpec((1, 1, d), lambda i, te, nu: (te[i], 0, 0)),
            pl.BlockSpec((tm, 1), lambda i, te, nu: (jnp.minimum(i, nu[0] - 1), 0)),
        ],
        out_specs=pl.BlockSpec((tm, d), lambda i, te, nu: (i, 0)),
    )
    return pl.pallas_call(
        _expert_kernel,
        grid_spec=grid_spec,
        out_shape=jax.ShapeDtypeStruct((p, d), BF16),
        compiler_params=_cparams(("arbitrary",)),
        name="experts",
    )(tile_expert, n_used, xs, wgl, bgl, wd, bd, cw)


def _route(top_idx, top_w, n_slots):
    tm = EXPERT_TILE
    e = top_idx.reshape(-1)
    n = e.shape[0]
    onehot = (e[:, None] == jnp.arange(N_EXPERTS, dtype=jnp.int32)[None, :]).astype(jnp.int32)
    counts = jnp.sum(onehot, axis=0)
    rank = jnp.take_along_axis(jnp.cumsum(onehot, axis=0) - onehot, e[:, None], axis=1)[:, 0]
    padded = ((counts + tm - 1) // tm) * tm
    ends = jnp.cumsum(padded)
    dest = (ends - padded)[e] + rank
    n_used = (ends[-1] // tm).astype(jnp.int32)
    tile_start = jnp.arange(n_slots // tm, dtype=jnp.int32) * tm
    tile_expert = jnp.minimum(jnp.searchsorted(ends, tile_start, side="right"), N_EXPERTS - 1)
    last_used = tile_expert[jnp.maximum(n_used - 1, 0)]
    tile_expert = jnp.where(tile_start < ends[-1], tile_expert, last_used).astype(jnp.int32)
    src = jnp.zeros((n_slots,), jnp.int32).at[dest].set(jnp.arange(n, dtype=jnp.int32) // TOP_K)
    cw = jnp.zeros((n_slots,), F32).at[dest].set(top_w.reshape(-1))
    return src, cw, dest, tile_expert, n_used.reshape(1)


def _combine_kernel(yg_ref, x_ref, mod_ref, fg_ref, o_ref, *, lc, tm, final):
    d = x_ref.shape[-1]
    moe = yg_ref[0, :, 0:d].astype(F32)
    for k in range(1, TOP_K):
        moe = moe + yg_ref[0, :, k * d:(k + 1) * d].astype(F32)
    gate = _row_select(mod_ref, 5, _is_ctx_rows(tm, lc))
    xn = x_ref[0] + gate * moe
    if final:
        xn = xn * lax.rsqrt(jnp.mean(xn * xn, axis=-1, keepdims=True) + EPS) * fg_ref[...]
    o_ref[0] = xn


def _combine(yg, xa, mod, final_g, lc, final):
    b, s, d = xa.shape
    tm = _pick(s, (544, 256, 128))
    return pl.pallas_call(
        functools.partial(_combine_kernel, lc=lc, tm=tm, final=final),
        grid=(b, s // tm),
        in_specs=[
            pl.BlockSpec((1, tm, TOP_K * d), lambda i, j: (i, j, 0)),
            pl.BlockSpec((1, tm, d), lambda i, j: (i, j, 0)),
            pl.BlockSpec((1, 2, 6, d), lambda i, j: (i, 0, 0, 0)),
            pl.BlockSpec((1, d), lambda i, j: (0, 0)),
        ],
        out_specs=pl.BlockSpec((1, tm, d), lambda i, j: (i, j, 0)),
        out_shape=jax.ShapeDtypeStruct((b, s, d), F32),
        compiler_params=_cparams(("parallel", "parallel")),
        name="moe_combine",
    )(yg, xa, mod, final_g.reshape(1, d))


def kernel(x, c, ctx, c_ctx, w_ada, b_ada, norm_g, w_in, ssm_a_re, ssm_a_im, ssm_log_dt, ssm_b_re, ssm_b_im, ssm_c_re, ssm_c_im, ssm_d, ssm_w_glu, ssm_b_glu, attn_q_gain, attn_k_gain, mlstm_gate_bias, mlstm_norm_g, w_branch_ssm, w_branch_attn, w_branch_mlstm, w_out, w_router, b_router, w_gate_up, b_gate_up, w_down, b_down, final_g):
    b, lx, d = x.shape
    lc = ctx.shape[1]
    s = lc + lx
    t = b * s
    depth = w_in.shape[0]
    assert b + 1 <= 8 and lc % MLSTM_CHUNK == 0 and lx % MLSTM_CHUNK == 0

    ssm_w = ssm_d.shape[1]
    attn_w = w_branch_attn.shape[1]
    kv_w = ATTN_KV_HEADS * ATTN_HEAD_DIM
    mqk_w = MLSTM_HEADS * MLSTM_QK_DIM
    mv_w = MLSTM_HEADS * MLSTM_V_DIM
    n_gates = 4 * MLSTM_HEADS
    col_q = ssm_w
    col_k = col_q + attn_w
    col_v = col_k + kv_w
    col_mq = col_v + kv_w
    col_mk = col_mq + mqk_w
    col_mv = col_mk + mqk_w
    col_mo = col_mv + mv_w
    col_mg = col_mo + mv_w
    col_bg = col_mg + n_gates
    main_pad = (-(col_mg + n_gates)) % 256

    xa = jnp.concatenate([ctx, x], axis=1)
    cond8 = jnp.zeros((8, d), F32).at[0].set(c_ctx).at[1:1 + b].set(c)
    mods = _ada_modulation(cond8, w_ada, b_ada).reshape(depth, 8, 6, d)
    cos, sin = _rope_tables(lc, lx)

    for l in range(depth):
        last = l == depth - 1
        mod = jnp.stack([jnp.broadcast_to(mods[l, 0], (b, 6, d)), mods[l, 1:1 + b]], axis=1)

        h = _norm_mod(xa, norm_g[l, 0], mod, lc, 0).reshape(t, d)
        w_main = jnp.concatenate(
            [w_in[l, :, :col_bg], jnp.zeros((d, main_pad), F32)], axis=1).astype(BF16)
        p_main = _gemm(h, w_main, F32)
        gates = _gemm(h, w_in[l, :, col_bg:].astype(BF16), BF16, act="sigmoid")
        p3 = p_main.reshape(b, s, -1)

        ops = _s5_operators(ssm_a_re[l], ssm_a_im[l], ssm_log_dt[l], ssm_b_re[l], ssm_b_im[l],
                            ssm_c_re[l], ssm_c_im[l], ssm_d[l])
        ys = _s5_mix(p3[:, :, :ssm_w], ops, b, lc)

        qh, kh, vh = _qk_prep(p3, cos, sin, attn_q_gain[l], attn_k_gain[l], col_q, col_k, col_v)
        att = _attention(qh, kh, vh, lc)

        hfb = _mlstm(p3, mlstm_gate_bias[l], lc, col_mq, col_mk, col_mv, col_mg)

        y = _merge(ys.reshape(t, ssm_w), att.reshape(t, attn_w), hfb.reshape(2, t, mv_w), p_main, gates,
                   ssm_w_glu[l].astype(BF16), ssm_b_glu[l], mlstm_norm_g[l],
                   w_branch_ssm[l].astype(BF16), w_branch_attn[l].astype(BF16),
                   w_branch_mlstm[l].astype(BF16), col_mo)
        xa = _out_proj(y.reshape(b, s, d), w_out[l].astype(BF16), xa, mod, lc)

        h2, top_idx, top_w = _norm_router(xa, norm_g[l, 1], mod, w_router[l], b_router[l], lc)
        n_slots = t * TOP_K + N_EXPERTS * EXPERT_TILE
        src, cw, dest, tile_expert, n_used = _route(top_idx, top_w, n_slots)
        xs = jnp.take(h2.reshape(t, d), src, axis=0)
        wgl = jnp.concatenate([w_gate_up[l][:, :, 0::2], w_gate_up[l][:, :, 1::2]], axis=-1).astype(BF16)
        bgl = jnp.concatenate([b_gate_up[l][:, 0::2], b_gate_up[l][:, 1::2]], axis=-1)[:, None, :]
        ysorted = _experts(xs, cw[:, None], tile_expert, n_used, wgl, bgl,
                           w_down[l].astype(BF16), b_down[l][:, None, :])
        yg = jnp.take(ysorted, dest, axis=0).reshape(b, s, TOP_K * d)
        xa = _combine(yg, xa, mod, final_g, lc, last)

    return xa[:, lc:, :]
```

```python
import functools
import math

import jax
import jax.numpy as jnp
from jax import lax
from jax.experimental import pallas as pl
from jax.experimental.pallas import tpu as pltpu

F32 = jnp.float32
BF16 = jnp.bfloat16

GRID_W = 64
EPS = 1e-6
SSM_GROUP = 16
SSM_STATE = 64
SSM_CHUNK = 16
ATTN_HEAD_DIM = 128
ATTN_KV_HEADS = 2
ATTN_Q_PER_KV = 4
ROPE_THETA = 10000.0
MLSTM_HEADS = 4
MLSTM_QK_DIM = 64
MLSTM_V_DIM = 128
MLSTM_CHUNK = 64
N_EXPERTS = 32
TOP_K = 4
SWIGLU_LIMIT = 7.0
SWIGLU_ALPHA = 1.702

LANES = 128
MXU_COLS = 256
VMEM_LIMIT_BYTES = 56 * 1024 * 1024
EXPERT_TILE = 256


def _cparams(sem):
    return pltpu.CompilerParams(dimension_semantics=sem, vmem_limit_bytes=VMEM_LIMIT_BYTES)


def _pick(n, cands):
    for c in cands:
        if n % c == 0:
            return c
    raise ValueError(f"no tile for {n} in {cands}")


def _ada_kernel(c_ref, w_ref, b_ref, o_ref):
    c = c_ref[...]
    s = c * jax.nn.sigmoid(c)
    o_ref[0] = jnp.dot(s, w_ref[0], preferred_element_type=F32) + b_ref[0]


def _ada_modulation(cond8, w_ada, b_ada):
    nl, d, n = w_ada.shape
    tn = _pick(n, (1024, 512, 256, 128))
    return pl.pallas_call(
        _ada_kernel,
        grid=(nl, n // tn),
        in_specs=[
            pl.BlockSpec((8, d), lambda l, j: (0, 0)),
            pl.BlockSpec((1, d, tn), lambda l, j: (l, 0, j)),
            pl.BlockSpec((1, 1, tn), lambda l, j: (l, 0, j)),
        ],
        out_specs=pl.BlockSpec((1, 8, tn), lambda l, j: (l, 0, j)),
        out_shape=jax.ShapeDtypeStruct((nl, 8, n), F32),
        compiler_params=_cparams(("parallel", "parallel")),
        name="ada_modulation",
    )(cond8, w_ada, b_ada.reshape(nl, 1, n))


def _row_select(mod_ref, k, is_ctx):
    return jnp.where(is_ctx, mod_ref[0, 0, k:k + 1, :], mod_ref[0, 1, k:k + 1, :])


def _modulated_norm(x, g, mod_ref, k_shift, is_ctx):
    ms = jnp.mean(x * x, axis=-1, keepdims=True)
    y = x * lax.rsqrt(ms + EPS) * g
    return y * (1.0 + _row_select(mod_ref, k_shift + 1, is_ctx)) + _row_select(mod_ref, k_shift, is_ctx)


def _is_ctx_rows(tm, lc):
    pos = pl.program_id(1) * tm + lax.broadcasted_iota(jnp.int32, (tm, 1), 0)
    return pos < lc


def _norm_mod_kernel(x_ref, g_ref, mod_ref, o_ref, *, lc, tm, k_shift):
    h = _modulated_norm(x_ref[0], g_ref[...], mod_ref, k_shift, _is_ctx_rows(tm, lc))
    o_ref[0] = h.astype(o_ref.dtype)


def _norm_mod(xa, g, mod, lc, k_shift):
    b, s, d = xa.shape
    tm = _pick(s, (544, 256, 128))
    return pl.pallas_call(
        functools.partial(_norm_mod_kernel, lc=lc, tm=tm, k_shift=k_shift),
        grid=(b, s // tm),
        in_specs=[
            pl.BlockSpec((1, tm, d), lambda i, j: (i, j, 0)),
            pl.BlockSpec((1, d), lambda i, j: (0, 0)),
            pl.BlockSpec((1, 2, 6, d), lambda i, j: (i, 0, 0, 0)),
        ],
        out_specs=pl.BlockSpec((1, tm, d), lambda i, j: (i, j, 0)),
        out_shape=jax.ShapeDtypeStruct((b, s, d), BF16),
        compiler_params=_cparams(("parallel", "parallel")),
        name="norm_mod",
    )(xa, g.reshape(1, d), mod)


def _norm_router_kernel(x_ref, g_ref, mod_ref, wr_ref, br_ref, h_ref, idx_ref, w_ref, *, lc, tm):
    h = _modulated_norm(x_ref[0], g_ref[...], mod_ref, 3, _is_ctx_rows(tm, lc))
    h_ref[0] = h.astype(h_ref.dtype)
    logits = jnp.dot(h, wr_ref[...], preferred_element_type=F32,
                     precision=lax.Precision.HIGHEST) + br_ref[...]
    lane = lax.broadcasted_iota(jnp.int32, logits.shape, 1).astype(F32)
    vals, idxs = [], []
    cur = logits
    for _ in range(TOP_K):
        mx = jnp.max(cur, axis=-1, keepdims=True)
        ix = jnp.min(jnp.where(cur == mx, lane, float(N_EXPERTS)), axis=-1, keepdims=True)
        vals.append(mx)
        idxs.append(ix)
        cur = jnp.where(lane == ix, -jnp.inf, cur)
    k_iota = lax.broadcasted_iota(jnp.int32, (logits.shape[0], TOP_K), 1)
    top = jnp.zeros((logits.shape[0], TOP_K), F32)
    top_i = jnp.zeros((logits.shape[0], TOP_K), F32)
    for k in range(TOP_K):
        top = jnp.where(k_iota == k, vals[k], top)
        top_i = jnp.where(k_iota == k, idxs[k], top_i)
    e = jnp.exp(top - vals[0])
    w_ref[0] = e / jnp.sum(e, axis=-1, keepdims=True)
    idx_ref[0] = top_i.astype(jnp.int32)


def _norm_router(xa, g, mod, w_router, b_router, lc):
    b, s, d = xa.shape
    tm = _pick(s, (544, 256, 128))
    ne = w_router.shape[1]
    return pl.pallas_call(
        functools.partial(_norm_router_kernel, lc=lc, tm=tm),
        grid=(b, s // tm),
        in_specs=[
            pl.BlockSpec((1, tm, d), lambda i, j: (i, j, 0)),
            pl.BlockSpec((1, d), lambda i, j: (0, 0)),
            pl.BlockSpec((1, 2, 6, d), lambda i, j: (i, 0, 0, 0)),
            pl.BlockSpec((d, ne), lambda i, j: (0, 0)),
            pl.BlockSpec((1, ne), lambda i, j: (0, 0)),
        ],
        out_specs=[
            pl.BlockSpec((1, tm, d), lambda i, j: (i, j, 0)),
            pl.BlockSpec((1, tm, TOP_K), lambda i, j: (i, j, 0)),
            pl.BlockSpec((1, tm, TOP_K), lambda i, j: (i, j, 0)),
        ],
        out_shape=[
            jax.ShapeDtypeStruct((b, s, d), BF16),
            jax.ShapeDtypeStruct((b, s, TOP_K), jnp.int32),
            jax.ShapeDtypeStruct((b, s, TOP_K), F32),
        ],
        compiler_params=_cparams(("parallel", "parallel")),
        name="norm_router",
    )(xa, g.reshape(1, d), mod, w_router, b_router.reshape(1, ne))


def _gemm_kernel(a_ref, w_ref, o_ref, *, act):
    acc = jnp.dot(a_ref[...], w_ref[...], preferred_element_type=F32)
    if act == "sigmoid":
        acc = jax.nn.sigmoid(acc)
    o_ref[...] = acc.astype(o_ref.dtype)


def _gemm(a, w, out_dtype, act=None, tn_cands=(1920, 1024, 512, 256, 128)):
    t, k = a.shape
    n = w.shape[1]
    tm = _pick(t, (1024, 768, 512, 256, 128))
    tn = _pick(n, tn_cands)
    return pl.pallas_call(
        functools.partial(_gemm_kernel, act=act),
        grid=(t // tm, n // tn),
        in_specs=[
            pl.BlockSpec((tm, k), lambda i, j: (i, 0)),
            pl.BlockSpec((k, tn), lambda i, j: (0, j)),
        ],
        out_specs=pl.BlockSpec((tm, tn), lambda i, j: (i, j)),
        out_shape=jax.ShapeDtypeStruct((t, n), out_dtype),
        compiler_params=_cparams(("parallel", "parallel")),
        name="gemm_" + (act or "plain"),
    )(a, w)


def _s5_operators(a_re, a_im, log_dt, b_re, b_im, c_re, c_im, d_skip):
    tc = SSM_CHUNK
    hp = lax.Precision.HIGHEST
    k_idx = jnp.arange(tc + 1, dtype=F32)
    lag = jnp.arange(tc)[None, :] - jnp.arange(tc)[:, None]
    ms, bcs, ccs, a16s = [], [], [], []
    for dirn in (0, 1):
        lam = lax.complex(a_re[dirn].astype(F32), a_im[dirn].astype(F32))
        dt = jnp.exp(log_dt[dirn].astype(F32))[:, None]
        pw = jnp.exp((lam * dt)[None] * k_idx[:, None, None])
        a_bar = pw[1]
        b_bar = ((a_bar - 1.0) / lam)[..., None] * lax.complex(b_re[dirn].astype(F32), b_im[dirn].astype(F32))
        c_mat = lax.complex(c_re[dirn].astype(F32), c_im[dirn].astype(F32))
        k_lag = jnp.real(jnp.einsum("ghp,kgp,gpj->kghj", c_mat, pw[:tc], b_bar, precision=hp))
        if dirn == 0:
            dist, valid = lag, lag >= 0
            end_pow = (tc - 1) - jnp.arange(tc)
            out_pow = jnp.arange(tc) + 1
        else:
            dist, valid = -lag, lag <= 0
            end_pow = jnp.arange(tc)
            out_pow = tc - jnp.arange(tc)
        kk = k_lag[jnp.clip(dist, 0, tc - 1)]
        kk = jnp.where(valid[:, :, None, None, None], kk, 0.0)
        g = kk.shape[2]
        ms.append(jnp.transpose(kk, (2, 0, 4, 1, 3)).reshape(g, tc * SSM_GROUP, tc * SSM_GROUP))
        bc = pw[end_pow][:, :, :, None] * b_bar[None]
        bcs.append(jnp.transpose(bc, (1, 0, 3, 2)).reshape(g, tc * SSM_GROUP, SSM_STATE))
        cw = c_mat[None] * pw[out_pow][:, :, None, :]
        ccs.append(jnp.transpose(cw, (1, 3, 0, 2)).reshape(g, SSM_STATE, tc * SSM_GROUP))
        a16s.append(pw[tc])
    m = ms[0] + ms[1]
    bc = jnp.concatenate([jnp.real(bcs[0]), jnp.real(bcs[1]), jnp.imag(bcs[0]), jnp.imag(bcs[1])], axis=-1)
    cc = jnp.concatenate([jnp.real(ccs[0]), jnp.real(ccs[1]), -jnp.imag(ccs[0]), -jnp.imag(ccs[1])], axis=1)
    a16 = jnp.stack([jnp.concatenate([jnp.real(a16s[0]), jnp.real(a16s[1])], axis=-1),
                     jnp.concatenate([jnp.imag(a16s[0]), jnp.imag(a16s[1])], axis=-1)], axis=1)
    g = m.shape[0]
    dtile = jnp.tile(d_skip.astype(F32).reshape(g, 1, SSM_GROUP), (1, 1, tc))
    return m.astype(BF16), bc.astype(BF16), cc.astype(BF16), a16.astype(F32), dtile


def _s5_kernel(u_ref, m_ref, bc_ref, cc_ref, a_ref, d_ref, y_ref, vre_scr, vim_scr, hf_re, hb_re, hf_im, hb_im,
               *, nb, nchunk, nctx):
    p = SSM_STATE
    u = u_ref[0]
    ub = u.astype(BF16)
    v = jnp.dot(ub, bc_ref[0], preferred_element_type=F32)
    vre_scr[...] = v[:, 0:2 * p]
    vim_scr[...] = v[:, 2 * p:4 * p]
    ar = a_ref[0, 0:1, :]
    ai = a_ref[0, 1:2, :]
    fwd_lanes = lax.broadcasted_iota(jnp.int32, (nb, 2 * p), 1) < p
    hre = jnp.zeros((nb, 2 * p), F32)
    him = jnp.zeros((nb, 2 * p), F32)
    for s in range(nchunk):
        cf = s
        cb = (nctx - 1 - s) if s < nctx else (nchunk - 1 - (s - nctx))
        rf = pl.ds(cf, nb, stride=nchunk)
        rb = pl.ds(cb, nb, stride=nchunk)
        hf_re[rf, :] = hre
        hb_re[rb, :] = hre
        hf_im[rf, :] = him
        hb_im[rb, :] = him
        vre = jnp.where(fwd_lanes, vre_scr[rf, :], vre_scr[rb, :])
        vim = jnp.where(fwd_lanes, vim_scr[rf, :], vim_scr[rb, :])
        hre, him = ar * hre - ai * him + vre, ar * him + ai * hre + vim
    is_fwd = lax.broadcasted_iota(jnp.int32, (u.shape[0], 2 * p), 1) < p
    hp = jnp.concatenate([jnp.where(is_fwd, hf_re[...], hb_re[...]),
                          jnp.where(is_fwd, hf_im[...], hb_im[...])], axis=1)
    y = jnp.dot(ub, m_ref[0], preferred_element_type=F32)
    y = y + jnp.dot(hp.astype(BF16), cc_ref[0], preferred_element_type=F32)
    y_ref[0] = y + u * d_ref[0]


def _s5_mix(u, ops, nb, lc):
    m, bc, cc, a16, dtile = ops
    b, s, w = u.shape
    g = w // SSM_GROUP
    tc = SSM_CHUNK
    nchunk = s // tc
    r = nchunk * nb
    k = tc * SSM_GROUP
    ug = u.reshape(b * s * SSM_GROUP, g).T.reshape(g, r, k)
    yg = pl.pallas_call(
        functools.partial(_s5_kernel, nb=nb, nchunk=nchunk, nctx=lc // tc),
        grid=(g,),
        in_specs=[
            pl.BlockSpec((1, r, k), lambda i: (i, 0, 0)),
            pl.BlockSpec((1, k, k), lambda i: (i, 0, 0)),
            pl.BlockSpec((1, k, 4 * SSM_STATE), lambda i: (i, 0, 0)),
            pl.BlockSpec((1, 4 * SSM_STATE, k), lambda i: (i, 0, 0)),
            pl.BlockSpec((1, 2, 2 * SSM_STATE), lambda i: (i, 0, 0)),
            pl.BlockSpec((1, 1, k), lambda i: (i, 0, 0)),
        ],
        out_specs=pl.BlockSpec((1, r, k), lambda i: (i, 0, 0)),
        out_shape=jax.ShapeDtypeStruct((g, r, k), F32),
        scratch_shapes=[pltpu.VMEM((r, 2 * SSM_STATE), F32)] * 6,
        compiler_params=_cparams(("parallel",)),
        name="s5_mix",
    )(ug, m, bc, cc, a16, dtile)
    return yg.reshape(g, b * s * SSM_GROUP).T.reshape(b, s, w)


def _rope_tables(lc, lx):
    n = ATTN_HEAD_DIM // 4
    freq = ROPE_THETA ** (-jnp.arange(n, dtype=F32) / n)
    t = jnp.arange(lx, dtype=jnp.int32)
    rows = (t // GRID_W).astype(F32)[:, None] * freq[None, :]
    cols = (t % GRID_W).astype(F32)[:, None] * freq[None, :]
    cos = jnp.concatenate([jnp.cos(rows), jnp.cos(rows), jnp.cos(cols), jnp.cos(cols)], axis=-1)
    sin = jnp.concatenate([-jnp.sin(rows), jnp.sin(rows), -jnp.sin(cols), jnp.sin(cols)], axis=-1)
    cos = jnp.concatenate([jnp.ones((lc, ATTN_HEAD_DIM), F32), cos], axis=0)
    sin = jnp.concatenate([jnp.zeros((lc, ATTN_HEAD_DIM), F32), sin], axis=0)
    return cos, sin


def _qk_prep_kernel(q_ref, k_ref, v_ref, cos_ref, sin_ref, qg_ref, kg_ref, qo_ref, ko_ref, vo_ref):
    hd = ATTN_HEAD_DIM
    cos = cos_ref[...]
    sin = sin_ref[...]
    lane = lax.broadcasted_iota(jnp.int32, cos.shape, 1)
    first = (lane % (hd // 2)) < (hd // 4)

    def prep(x, g):
        y = x * lax.rsqrt(jnp.mean(x * x, axis=-1, keepdims=True) + EPS) * g
        partner = jnp.where(first, pltpu.roll(y, hd - hd // 4, 1), pltpu.roll(y, hd // 4, 1))
        return y * cos + partner * sin

    for h in range(ATTN_Q_PER_KV):
        sl = slice(h * hd, (h + 1) * hd)
        qo_ref[0, :, sl] = prep(q_ref[0, :, sl], qg_ref[...]).astype(qo_ref.dtype)
    ko_ref[0] = prep(k_ref[0], kg_ref[...]).astype(ko_ref.dtype)
    vo_ref[0] = v_ref[0].astype(vo_ref.dtype)


def _qk_prep(p3, cos, sin, q_gain, k_gain, col_q, col_k, col_v):
    b, s, _ = p3.shape
    hd = ATTN_HEAD_DIM
    qw = ATTN_Q_PER_KV * hd
    tq = _pick(s, (544, 256, 128))
    return pl.pallas_call(
        _qk_prep_kernel,
        grid=(b, s // tq, ATTN_KV_HEADS),
        in_specs=[
            pl.BlockSpec((1, tq, qw), lambda i, j, kv: (i, j, col_q // qw + kv)),
            pl.BlockSpec((1, tq, hd), lambda i, j, kv: (i, j, col_k // hd + kv)),
            pl.BlockSpec((1, tq, hd), lambda i, j, kv: (i, j, col_v // hd + kv)),
            pl.BlockSpec((tq, hd), lambda i, j, kv: (j, 0)),
            pl.BlockSpec((tq, hd), lambda i, j, kv: (j, 0)),
            pl.BlockSpec((1, hd), lambda i, j, kv: (0, 0)),
            pl.BlockSpec((1, hd), lambda i, j, kv: (0, 0)),
        ],
        out_specs=[
            pl.BlockSpec((1, tq, qw), lambda i, j, kv: (i, j, kv)),
            pl.BlockSpec((1, tq, hd), lambda i, j, kv: (i, j, kv)),
            pl.BlockSpec((1, tq, hd), lambda i, j, kv: (i, j, kv)),
        ],
        out_shape=[
            jax.ShapeDtypeStruct((b, s, ATTN_KV_HEADS * qw), BF16),
            jax.ShapeDtypeStruct((b, s, ATTN_KV_HEADS * hd), BF16),
            jax.ShapeDtypeStruct((b, s, ATTN_KV_HEADS * hd), BF16),
        ],
        compiler_params=_cparams(("parallel", "parallel", "parallel")),
        name="qk_prep",
    )(p3, p3, p3, cos, sin, q_gain.reshape(1, hd), k_gain.reshape(1, hd))


def _attn_kernel(q_ref, k_ref, v_ref, o_ref, *, lc, tq):
    hd = ATTN_HEAD_DIM
    c = (hd ** -0.5) * math.log2(math.e)

    def run(kk, vv):
        def scores(h):
            return lax.dot_general(q_ref[0, :, h * hd:(h + 1) * hd], kk, (((1,), (1,)), ((), ())),
                                   preferred_element_type=F32)

        s_next = scores(0)
        for h in range(ATTN_Q_PER_KV):
            s = s_next
            if h + 1 < ATTN_Q_PER_KV:
                s_next = scores(h + 1)
            m = jnp.max(s, axis=-1, keepdims=True)
            p = jnp.exp2((s - m) * c)
            l = jnp.sum(p, axis=-1, keepdims=True)
            o = jnp.dot(p.astype(BF16), vv, preferred_element_type=F32)
            o_ref[0, :, h * hd:(h + 1) * hd] = (o / l).astype(o_ref.dtype)

    is_ctx_tile = pl.program_id(2) * tq < lc

    @pl.when(jnp.logical_not(is_ctx_tile))
    def _():
        run(k_ref[0], v_ref[0])

    @pl.when(is_ctx_tile)
    def _():
        run(k_ref[0, :lc, :], v_ref[0, :lc, :])


def _attention(qh, kh, vh, lc):
    b, s, _ = qh.shape
    hd = ATTN_HEAD_DIM
    qw = ATTN_Q_PER_KV * hd
    tq = _pick(lc, (256, 128))
    assert s % tq == 0
    return pl.pallas_call(
        functools.partial(_attn_kernel, lc=lc, tq=tq),
        grid=(b, ATTN_KV_HEADS, s // tq),
        in_specs=[
            pl.BlockSpec((1, tq, qw), lambda i, kv, j: (i, j, kv)),
            pl.BlockSpec((1, s, hd), lambda i, kv, j: (i, 0, kv)),
            pl.BlockSpec((1, s, hd), lambda i, kv, j: (i, 0, kv)),
        ],
        out_specs=pl.BlockSpec((1, tq, qw), lambda i, kv, j: (i, j, kv)),
        out_shape=jax.ShapeDtypeStruct((b, s, ATTN_KV_HEADS * qw), BF16),
        compiler_params=_cparams(("parallel", "parallel", "parallel")),
        name="attention",
    )(qh, kh, vh)


def _split3(x):
    hi = x.astype(BF16)
    r = x - hi.astype(F32)
    mid = r.astype(BF16)
    lo = (r - mid.astype(F32)).astype(BF16)
    return hi, mid, lo


def _log_sigmoid(x):
    return jnp.minimum(x, 0.0) - jnp.log(1.0 + jnp.exp(-jnp.abs(x)))


def _mlstm_kernel(q_ref, k_ref, v_ref, gc_ref, gr_ref, bc_ref, br_ref, o_ref, c_scr, n_scr, m_scr):
    nh, dk, dv, cl = MLSTM_HEADS, MLSTM_QK_DIM, MLSTM_V_DIM, MLSTM_CHUNK
    nb = q_ref.shape[0]
    d = pl.program_id(0)
    c = pl.program_id(1)

    @pl.when(c == 0)
    def _():
        c_scr[...] = jnp.zeros_like(c_scr)
        n_scr[...] = jnp.zeros_like(n_scr)
        m_scr[...] = jnp.zeros_like(m_scr)

    row = lax.broadcasted_iota(jnp.int32, (cl, cl), 0)
    col = lax.broadcasted_iota(jnp.int32, (cl, cl), 1)
    sign = jnp.where(d == 0, 1, -1)
    mask = sign * (row - col) >= 0
    tm = jnp.where(mask, 1.0, 0.0).astype(BF16)
    tmt = jnp.where(sign * (col - row) >= 0, 1.0, 0.0).astype(BF16)

    def pick_c(a, k):
        return jnp.where(d == 0, a[:, k:k + 1], a[:, 2 * nh + k:2 * nh + k + 1])

    def pick_r(a, k):
        return jnp.where(d == 0, a[k:k + 1, :], a[2 * nh + k:2 * nh + k + 1, :])

    nt = (((1,), (1,)), ((), ()))
    tn = (((0,), (0,)), ((), ()))
    heads = [(bi, h) for bi in range(nb) for h in range(nh)]

    gcs = [gc_ref[bi] + bc_ref[...] for bi in range(nb)]
    grs = [gr_ref[bi, 0] + br_ref[...] for bi in range(nb)]
    lf_cs = [_log_sigmoid(g) for g in gcs]
    lf_rs = [_log_sigmoid(g) for g in grs]
    cum_cs = [sum(jnp.dot(tm, part, preferred_element_type=F32) for part in _split3(x)) for x in lf_cs]
    cum_rs = [sum(jnp.dot(part, tmt, preferred_element_type=F32) for part in _split3(x)) for x in lf_rs]
    totals = [jnp.sum(x, axis=0, keepdims=True) for x in lf_cs]

    qs = [q_ref[bi][:, h * dk:(h + 1) * dk] * (dk ** -0.5) for bi, h in heads]
    ks = [k_ref[bi][:, h * dk:(h + 1) * dk] for bi, h in heads]
    vs = [v_ref[bi][:, h * dv:(h + 1) * dv] for bi, h in heads]
    qbs = [x.astype(BF16) for x in qs]
    kbs = [x.astype(BF16) for x in ks]
    m_sts = [m_scr[i][:, 0:1] for i in range(len(heads))]
    c_sts = [c_scr[i] for i in range(len(heads))]
    n_sts = [n_scr[i] for i in range(len(heads))]

    s_raw = [lax.dot_general(qbs[i], kbs[i], nt, preferred_element_type=F32) for i in range(len(heads))]
    cqs = [lax.dot_general(qbs[i], c_sts[i].astype(BF16), nt, preferred_element_type=F32)
           for i in range(len(heads))]

    cf_c = [pick_c(cum_cs[bi], nh + h) for bi, h in heads]
    cf_r = [pick_r(cum_rs[bi], nh + h) for bi, h in heads]
    li_c = [pick_c(gcs[bi], h) for bi, h in heads]
    li_r = [pick_r(grs[bi], h) for bi, h in heads]
    tot = [pick_c(totals[bi], nh + h) for bi, h in heads]

    log_d = [jnp.where(mask, cf_c[i] - cf_r[i] + li_r[i], -jnp.inf) for i in range(len(heads))]
    log_inter = [cf_c[i] + m_sts[i] for i in range(len(heads))]
    m_row = [jnp.maximum(log_inter[i], jnp.max(log_d[i], axis=-1, keepdims=True)) for i in range(len(heads))]
    s = [s_raw[i] * jnp.exp(log_d[i] - m_row[i]) for i in range(len(heads))]
    w_inter = [jnp.exp(log_inter[i] - m_row[i]) for i in range(len(heads))]
    sv = [jnp.dot(s[i].astype(BF16), vs[i].astype(BF16), preferred_element_type=F32) for i in range(len(heads))]
    den = [jnp.sum(s[i], axis=-1, keepdims=True)
           + w_inter[i] * jnp.sum(qs[i] * n_sts[i], axis=-1, keepdims=True) for i in range(len(heads))]
    for i, (bi, h) in enumerate(heads):
        num = sv[i] + w_inter[i] * cqs[i]
        o_ref[0, bi, :, h * dv:(h + 1) * dv] = num / jnp.maximum(jnp.abs(den[i]), jnp.exp(-m_row[i]))

    log_w = [tot[i] - cf_c[i] + li_c[i] for i in range(len(heads))]
    m_new = [jnp.maximum(tot[i] + m_sts[i], jnp.max(log_w[i], axis=0, keepdims=True)) for i in range(len(heads))]
    decay = [jnp.exp(tot[i] + m_sts[i] - m_new[i]) for i in range(len(heads))]
    w = [jnp.exp(log_w[i] - m_new[i]) for i in range(len(heads))]
    kv = [lax.dot_general((w[i] * vs[i]).astype(BF16), kbs[i], tn, preferred_element_type=F32)
          for i in range(len(heads))]
    for i in range(len(heads)):
        c_scr[i] = decay[i] * c_sts[i] + kv[i]
        n_scr[i] = decay[i] * n_sts[i] + jnp.sum(w[i] * ks[i], axis=0, keepdims=True)
        m_scr[i] = jnp.broadcast_to(m_new[i], (1, LANES))


def _mlstm(p3, gate_bias, lc, col_q, col_k, col_v, col_g):
    b, s, _ = p3.shape
    nh, dk, dv, cl = MLSTM_HEADS, MLSTM_QK_DIM, MLSTM_V_DIM, MLSTM_CHUNK
    nc = s // cl
    nctx = lc // cl
    ng = 4 * nh
    g_rows = jnp.transpose(p3[:, :, col_g:col_g + ng].reshape(b, nc, cl, ng), (0, 1, 3, 2))
    bias = gate_bias.astype(F32).reshape(ng)

    def chunk(d, c):
        bwd = jnp.where(c < nctx, nctx - 1 - c, nc - 1 - (c - nctx))
        return jnp.where(d == 0, c, bwd)

    return pl.pallas_call(
        _mlstm_kernel,
        grid=(2, nc),
        in_specs=[
            pl.BlockSpec((b, cl, nh * dk), lambda d, c: (0, chunk(d, c), col_q // (nh * dk))),
            pl.BlockSpec((b, cl, nh * dk), lambda d, c: (0, chunk(d, c), col_k // (nh * dk))),
            pl.BlockSpec((b, cl, nh * dv), lambda d, c: (0, chunk(d, c), col_v // (nh * dv))),
            pl.BlockSpec((b, cl, LANES), lambda d, c: (0, chunk(d, c), col_g // LANES)),
            pl.BlockSpec((b, 1, ng, cl), lambda d, c: (0, chunk(d, c), 0, 0)),
            pl.BlockSpec((1, LANES), lambda d, c: (0, 0)),
            pl.BlockSpec((ng, 1), lambda d, c: (0, 0)),
        ],
        out_specs=pl.BlockSpec((1, b, cl, nh * dv), lambda d, c: (d, 0, chunk(d, c), 0)),
        out_shape=jax.ShapeDtypeStruct((2, b, s, nh * dv), F32),
        scratch_shapes=[pltpu.VMEM((b * nh, dv, dk), F32), pltpu.VMEM((b * nh, 1, dk), F32),
                        pltpu.VMEM((b * nh, 1, LANES), F32)],
        compiler_params=_cparams(("arbitrary", "arbitrary")),
        name="mlstm",
    )(p3, p3, p3, p3, g_rows, jnp.pad(bias, (0, LANES - ng)).reshape(1, LANES), bias.reshape(ng, 1))


def _gelu_tanh(x):
    return 0.5 * x * (1.0 + jnp.tanh(math.sqrt(2.0 / math.pi) * (x + 0.044715 * (x * x * x))))


def _merge_kernel(ys_ref, att_ref, hf_ref, hb_ref, og_ref, gl_ref, wglu_ref, bglu_ref, ng_ref,
                  wbs_ref, wba_ref, wbm_ref, y_ref):
    d = y_ref.shape[-1]
    z = _gelu_tanh(ys_ref[...])
    s5 = z * jax.nn.sigmoid(jnp.dot(z.astype(BF16), wglu_ref[...], preferred_element_type=F32) + bglu_ref[...])
    hsum = hf_ref[0] + hb_ref[0]
    dv = MLSTM_V_DIM
    parts = []
    for h in range(MLSTM_HEADS):
        blk = hsum[:, h * dv:(h + 1) * dv]
        parts.append(blk * lax.rsqrt(jnp.mean(blk * blk, axis=-1, keepdims=True) + EPS))
    ml = jnp.concatenate(parts, axis=-1) * ng_ref[...] * jax.nn.sigmoid(og_ref[...])
    y = gl_ref[:, 0:d].astype(F32) * jnp.dot(s5.astype(BF16), wbs_ref[...], preferred_element_type=F32)
    y = y + gl_ref[:, d:2 * d].astype(F32) * jnp.dot(att_ref[...], wba_ref[...], preferred_element_type=F32)
    y = y + gl_ref[:, 2 * d:3 * d].astype(F32) * jnp.dot(ml.astype(BF16), wbm_ref[...], preferred_element_type=F32)
    y_ref[...] = y.astype(y_ref.dtype)


def _merge(ys, att, hfb, p_main, gates, wglu, bglu, ng, wbs, wba, wbm, col_o):
    t, ws = ys.shape
    wa = att.shape[1]
    wm = hfb.shape[-1]
    d = wbs.shape[1]
    tm = _pick(t, (256, 128))
    full = lambda shape: pl.BlockSpec(shape, lambda i: (0,) * len(shape))
    return pl.pallas_call(
        _merge_kernel,
        grid=(t // tm,),
        in_specs=[
            pl.BlockSpec((tm, ws), lambda i: (i, 0)),
            pl.BlockSpec((tm, wa), lambda i: (i, 0)),
            pl.BlockSpec((1, tm, wm), lambda i: (0, i, 0)),
            pl.BlockSpec((1, tm, wm), lambda i: (1, i, 0)),
            pl.BlockSpec((tm, wm), lambda i: (i, col_o // wm)),
            pl.BlockSpec((tm, 3 * d), lambda i: (i, 0)),
            full((ws, ws)), full((1, ws)), full((1, wm)),
            full((ws, d)), full((wa, d)), full((wm, d)),
        ],
        out_specs=pl.BlockSpec((tm, d), lambda i: (i, 0)),
        out_shape=jax.ShapeDtypeStruct((t, d), BF16),
        compiler_params=_cparams(("parallel",)),
        name="branch_merge",
    )(ys, att, hfb, hfb, p_main, gates, wglu, bglu.reshape(1, ws), ng.reshape(1, wm), wbs, wba, wbm)


def _out_proj_kernel(y_ref, w_ref, x_ref, mod_ref, o_ref, *, lc, tm):
    out = jnp.dot(y_ref[0], w_ref[...], preferred_element_type=F32)
    gate = _row_select(mod_ref, 2, _is_ctx_rows(tm, lc))
    o_ref[0] = x_ref[0] + gate * out


def _out_proj(y3, w_out, xa, mod, lc):
    b, s, d = xa.shape
    tm = _pick(s, (544, 256, 128))
    return pl.pallas_call(
        functools.partial(_out_proj_kernel, lc=lc, tm=tm),
        grid=(b, s // tm),
        in_specs=[
            pl.BlockSpec((1, tm, d), lambda i, j: (i, j, 0)),
            pl.BlockSpec((d, d), lambda i, j: (0, 0)),
            pl.BlockSpec((1, tm, d), lambda i, j: (i, j, 0)),
            pl.BlockSpec((1, 2, 6, d), lambda i, j: (i, 0, 0, 0)),
        ],
        out_specs=pl.BlockSpec((1, tm, d), lambda i, j: (i, j, 0)),
        out_shape=jax.ShapeDtypeStruct((b, s, d), F32),
        compiler_params=_cparams(("parallel", "parallel")),
        name="out_proj_residual",
    )(y3, w_out, xa, mod)


def _pair_split_matrix():
    half = MXU_COLS // 2
    n = jnp.arange(MXU_COLS)
    src = jnp.where(n < half, 2 * n, 2 * (n - half) + 1)
    return (jnp.arange(MXU_COLS)[:, None] == src[None, :]).astype(BF16)


def _pair_split_index(width):
    half = MXU_COLS // 2
    n = jnp.arange(width)
    blk, r = n // MXU_COLS, n % MXU_COLS
    return blk * MXU_COLS + jnp.where(r < half, 2 * r, 2 * (r - half) + 1)


def _split_pairs_kernel(w_ref, p_ref, o_ref):
    for cb in range(w_ref.shape[-1] // MXU_COLS):
        sl = slice(cb * MXU_COLS, (cb + 1) * MXU_COLS)
        o_ref[0, :, sl] = jnp.dot(w_ref[0, :, sl].astype(BF16), p_ref[...],
                                  preferred_element_type=F32).astype(o_ref.dtype)


def _split_pairs(w):
    ne, d, n = w.shape
    tr = _pick(d, (512, 256, 128))
    return pl.pallas_call(
        _split_pairs_kernel,
        grid=(ne, d // tr),
        in_specs=[
            pl.BlockSpec((1, tr, n), lambda e, i: (e, i, 0)),
            pl.BlockSpec((MXU_COLS, MXU_COLS), lambda e, i: (0, 0)),
        ],
        out_specs=pl.BlockSpec((1, tr, n), lambda e, i: (e, i, 0)),
        out_shape=jax.ShapeDtypeStruct((ne, d, n), BF16),
        compiler_params=_cparams(("parallel", "parallel")),
        name="split_gate_linear",
    )(w, _pair_split_matrix())


def _expert_kernel(te_ref, nu_ref, x_ref, wgl_ref, bgl_ref, wd_ref, bd_ref, o_ref):
    i = pl.program_id(0)
    half = MXU_COLS // 2

    @pl.when(i < nu_ref[0])
    def _():
        gu = jnp.dot(x_ref[...], wgl_ref[0], preferred_element_type=F32) + bgl_ref[0]
        nblk = gu.shape[1] // MXU_COLS
        gate = jnp.concatenate([gu[:, cb * MXU_COLS:cb * MXU_COLS + half] for cb in range(nblk)], axis=1)
        lin = jnp.concatenate([gu[:, cb * MXU_COLS + half:(cb + 1) * MXU_COLS] for cb in range(nblk)], axis=1)
        gate = jnp.minimum(gate, SWIGLU_LIMIT)
        lin = jnp.clip(lin, -SWIGLU_LIMIT, SWIGLU_LIMIT)
        act = gate * jax.nn.sigmoid(SWIGLU_ALPHA * gate) * (lin + 1.0)
        y = jnp.dot(act.astype(BF16), wd_ref[0], preferred_element_type=F32) + bd_ref[0]
        o_ref[...] = y.astype(o_ref.dtype)

    @pl.when(i >= nu_ref[0])
    def _():
        o_ref[...] = jnp.zeros_like(o_ref)


def _experts(xs, tile_expert, n_used, wgl, bgl, wd, bd):
    p, d = xs.shape
    ne, _, ff2 = wgl.shape
    ff = ff2 // 2
    tm = EXPERT_TILE
    n_tiles = p // tm
    grid_spec = pltpu.PrefetchScalarGridSpec(
        num_scalar_prefetch=2,
        grid=(n_tiles,),
        in_specs=[
            pl.BlockSpec((tm, d), lambda i, te, nu: (jnp.minimum(i, nu[0] - 1), 0)),
            pl.BlockSpec((1, d, ff2), lambda i, te, nu: (te[i], 0, 0)),
            pl.BlockSpec((1, 1, ff2), lambda i, te, nu: (te[i], 0, 0)),
            pl.BlockSpec((1, ff, d), lambda i, te, nu: (te[i], 0, 0)),
            pl.BlockSpec((1, 1, d), lambda i, te, nu: (te[i], 0, 0)),
        ],
        out_specs=pl.BlockSpec((tm, d), lambda i, te, nu: (i, 0)),
    )
    return pl.pallas_call(
        _expert_kernel,
        grid_spec=grid_spec,
        out_shape=jax.ShapeDtypeStruct((p, d), BF16),
        compiler_params=_cparams(("arbitrary",)),
        name="experts",
    )(tile_expert, n_used, xs, wgl, bgl, wd, bd)


def _route(top_idx, n_slots):
    tm = EXPERT_TILE
    e = top_idx.reshape(-1)
    n = e.shape[0]
    onehot = (e[:, None] == jnp.arange(N_EXPERTS, dtype=jnp.int32)[None, :]).astype(jnp.int32)
    counts = jnp.sum(onehot, axis=0)
    rank = jnp.take_along_axis(jnp.cumsum(onehot, axis=0) - onehot, e[:, None], axis=1)[:, 0]
    padded = ((counts + tm - 1) // tm) * tm
    ends = jnp.cumsum(padded)
    dest = (ends - padded)[e] + rank
    n_used = (ends[-1] // tm).astype(jnp.int32)
    tile_start = jnp.arange(n_slots // tm, dtype=jnp.int32) * tm
    tile_expert = jnp.sum((ends[None, :] <= tile_start[:, None]).astype(jnp.int32), axis=1)
    last_used = jnp.sum((ends <= (n_used - 1) * tm).astype(jnp.int32))
    tile_expert = jnp.minimum(jnp.where(tile_start < ends[-1], tile_expert, last_used), N_EXPERTS - 1)
    src = jnp.zeros((n_slots,), jnp.int32).at[dest].set(jnp.arange(n, dtype=jnp.int32) // TOP_K)
    return src, dest, tile_expert.astype(jnp.int32), n_used.reshape(1)


def _combine_kernel(y0_ref, y1_ref, y2_ref, y3_ref, tw_ref, x_ref, mod_ref, fg_ref, o_ref, *, lc, tm, final):
    tw = tw_ref[0]
    moe = tw[:, 0:1] * y0_ref[0, 0].astype(F32)
    for k, y_ref in enumerate((y1_ref, y2_ref, y3_ref), start=1):
        moe = moe + tw[:, k:k + 1] * y_ref[0, 0].astype(F32)
    gate = _row_select(mod_ref, 5, _is_ctx_rows(tm, lc))
    xn = x_ref[0] + gate * moe
    if final:
        xn = xn * lax.rsqrt(jnp.mean(xn * xn, axis=-1, keepdims=True) + EPS) * fg_ref[...]
    o_ref[0] = xn


def _combine(yg, top_w, xa, mod, final_g, lc, final):
    b, s, d = xa.shape
    tm = _pick(s, (544, 256, 128))
    assert TOP_K == 4
    y_specs = [pl.BlockSpec((1, 1, tm, d), functools.partial(lambda i, j, k: (k, i, j, 0), k=k))
               for k in range(TOP_K)]
    return pl.pallas_call(
        functools.partial(_combine_kernel, lc=lc, tm=tm, final=final),
        grid=(b, s // tm),
        in_specs=y_specs + [
            pl.BlockSpec((1, tm, TOP_K), lambda i, j: (i, j, 0)),
            pl.BlockSpec((1, tm, d), lambda i, j: (i, j, 0)),
            pl.BlockSpec((1, 2, 6, d), lambda i, j: (i, 0, 0, 0)),
            pl.BlockSpec((1, d), lambda i, j: (0, 0)),
        ],
        out_specs=pl.BlockSpec((1, tm, d), lambda i, j: (i, j, 0)),
        out_shape=jax.ShapeDtypeStruct((b, s, d), F32),
        compiler_params=_cparams(("parallel", "parallel")),
        name="moe_combine",
    )(yg, yg, yg, yg, top_w, xa, mod, final_g.reshape(1, d))


def kernel(x, c, ctx, c_ctx, w_ada, b_ada, norm_g, w_in, ssm_a_re, ssm_a_im, ssm_log_dt, ssm_b_re, ssm_b_im, ssm_c_re, ssm_c_im, ssm_d, ssm_w_glu, ssm_b_glu, attn_q_gain, attn_k_gain, mlstm_gate_bias, mlstm_norm_g, w_branch_ssm, w_branch_attn, w_branch_mlstm, w_out, w_router, b_router, w_gate_up, b_gate_up, w_down, b_down, final_g):
    b, lx, d = x.shape
    lc = ctx.shape[1]
    s = lc + lx
    t = b * s
    depth = w_in.shape[0]
    assert b + 1 <= 8 and lc % MLSTM_CHUNK == 0 and lx % MLSTM_CHUNK == 0

    ssm_w = ssm_d.shape[1]
    attn_w = w_branch_attn.shape[1]
    kv_w = ATTN_KV_HEADS * ATTN_HEAD_DIM
    mqk_w = MLSTM_HEADS * MLSTM_QK_DIM
    mv_w = MLSTM_HEADS * MLSTM_V_DIM
    n_gates = 4 * MLSTM_HEADS
    col_q = ssm_w
    col_k = col_q + attn_w
    col_v = col_k + kv_w
    col_mq = col_v + kv_w
    col_mk = col_mq + mqk_w
    col_mv = col_mk + mqk_w
    col_mo = col_mv + mv_w
    col_mg = col_mo + mv_w
    col_bg = col_mg + n_gates
    main_pad = (-(col_mg + n_gates)) % 256

    xa = jnp.concatenate([ctx, x], axis=1)
    cond8 = jnp.zeros((8, d), F32).at[0].set(c_ctx).at[1:1 + b].set(c)
    mods = _ada_modulation(cond8, w_ada, b_ada).reshape(depth, 8, 6, d)
    cos, sin = _rope_tables(lc, lx)
    n_groups = ssm_w // SSM_GROUP

    def regroup(a, axis):
        shp = a.shape
        a = a.reshape(shp[:axis] + (n_groups, SSM_GROUP) + shp[axis + 1:])
        return jnp.swapaxes(a, axis, axis + 1).reshape(shp)

    ne, ff2 = w_gate_up.shape[1], w_gate_up.shape[3]
    wgl_all = _split_pairs(w_gate_up.reshape(depth * ne, d, ff2)).reshape(depth, ne, d, ff2)

    for l in range(depth):
        last = l == depth - 1
        mod = jnp.stack([jnp.broadcast_to(mods[l, 0], (b, 6, d)), mods[l, 1:1 + b]], axis=1)

        h = _norm_mod(xa, norm_g[l, 0], mod, lc, 0).reshape(t, d)
        w_main = jnp.concatenate(
            [regroup(w_in[l, :, :ssm_w], 1), w_in[l, :, ssm_w:col_bg], jnp.zeros((d, main_pad), F32)],
            axis=1).astype(BF16)
        p_main = _gemm(h, w_main, F32)
        gates = _gemm(h, w_in[l, :, col_bg:].astype(BF16), BF16, act="sigmoid")
        p3 = p_main.reshape(b, s, -1)

        ops = _s5_operators(ssm_a_re[l], ssm_a_im[l], ssm_log_dt[l], ssm_b_re[l], ssm_b_im[l],
                            ssm_c_re[l], ssm_c_im[l], ssm_d[l])
        ys = _s5_mix(p3[:, :, :ssm_w], ops, b, lc)

        qh, kh, vh = _qk_prep(p3, cos, sin, attn_q_gain[l], attn_k_gain[l], col_q, col_k, col_v)
        att = _attention(qh, kh, vh, lc)

        hfb = _mlstm(p3, mlstm_gate_bias[l], lc, col_mq, col_mk, col_mv, col_mg)

        y = _merge(ys.reshape(t, ssm_w), att.reshape(t, attn_w), hfb.reshape(2, t, mv_w), p_main, gates,
                   regroup(regroup(ssm_w_glu[l], 0), 1).astype(BF16), regroup(ssm_b_glu[l], 0), mlstm_norm_g[l],
                   regroup(w_branch_ssm[l], 0).astype(BF16), w_branch_attn[l].astype(BF16),
                   w_branch_mlstm[l].astype(BF16), col_mo)
        xa = _out_proj(y.reshape(b, s, d), w_out[l].astype(BF16), xa, mod, lc)

        h2, top_idx, top_w = _norm_router(xa, norm_g[l, 1], mod, w_router[l], b_router[l], lc)
        n_slots = t * TOP_K + N_EXPERTS * EXPERT_TILE
        src, dest, tile_expert, n_used = _route(top_idx, n_slots)
        xs = h2.reshape(t, d).at[src].get(mode="promise_in_bounds")
        bgl = b_gate_up[l][:, _pair_split_index(b_gate_up.shape[-1])][:, None, :]
        ysorted = _experts(xs, tile_expert, n_used, wgl_all[l], bgl,
                           w_down[l].astype(BF16), b_down[l][:, None, :])
        yg = ysorted.at[dest.reshape(t, TOP_K).T].get(mode="promise_in_bounds").reshape(TOP_K, b, s, d)
        xa = _combine(yg, top_w, xa, mod, final_g, lc, last)

    return xa[:, lc:, :]
```

```python
import functools
import math

import jax
import jax.numpy as jnp
from jax import lax
from jax.experimental import pallas as pl
from jax.experimental.pallas import tpu as pltpu

F32 = jnp.float32
BF16 = jnp.bfloat16

GRID_W = 64
EPS = 1e-6
SSM_GROUP = 16
SSM_STATE = 64
SSM_CHUNK = 16
ATTN_HEAD_DIM = 128
ATTN_KV_HEADS = 2
ATTN_Q_PER_KV = 4
ROPE_THETA = 10000.0
MLSTM_HEADS = 4
MLSTM_QK_DIM = 64
MLSTM_V_DIM = 128
MLSTM_CHUNK = 64
N_EXPERTS = 32
TOP_K = 4
SWIGLU_LIMIT = 7.0
SWIGLU_ALPHA = 1.702

LANES = 128
MXU_COLS = 256
VMEM_LIMIT_BYTES = 56 * 1024 * 1024
EXPERT_TILE = 256


def _cparams(sem):
    return pltpu.CompilerParams(dimension_semantics=sem, vmem_limit_bytes=VMEM_LIMIT_BYTES)


def _pick(n, cands):
    for c in cands:
        if n % c == 0:
            return c
    raise ValueError(f"no tile for {n} in {cands}")


def _ada_kernel(c_ref, w_ref, b_ref, o_ref):
    c = c_ref[...]
    s = c * jax.nn.sigmoid(c)
    o_ref[0] = jnp.dot(s, w_ref[0], preferred_element_type=F32) + b_ref[0]


def _ada_modulation(cond8, w_ada, b_ada):
    nl, d, n = w_ada.shape
    tn = _pick(n, (1024, 512, 256, 128))
    return pl.pallas_call(
        _ada_kernel,
        grid=(nl, n // tn),
        in_specs=[
            pl.BlockSpec((8, d), lambda l, j: (0, 0)),
            pl.BlockSpec((1, d, tn), lambda l, j: (l, 0, j)),
            pl.BlockSpec((1, 1, tn), lambda l, j: (l, 0, j)),
        ],
        out_specs=pl.BlockSpec((1, 8, tn), lambda l, j: (l, 0, j)),
        out_shape=jax.ShapeDtypeStruct((nl, 8, n), F32),
        compiler_params=_cparams(("parallel", "parallel")),
        name="ada_modulation",
    )(cond8, w_ada, b_ada.reshape(nl, 1, n))


def _row_select(mod_ref, k, is_ctx):
    return jnp.where(is_ctx, mod_ref[0, 0, k:k + 1, :], mod_ref[0, 1, k:k + 1, :])


def _modulated_norm(x, g, mod_ref, k_shift, is_ctx):
    ms = jnp.mean(x * x, axis=-1, keepdims=True)
    y = x * lax.rsqrt(ms + EPS) * g
    return y * (1.0 + _row_select(mod_ref, k_shift + 1, is_ctx)) + _row_select(mod_ref, k_shift, is_ctx)


def _is_ctx_rows(tm, lc):
    pos = pl.program_id(1) * tm + lax.broadcasted_iota(jnp.int32, (tm, 1), 0)
    return pos < lc


def _norm_mod_kernel(x_ref, g_ref, mod_ref, o_ref, *, lc, tm, k_shift):
    h = _modulated_norm(x_ref[0], g_ref[...], mod_ref, k_shift, _is_ctx_rows(tm, lc))
    o_ref[0] = h.astype(o_ref.dtype)


def _norm_mod(xa, g, mod, lc, k_shift):
    b, s, d = xa.shape
    tm = _pick(s, (544, 256, 128))
    return pl.pallas_call(
        functools.partial(_norm_mod_kernel, lc=lc, tm=tm, k_shift=k_shift),
        grid=(b, s // tm),
        in_specs=[
            pl.BlockSpec((1, tm, d), lambda i, j: (i, j, 0)),
            pl.BlockSpec((1, d), lambda i, j: (0, 0)),
            pl.BlockSpec((1, 2, 6, d), lambda i, j: (i, 0, 0, 0)),
        ],
        out_specs=pl.BlockSpec((1, tm, d), lambda i, j: (i, j, 0)),
        out_shape=jax.ShapeDtypeStruct((b, s, d), BF16),
        compiler_params=_cparams(("parallel", "parallel")),
        name="norm_mod",
    )(xa, g.reshape(1, d), mod)


def _norm_router_kernel(x_ref, g_ref, mod_ref, wr_ref, br_ref, h_ref, idx_ref, w_ref, *, lc, tm):
    h = _modulated_norm(x_ref[0], g_ref[...], mod_ref, 3, _is_ctx_rows(tm, lc))
    h_ref[0] = h.astype(h_ref.dtype)
    logits = jnp.dot(h, wr_ref[...], preferred_element_type=F32,
                     precision=lax.Precision.HIGHEST) + br_ref[...]
    lane = lax.broadcasted_iota(jnp.int32, logits.shape, 1).astype(F32)
    vals, idxs = [], []
    cur = logits
    for _ in range(TOP_K):
        mx = jnp.max(cur, axis=-1, keepdims=True)
        ix = jnp.min(jnp.where(cur == mx, lane, float(N_EXPERTS)), axis=-1, keepdims=True)
        vals.append(mx)
        idxs.append(ix)
        cur = jnp.where(lane == ix, -jnp.inf, cur)
    k_iota = lax.broadcasted_iota(jnp.int32, (logits.shape[0], TOP_K), 1)
    top = jnp.zeros((logits.shape[0], TOP_K), F32)
    top_i = jnp.zeros((logits.shape[0], TOP_K), F32)
    for k in range(TOP_K):
        top = jnp.where(k_iota == k, vals[k], top)
        top_i = jnp.where(k_iota == k, idxs[k], top_i)
    e = jnp.exp(top - vals[0])
    w_ref[0] = e / jnp.sum(e, axis=-1, keepdims=True)
    idx_ref[0] = top_i.astype(jnp.int32)


def _norm_router(xa, g, mod, w_router, b_router, lc):
    b, s, d = xa.shape
    tm = _pick(s, (544, 256, 128))
    ne = w_router.shape[1]
    return pl.pallas_call(
        functools.partial(_norm_router_kernel, lc=lc, tm=tm),
        grid=(b, s // tm),
        in_specs=[
            pl.BlockSpec((1, tm, d), lambda i, j: (i, j, 0)),
            pl.BlockSpec((1, d), lambda i, j: (0, 0)),
            pl.BlockSpec((1, 2, 6, d), lambda i, j: (i, 0, 0, 0)),
            pl.BlockSpec((d, ne), lambda i, j: (0, 0)),
            pl.BlockSpec((1, ne), lambda i, j: (0, 0)),
        ],
        out_specs=[
            pl.BlockSpec((1, tm, d), lambda i, j: (i, j, 0)),
            pl.BlockSpec((1, tm, TOP_K), lambda i, j: (i, j, 0)),
            pl.BlockSpec((1, tm, TOP_K), lambda i, j: (i, j, 0)),
        ],
        out_shape=[
            jax.ShapeDtypeStruct((b, s, d), BF16),
            jax.ShapeDtypeStruct((b, s, TOP_K), jnp.int32),
            jax.ShapeDtypeStruct((b, s, TOP_K), F32),
        ],
        compiler_params=_cparams(("parallel", "parallel")),
        name="norm_router",
    )(xa, g.reshape(1, d), mod, w_router, b_router.reshape(1, ne))


def _gemm_kernel(a_ref, w_ref, o_ref, *, act):
    acc = jnp.dot(a_ref[...], w_ref[...], preferred_element_type=F32)
    if act == "sigmoid":
        acc = jax.nn.sigmoid(acc)
    o_ref[...] = acc.astype(o_ref.dtype)


def _gemm(a, w, out_dtype, act=None, tn_cands=(1920, 1024, 512, 256, 128)):
    t, k = a.shape
    n = w.shape[1]
    tm = _pick(t, (1024, 768, 512, 256, 128))
    tn = _pick(n, tn_cands)
    return pl.pallas_call(
        functools.partial(_gemm_kernel, act=act),
        grid=(t // tm, n // tn),
        in_specs=[
            pl.BlockSpec((tm, k), lambda i, j: (i, 0)),
            pl.BlockSpec((k, tn), lambda i, j: (0, j)),
        ],
        out_specs=pl.BlockSpec((tm, tn), lambda i, j: (i, j)),
        out_shape=jax.ShapeDtypeStruct((t, n), out_dtype),
        compiler_params=_cparams(("parallel", "parallel")),
        name="gemm_" + (act or "plain"),
    )(a, w)


def _s5_operators(a_re, a_im, log_dt, b_re, b_im, c_re, c_im, d_skip):
    tc = SSM_CHUNK
    hp = lax.Precision.HIGHEST
    k_idx = jnp.arange(tc + 1, dtype=F32)
    lag = jnp.arange(tc)[None, :] - jnp.arange(tc)[:, None]
    ms, bcs, ccs, a16s = [], [], [], []
    for dirn in (0, 1):
        lam = lax.complex(a_re[dirn].astype(F32), a_im[dirn].astype(F32))
        dt = jnp.exp(log_dt[dirn].astype(F32))[:, None]
        pw = jnp.exp((lam * dt)[None] * k_idx[:, None, None])
        a_bar = pw[1]
        b_bar = ((a_bar - 1.0) / lam)[..., None] * lax.complex(b_re[dirn].astype(F32), b_im[dirn].astype(F32))
        c_mat = lax.complex(c_re[dirn].astype(F32), c_im[dirn].astype(F32))
        k_lag = jnp.real(jnp.einsum("ghp,kgp,gpj->kghj", c_mat, pw[:tc], b_bar, precision=hp))
        if dirn == 0:
            dist, valid = lag, lag >= 0
            end_pow = (tc - 1) - jnp.arange(tc)
            out_pow = jnp.arange(tc) + 1
        else:
            dist, valid = -lag, lag <= 0
            end_pow = jnp.arange(tc)
            out_pow = tc - jnp.arange(tc)
        kk = k_lag[jnp.clip(dist, 0, tc - 1)]
        kk = jnp.where(valid[:, :, None, None, None], kk, 0.0)
        g = kk.shape[2]
        ms.append(jnp.transpose(kk, (2, 0, 4, 1, 3)).reshape(g, tc * SSM_GROUP, tc * SSM_GROUP))
        bc = pw[end_pow][:, :, :, None] * b_bar[None]
        bcs.append(jnp.transpose(bc, (1, 0, 3, 2)).reshape(g, tc * SSM_GROUP, SSM_STATE))
        cw = c_mat[None] * pw[out_pow][:, :, None, :]
        ccs.append(jnp.transpose(cw, (1, 3, 0, 2)).reshape(g, SSM_STATE, tc * SSM_GROUP))
        a16s.append(pw[tc])
    m = ms[0] + ms[1]
    bc = jnp.concatenate([jnp.real(bcs[0]), jnp.real(bcs[1]), jnp.imag(bcs[0]), jnp.imag(bcs[1])], axis=-1)
    cc = jnp.concatenate([jnp.real(ccs[0]), jnp.real(ccs[1]), -jnp.imag(ccs[0]), -jnp.imag(ccs[1])], axis=1)
    a16 = jnp.stack([jnp.concatenate([jnp.real(a16s[0]), jnp.real(a16s[1])], axis=-1),
                     jnp.concatenate([jnp.imag(a16s[0]), jnp.imag(a16s[1])], axis=-1)], axis=1)
    g = m.shape[0]
    dtile = jnp.tile(d_skip.astype(F32).reshape(g, 1, SSM_GROUP), (1, 1, tc))
    return m.astype(BF16), bc.astype(BF16), cc.astype(BF16), a16.astype(F32), dtile


def _s5_kernel(u_ref, m_ref, bc_ref, cc_ref, a_ref, d_ref, y_ref, vre_scr, vim_scr, hf_re, hb_re, hf_im, hb_im,
               *, nb, nchunk, nctx):
    p = SSM_STATE
    u = u_ref[0]
    ub = u.astype(BF16)
    v = jnp.dot(ub, bc_ref[0], preferred_element_type=F32)
    vre_scr[...] = v[:, 0:2 * p]
    vim_scr[...] = v[:, 2 * p:4 * p]
    ar = a_ref[0, 0:1, :]
    ai = a_ref[0, 1:2, :]
    fwd_lanes = lax.broadcasted_iota(jnp.int32, (nb, 2 * p), 1) < p
    hre = jnp.zeros((nb, 2 * p), F32)
    him = jnp.zeros((nb, 2 * p), F32)
    for s in range(nchunk):
        cf = s
        cb = (nctx - 1 - s) if s < nctx else (nchunk - 1 - (s - nctx))
        rf = pl.ds(cf, nb, stride=nchunk)
        rb = pl.ds(cb, nb, stride=nchunk)
        hf_re[rf, :] = hre
        hb_re[rb, :] = hre
        hf_im[rf, :] = him
        hb_im[rb, :] = him
        vre = jnp.where(fwd_lanes, vre_scr[rf, :], vre_scr[rb, :])
        vim = jnp.where(fwd_lanes, vim_scr[rf, :], vim_scr[rb, :])
        hre, him = ar * hre - ai * him + vre, ar * him + ai * hre + vim
    is_fwd = lax.broadcasted_iota(jnp.int32, (u.shape[0], 2 * p), 1) < p
    hp = jnp.concatenate([jnp.where(is_fwd, hf_re[...], hb_re[...]),
                          jnp.where(is_fwd, hf_im[...], hb_im[...])], axis=1)
    y = jnp.dot(ub, m_ref[0], preferred_element_type=F32)
    y = y + jnp.dot(hp.astype(BF16), cc_ref[0], preferred_element_type=F32)
    y_ref[0] = y + u * d_ref[0]


def _s5_mix(u, ops, nb, lc):
    m, bc, cc, a16, dtile = ops
    b, s, w = u.shape
    g = w // SSM_GROUP
    tc = SSM_CHUNK
    nchunk = s // tc
    r = nchunk * nb
    k = tc * SSM_GROUP
    ug = u.reshape(b * s * SSM_GROUP, g).T.reshape(g, r, k)
    yg = pl.pallas_call(
        functools.partial(_s5_kernel, nb=nb, nchunk=nchunk, nctx=lc // tc),
        grid=(g,),
        in_specs=[
            pl.BlockSpec((1, r, k), lambda i: (i, 0, 0)),
            pl.BlockSpec((1, k, k), lambda i: (i, 0, 0)),
            pl.BlockSpec((1, k, 4 * SSM_STATE), lambda i: (i, 0, 0)),
            pl.BlockSpec((1, 4 * SSM_STATE, k), lambda i: (i, 0, 0)),
            pl.BlockSpec((1, 2, 2 * SSM_STATE), lambda i: (i, 0, 0)),
            pl.BlockSpec((1, 1, k), lambda i: (i, 0, 0)),
        ],
        out_specs=pl.BlockSpec((1, r, k), lambda i: (i, 0, 0)),
        out_shape=jax.ShapeDtypeStruct((g, r, k), F32),
        scratch_shapes=[pltpu.VMEM((r, 2 * SSM_STATE), F32)] * 6,
        compiler_params=_cparams(("parallel",)),
        name="s5_mix",
    )(ug, m, bc, cc, a16, dtile)
    return yg.reshape(g, b * s * SSM_GROUP).T.reshape(b, s, w)


def _rope_tables(lc, lx):
    n = ATTN_HEAD_DIM // 4
    freq = ROPE_THETA ** (-jnp.arange(n, dtype=F32) / n)
    t = jnp.arange(lx, dtype=jnp.int32)
    rows = (t // GRID_W).astype(F32)[:, None] * freq[None, :]
    cols = (t % GRID_W).astype(F32)[:, None] * freq[None, :]
    cos = jnp.concatenate([jnp.cos(rows), jnp.cos(rows), jnp.cos(cols), jnp.cos(cols)], axis=-1)
    sin = jnp.concatenate([-jnp.sin(rows), jnp.sin(rows), -jnp.sin(cols), jnp.sin(cols)], axis=-1)
    cos = jnp.concatenate([jnp.ones((lc, ATTN_HEAD_DIM), F32), cos], axis=0)
    sin = jnp.concatenate([jnp.zeros((lc, ATTN_HEAD_DIM), F32), sin], axis=0)
    return cos, sin


def _qk_prep_kernel(q_ref, k_ref, v_ref, cos_ref, sin_ref, qg_ref, kg_ref, qo_ref, ko_ref, vo_ref):
    hd = ATTN_HEAD_DIM
    cos = cos_ref[...]
    sin = sin_ref[...]
    lane = lax.broadcasted_iota(jnp.int32, cos.shape, 1)
    first = (lane % (hd // 2)) < (hd // 4)

    def prep(x, g):
        y = x * lax.rsqrt(jnp.mean(x * x, axis=-1, keepdims=True) + EPS) * g
        partner = jnp.where(first, pltpu.roll(y, hd - hd // 4, 1), pltpu.roll(y, hd // 4, 1))
        return y * cos + partner * sin

    for h in range(ATTN_Q_PER_KV):
        sl = slice(h * hd, (h + 1) * hd)
        qo_ref[0, :, sl] = prep(q_ref[0, :, sl], qg_ref[...]).astype(qo_ref.dtype)
    ko_ref[0] = prep(k_ref[0], kg_ref[...]).astype(ko_ref.dtype)
    vo_ref[0] = v_ref[0].astype(vo_ref.dtype)


def _qk_prep(p3, cos, sin, q_gain, k_gain, col_q, col_k, col_v):
    b, s, _ = p3.shape
    hd = ATTN_HEAD_DIM
    qw = ATTN_Q_PER_KV * hd
    tq = _pick(s, (544, 256, 128))
    return pl.pallas_call(
        _qk_prep_kernel,
        grid=(b, s // tq, ATTN_KV_HEADS),
        in_specs=[
            pl.BlockSpec((1, tq, qw), lambda i, j, kv: (i, j, col_q // qw + kv)),
            pl.BlockSpec((1, tq, hd), lambda i, j, kv: (i, j, col_k // hd + kv)),
            pl.BlockSpec((1, tq, hd), lambda i, j, kv: (i, j, col_v // hd + kv)),
            pl.BlockSpec((tq, hd), lambda i, j, kv: (j, 0)),
            pl.BlockSpec((tq, hd), lambda i, j, kv: (j, 0)),
            pl.BlockSpec((1, hd), lambda i, j, kv: (0, 0)),
            pl.BlockSpec((1, hd), lambda i, j, kv: (0, 0)),
        ],
        out_specs=[
            pl.BlockSpec((1, tq, qw), lambda i, j, kv: (i, j, kv)),
            pl.BlockSpec((1, tq, hd), lambda i, j, kv: (i, j, kv)),
            pl.BlockSpec((1, tq, hd), lambda i, j, kv: (i, j, kv)),
        ],
        out_shape=[
            jax.ShapeDtypeStruct((b, s, ATTN_KV_HEADS * qw), BF16),
            jax.ShapeDtypeStruct((b, s, ATTN_KV_HEADS * hd), BF16),
            jax.ShapeDtypeStruct((b, s, ATTN_KV_HEADS * hd), BF16),
        ],
        compiler_params=_cparams(("parallel", "parallel", "parallel")),
        name="qk_prep",
    )(p3, p3, p3, cos, sin, q_gain.reshape(1, hd), k_gain.reshape(1, hd))


def _attn_kernel(q_ref, k_ref, v_ref, o_ref, *, lc, tq):
    hd = ATTN_HEAD_DIM
    c = (hd ** -0.5) * math.log2(math.e)

    def run(kk, vv):
        def scores(h):
            return lax.dot_general(q_ref[0, :, h * hd:(h + 1) * hd], kk, (((1,), (1,)), ((), ())),
                                   preferred_element_type=F32)

        s_next = scores(0)
        for h in range(ATTN_Q_PER_KV):
            s = s_next
            if h + 1 < ATTN_Q_PER_KV:
                s_next = scores(h + 1)
            m = jnp.max(s, axis=-1, keepdims=True)
            p = jnp.exp2((s - m) * c)
            l = jnp.sum(p, axis=-1, keepdims=True)
            o = jnp.dot(p.astype(BF16), vv, preferred_element_type=F32)
            o_ref[0, :, h * hd:(h + 1) * hd] = (o / l).astype(o_ref.dtype)

    is_ctx_tile = pl.program_id(2) * tq < lc

    @pl.when(jnp.logical_not(is_ctx_tile))
    def _():
        run(k_ref[0], v_ref[0])

    @pl.when(is_ctx_tile)
    def _():
        run(k_ref[0, :lc, :], v_ref[0, :lc, :])


def _attention(qh, kh, vh, lc):
    b, s, _ = qh.shape
    hd = ATTN_HEAD_DIM
    qw = ATTN_Q_PER_KV * hd
    tq = _pick(lc, (256, 128))
    assert s % tq == 0
    return pl.pallas_call(
        functools.partial(_attn_kernel, lc=lc, tq=tq),
        grid=(b, ATTN_KV_HEADS, s // tq),
        in_specs=[
            pl.BlockSpec((1, tq, qw), lambda i, kv, j: (i, j, kv)),
            pl.BlockSpec((1, s, hd), lambda i, kv, j: (i, 0, kv)),
            pl.BlockSpec((1, s, hd), lambda i, kv, j: (i, 0, kv)),
        ],
        out_specs=pl.BlockSpec((1, tq, qw), lambda i, kv, j: (i, j, kv)),
        out_shape=jax.ShapeDtypeStruct((b, s, ATTN_KV_HEADS * qw), BF16),
        compiler_params=_cparams(("parallel", "parallel", "parallel")),
        name="attention",
    )(qh, kh, vh)


def _split3(x):
    hi = x.astype(BF16)
    r = x - hi.astype(F32)
    mid = r.astype(BF16)
    lo = (r - mid.astype(F32)).astype(BF16)
    return hi, mid, lo


def _log_sigmoid(x):
    return jnp.minimum(x, 0.0) - jnp.log(1.0 + jnp.exp(-jnp.abs(x)))


def _mlstm_kernel(q_ref, k_ref, v_ref, gc_ref, gr_ref, bc_ref, br_ref, o_ref, c_scr, n_scr, m_scr):
    nh, dk, dv, cl = MLSTM_HEADS, MLSTM_QK_DIM, MLSTM_V_DIM, MLSTM_CHUNK
    nb = q_ref.shape[0]
    d = pl.program_id(0)
    c = pl.program_id(1)

    @pl.when(c == 0)
    def _():
        c_scr[...] = jnp.zeros_like(c_scr)
        n_scr[...] = jnp.zeros_like(n_scr)
        m_scr[...] = jnp.zeros_like(m_scr)

    row = lax.broadcasted_iota(jnp.int32, (cl, cl), 0)
    col = lax.broadcasted_iota(jnp.int32, (cl, cl), 1)
    sign = jnp.where(d == 0, 1, -1)
    mask = sign * (row - col) >= 0
    tm = jnp.where(mask, 1.0, 0.0).astype(BF16)
    tmt = jnp.where(sign * (col - row) >= 0, 1.0, 0.0).astype(BF16)

    def pick_c(a, k):
        return jnp.where(d == 0, a[:, k:k + 1], a[:, 2 * nh + k:2 * nh + k + 1])

    def pick_r(a, k):
        return jnp.where(d == 0, a[k:k + 1, :], a[2 * nh + k:2 * nh + k + 1, :])

    nt = (((1,), (1,)), ((), ()))
    tn = (((0,), (0,)), ((), ()))
    heads = [(bi, h) for bi in range(nb) for h in range(nh)]

    gcs = [gc_ref[bi] + bc_ref[...] for bi in range(nb)]
    grs = [gr_ref[bi, 0] + br_ref[...] for bi in range(nb)]
    lf_cs = [_log_sigmoid(g) for g in gcs]
    lf_rs = [_log_sigmoid(g) for g in grs]
    cum_cs = [sum(jnp.dot(tm, part, preferred_element_type=F32) for part in _split3(x)) for x in lf_cs]
    cum_rs = [sum(jnp.dot(part, tmt, preferred_element_type=F32) for part in _split3(x)) for x in lf_rs]
    totals = [jnp.sum(x, axis=0, keepdims=True) for x in lf_cs]

    qs = [q_ref[bi][:, h * dk:(h + 1) * dk] * (dk ** -0.5) for bi, h in heads]
    ks = [k_ref[bi][:, h * dk:(h + 1) * dk] for bi, h in heads]
    vs = [v_ref[bi][:, h * dv:(h + 1) * dv] for bi, h in heads]
    qbs = [x.astype(BF16) for x in qs]
    kbs = [x.astype(BF16) for x in ks]
    m_sts = [m_scr[i][:, 0:1] for i in range(len(heads))]
    c_sts = [c_scr[i] for i in range(len(heads))]
    n_sts = [n_scr[i] for i in range(len(heads))]

    s_raw = [lax.dot_general(qbs[i], kbs[i], nt, preferred_element_type=F32) for i in range(len(heads))]
    cqs = [lax.dot_general(qbs[i], c_sts[i].astype(BF16), nt, preferred_element_type=F32)
           for i in range(len(heads))]

    cf_c = [pick_c(cum_cs[bi], nh + h) for bi, h in heads]
    cf_r = [pick_r(cum_rs[bi], nh + h) for bi, h in heads]
    li_c = [pick_c(gcs[bi], h) for bi, h in heads]
    li_r = [pick_r(grs[bi], h) for bi, h in heads]
    tot = [pick_c(totals[bi], nh + h) for bi, h in heads]

    log_d = [jnp.where(mask, cf_c[i] - cf_r[i] + li_r[i], -jnp.inf) for i in range(len(heads))]
    log_inter = [cf_c[i] + m_sts[i] for i in range(len(heads))]
    m_row = [jnp.maximum(log_inter[i], jnp.max(log_d[i], axis=-1, keepdims=True)) for i in range(len(heads))]
    s = [s_raw[i] * jnp.exp(log_d[i] - m_row[i]) for i in range(len(heads))]
    w_inter = [jnp.exp(log_inter[i] - m_row[i]) for i in range(len(heads))]
    sv = [jnp.dot(s[i].astype(BF16), vs[i].astype(BF16), preferred_element_type=F32) for i in range(len(heads))]
    den = [jnp.sum(s[i], axis=-1, keepdims=True)
           + w_inter[i] * jnp.sum(qs[i] * n_sts[i], axis=-1, keepdims=True) for i in range(len(heads))]
    for i, (bi, h) in enumerate(heads):
        num = sv[i] + w_inter[i] * cqs[i]
        o_ref[0, bi, :, h * dv:(h + 1) * dv] = num / jnp.maximum(jnp.abs(den[i]), jnp.exp(-m_row[i]))

    log_w = [tot[i] - cf_c[i] + li_c[i] for i in range(len(heads))]
    m_new = [jnp.maximum(tot[i] + m_sts[i], jnp.max(log_w[i], axis=0, keepdims=True)) for i in range(len(heads))]
    decay = [jnp.exp(tot[i] + m_sts[i] - m_new[i]) for i in range(len(heads))]
    w = [jnp.exp(log_w[i] - m_new[i]) for i in range(len(heads))]
    kv = [lax.dot_general((w[i] * vs[i]).astype(BF16), kbs[i], tn, preferred_element_type=F32)
          for i in range(len(heads))]
    for i in range(len(heads)):
        c_scr[i] = decay[i] * c_sts[i] + kv[i]
        n_scr[i] = decay[i] * n_sts[i] + jnp.sum(w[i] * ks[i], axis=0, keepdims=True)
        m_scr[i] = jnp.broadcast_to(m_new[i], (1, LANES))


def _mlstm(p3, gate_bias, lc, col_q, col_k, col_v, col_g):
    b, s, _ = p3.shape
    nh, dk, dv, cl = MLSTM_HEADS, MLSTM_QK_DIM, MLSTM_V_DIM, MLSTM_CHUNK
    nc = s // cl
    nctx = lc // cl
    ng = 4 * nh
    g_rows = jnp.transpose(p3[:, :, col_g:col_g + ng].reshape(b, nc, cl, ng), (0, 1, 3, 2))
    bias = gate_bias.astype(F32).reshape(ng)

    def chunk(d, c):
        bwd = jnp.where(c < nctx, nctx - 1 - c, nc - 1 - (c - nctx))
        return jnp.where(d == 0, c, bwd)

    return pl.pallas_call(
        _mlstm_kernel,
        grid=(2, nc),
        in_specs=[
            pl.BlockSpec((b, cl, nh * dk), lambda d, c: (0, chunk(d, c), col_q // (nh * dk))),
            pl.BlockSpec((b, cl, nh * dk), lambda d, c: (0, chunk(d, c), col_k // (nh * dk))),
            pl.BlockSpec((b, cl, nh * dv), lambda d, c: (0, chunk(d, c), col_v // (nh * dv))),
            pl.BlockSpec((b, cl, LANES), lambda d, c: (0, chunk(d, c), col_g // LANES)),
            pl.BlockSpec((b, 1, ng, cl), lambda d, c: (0, chunk(d, c), 0, 0)),
            pl.BlockSpec((1, LANES), lambda d, c: (0, 0)),
            pl.BlockSpec((ng, 1), lambda d, c: (0, 0)),
        ],
        out_specs=pl.BlockSpec((1, b, cl, nh * dv), lambda d, c: (d, 0, chunk(d, c), 0)),
        out_shape=jax.ShapeDtypeStruct((2, b, s, nh * dv), F32),
        scratch_shapes=[pltpu.VMEM((b * nh, dv, dk), F32), pltpu.VMEM((b * nh, 1, dk), F32),
                        pltpu.VMEM((b * nh, 1, LANES), F32)],
        compiler_params=_cparams(("arbitrary", "arbitrary")),
        name="mlstm",
    )(p3, p3, p3, p3, g_rows, jnp.pad(bias, (0, LANES - ng)).reshape(1, LANES), bias.reshape(ng, 1))


def _gelu_tanh(x):
    return 0.5 * x * (1.0 + jnp.tanh(math.sqrt(2.0 / math.pi) * (x + 0.044715 * (x * x * x))))


def _merge_kernel(ys_ref, att_ref, hf_ref, hb_ref, og_ref, gl_ref, wglu_ref, bglu_ref, ng_ref,
                  wbs_ref, wba_ref, wbm_ref, y_ref):
    d = y_ref.shape[-1]
    z = _gelu_tanh(ys_ref[...])
    s5 = z * jax.nn.sigmoid(jnp.dot(z.astype(BF16), wglu_ref[...], preferred_element_type=F32) + bglu_ref[...])
    hsum = hf_ref[0] + hb_ref[0]
    dv = MLSTM_V_DIM
    parts = []
    for h in range(MLSTM_HEADS):
        blk = hsum[:, h * dv:(h + 1) * dv]
        parts.append(blk * lax.rsqrt(jnp.mean(blk * blk, axis=-1, keepdims=True) + EPS))
    ml = jnp.concatenate(parts, axis=-1) * ng_ref[...] * jax.nn.sigmoid(og_ref[...])
    y = gl_ref[:, 0:d].astype(F32) * jnp.dot(s5.astype(BF16), wbs_ref[...], preferred_element_type=F32)
    y = y + gl_ref[:, d:2 * d].astype(F32) * jnp.dot(att_ref[...], wba_ref[...], preferred_element_type=F32)
    y = y + gl_ref[:, 2 * d:3 * d].astype(F32) * jnp.dot(ml.astype(BF16), wbm_ref[...], preferred_element_type=F32)
    y_ref[...] = y.astype(y_ref.dtype)


def _merge(ys, att, hfb, p_main, gates, wglu, bglu, ng, wbs, wba, wbm, col_o):
    t, ws = ys.shape
    wa = att.shape[1]
    wm = hfb.shape[-1]
    d = wbs.shape[1]
    tm = _pick(t, (256, 128))
    full = lambda shape: pl.BlockSpec(shape, lambda i: (0,) * len(shape))
    return pl.pallas_call(
        _merge_kernel,
        grid=(t // tm,),
        in_specs=[
            pl.BlockSpec((tm, ws), lambda i: (i, 0)),
            pl.BlockSpec((tm, wa), lambda i: (i, 0)),
            pl.BlockSpec((1, tm, wm), lambda i: (0, i, 0)),
            pl.BlockSpec((1, tm, wm), lambda i: (1, i, 0)),
            pl.BlockSpec((tm, wm), lambda i: (i, col_o // wm)),
            pl.BlockSpec((tm, 3 * d), lambda i: (i, 0)),
            full((ws, ws)), full((1, ws)), full((1, wm)),
            full((ws, d)), full((wa, d)), full((wm, d)),
        ],
        out_specs=pl.BlockSpec((tm, d), lambda i: (i, 0)),
        out_shape=jax.ShapeDtypeStruct((t, d), BF16),
        compiler_params=_cparams(("parallel",)),
        name="branch_merge",
    )(ys, att, hfb, hfb, p_main, gates, wglu, bglu.reshape(1, ws), ng.reshape(1, wm), wbs, wba, wbm)


def _out_proj_kernel(y_ref, w_ref, x_ref, mod_ref, o_ref, *, lc, tm):
    out = jnp.dot(y_ref[0], w_ref[...], preferred_element_type=F32)
    gate = _row_select(mod_ref, 2, _is_ctx_rows(tm, lc))
    o_ref[0] = x_ref[0] + gate * out


def _out_proj(y3, w_out, xa, mod, lc):
    b, s, d = xa.shape
    tm = _pick(s, (544, 256, 128))
    return pl.pallas_call(
        functools.partial(_out_proj_kernel, lc=lc, tm=tm),
        grid=(b, s // tm),
        in_specs=[
            pl.BlockSpec((1, tm, d), lambda i, j: (i, j, 0)),
            pl.BlockSpec((d, d), lambda i, j: (0, 0)),
            pl.BlockSpec((1, tm, d), lambda i, j: (i, j, 0)),
            pl.BlockSpec((1, 2, 6, d), lambda i, j: (i, 0, 0, 0)),
        ],
        out_specs=pl.BlockSpec((1, tm, d), lambda i, j: (i, j, 0)),
        out_shape=jax.ShapeDtypeStruct((b, s, d), F32),
        compiler_params=_cparams(("parallel", "parallel")),
        name="out_proj_residual",
    )(y3, w_out, xa, mod)


def _pair_split_matrix():
    half = MXU_COLS // 2
    n = jnp.arange(MXU_COLS)
    src = jnp.where(n < half, 2 * n, 2 * (n - half) + 1)
    return (jnp.arange(MXU_COLS)[:, None] == src[None, :]).astype(BF16)


def _pair_split_index(width):
    half = MXU_COLS // 2
    n = jnp.arange(width)
    blk, r = n // MXU_COLS, n % MXU_COLS
    return blk * MXU_COLS + jnp.where(r < half, 2 * r, 2 * (r - half) + 1)


def _split_pairs_kernel(w_ref, p_ref, o_ref):
    for cb in range(w_ref.shape[-1] // MXU_COLS):
        sl = slice(cb * MXU_COLS, (cb + 1) * MXU_COLS)
        o_ref[0, :, sl] = jnp.dot(w_ref[0, :, sl].astype(BF16), p_ref[...],
                                  preferred_element_type=F32).astype(o_ref.dtype)


def _split_pairs(w):
    ne, d, n = w.shape
    tr = _pick(d, (1024, 512, 256, 128))
    return pl.pallas_call(
        _split_pairs_kernel,
        grid=(ne, d // tr),
        in_specs=[
            pl.BlockSpec((1, tr, n), lambda e, i: (e, i, 0)),
            pl.BlockSpec((MXU_COLS, MXU_COLS), lambda e, i: (0, 0)),
        ],
        out_specs=pl.BlockSpec((1, tr, n), lambda e, i: (e, i, 0)),
        out_shape=jax.ShapeDtypeStruct((ne, d, n), BF16),
        compiler_params=_cparams(("parallel", "parallel")),
        name="split_gate_linear",
    )(w, _pair_split_matrix())


def _expert_kernel(te_ref, nu_ref, x_ref, wgl_ref, bgl_ref, wd_ref, bd_ref, o_ref, wd_scr):
    i = pl.program_id(0)
    half = MXU_COLS // 2
    used = i < nu_ref[0]
    new_expert = jnp.logical_or(i == 0, te_ref[i] != te_ref[jnp.maximum(i - 1, 0)])

    @pl.when(jnp.logical_and(used, new_expert))
    def _():
        wd_scr[...] = wd_ref[0].astype(BF16)

    @pl.when(used)
    def _():
        gu = jnp.dot(x_ref[...], wgl_ref[0], preferred_element_type=F32) + bgl_ref[0]
        nblk = gu.shape[1] // MXU_COLS
        gate = jnp.concatenate([gu[:, cb * MXU_COLS:cb * MXU_COLS + half] for cb in range(nblk)], axis=1)
        lin = jnp.concatenate([gu[:, cb * MXU_COLS + half:(cb + 1) * MXU_COLS] for cb in range(nblk)], axis=1)
        gate = jnp.minimum(gate, SWIGLU_LIMIT)
        lin = jnp.clip(lin, -SWIGLU_LIMIT, SWIGLU_LIMIT)
        act = gate * jax.nn.sigmoid(SWIGLU_ALPHA * gate) * (lin + 1.0)
        y = jnp.dot(act.astype(BF16), wd_scr[...], preferred_element_type=F32) + bd_ref[0]
        o_ref[...] = y.astype(o_ref.dtype)

    @pl.when(jnp.logical_not(used))
    def _():
        o_ref[...] = jnp.zeros_like(o_ref)


def _experts(xs, tile_expert, n_used, wgl, bgl, wd, bd):
    p, d = xs.shape
    _, _, ff2 = wgl.shape
    ff = ff2 // 2
    tm = EXPERT_TILE
    n_tiles = p // tm
    grid_spec = pltpu.PrefetchScalarGridSpec(
        num_scalar_prefetch=2,
        grid=(n_tiles,),
        in_specs=[
            pl.BlockSpec((tm, d), lambda i, te, nu: (jnp.minimum(i, nu[0] - 1), 0)),
            pl.BlockSpec((1, d, ff2), lambda i, te, nu: (te[i], 0, 0)),
            pl.BlockSpec((1, 1, ff2), lambda i, te, nu: (te[i], 0, 0)),
            pl.BlockSpec((1, ff, d), lambda i, te, nu: (te[i], 0, 0)),
            pl.BlockSpec((1, 1, d), lambda i, te, nu: (te[i], 0, 0)),
        ],
        out_specs=pl.BlockSpec((tm, d), lambda i, te, nu: (i, 0)),
        scratch_shapes=[pltpu.VMEM((ff, d), BF16)],
    )
    return pl.pallas_call(
        _expert_kernel,
        grid_spec=grid_spec,
        out_shape=jax.ShapeDtypeStruct((p, d), BF16),
        compiler_params=_cparams(("arbitrary",)),
        name="experts",
    )(tile_expert, n_used, xs, wgl, bgl, wd, bd)


def _route(top_idx, n_slots):
    tm = EXPERT_TILE
    e = top_idx.reshape(-1)
    n = e.shape[0]
    onehot = (e[:, None] == jnp.arange(N_EXPERTS, dtype=jnp.int32)[None, :]).astype(jnp.int32)
    counts = jnp.sum(onehot, axis=0)
    rank = jnp.take_along_axis(jnp.cumsum(onehot, axis=0) - onehot, e[:, None], axis=1)[:, 0]
    padded = ((counts + tm - 1) // tm) * tm
    ends = jnp.cumsum(padded)
    dest = (ends - padded)[e] + rank
    n_used = (ends[-1] // tm).astype(jnp.int32)
    tile_start = jnp.arange(n_slots // tm, dtype=jnp.int32) * tm
    tile_expert = jnp.sum((ends[None, :] <= tile_start[:, None]).astype(jnp.int32), axis=1)
    last_used = jnp.sum((ends <= (n_used - 1) * tm).astype(jnp.int32))
    tile_expert = jnp.minimum(jnp.where(tile_start < ends[-1], tile_expert, last_used), N_EXPERTS - 1)
    src = jnp.zeros((n_slots,), jnp.int32).at[dest].set(jnp.arange(n, dtype=jnp.int32) // TOP_K)
    return src, dest, tile_expert.astype(jnp.int32), n_used.reshape(1)


def _combine_kernel(y0_ref, y1_ref, y2_ref, y3_ref, tw_ref, x_ref, mod_ref, fg_ref, o_ref, *, lc, tm, final):
    tw = tw_ref[0]
    moe = tw[:, 0:1] * y0_ref[0, 0].astype(F32)
    for k, y_ref in enumerate((y1_ref, y2_ref, y3_ref), start=1):
        moe = moe + tw[:, k:k + 1] * y_ref[0, 0].astype(F32)
    if final:
        xn = x_ref[0] + mod_ref[0, 1, 5:6, :] * moe
        xn = xn * lax.rsqrt(jnp.mean(xn * xn, axis=-1, keepdims=True) + EPS) * fg_ref[...]
    else:
        xn = x_ref[0] + _row_select(mod_ref, 5, _is_ctx_rows(tm, lc)) * moe
    o_ref[0] = xn


def _combine(yg, top_w, xa, mod, final_g, lc, final):
    b, s, d = xa.shape
    assert TOP_K == 4
    if final:
        tm = _pick(lc, (256, 128))
        assert (s - lc) % tm == 0
        off, rows = lc // tm, s - lc
    else:
        tm = _pick(s, (544, 256, 128))
        off, rows = 0, s
    y_specs = [pl.BlockSpec((1, 1, tm, d), functools.partial(lambda i, j, k: (k, i, j + off, 0), k=k))
               for k in range(TOP_K)]
    return pl.pallas_call(
        functools.partial(_combine_kernel, lc=lc, tm=tm, final=final),
        grid=(b, rows // tm),
        in_specs=y_specs + [
            pl.BlockSpec((1, tm, TOP_K), lambda i, j: (i, j + off, 0)),
            pl.BlockSpec((1, tm, d), lambda i, j: (i, j + off, 0)),
            pl.BlockSpec((1, 2, 6, d), lambda i, j: (i, 0, 0, 0)),
            pl.BlockSpec((1, d), lambda i, j: (0, 0)),
        ],
        out_specs=pl.BlockSpec((1, tm, d), lambda i, j: (i, j, 0)),
        out_shape=jax.ShapeDtypeStruct((b, rows, d), F32),
        compiler_params=_cparams(("parallel", "parallel")),
        name="moe_combine",
    )(yg, yg, yg, yg, top_w, xa, mod, final_g.reshape(1, d))


def kernel(x, c, ctx, c_ctx, w_ada, b_ada, norm_g, w_in, ssm_a_re, ssm_a_im, ssm_log_dt, ssm_b_re, ssm_b_im, ssm_c_re, ssm_c_im, ssm_d, ssm_w_glu, ssm_b_glu, attn_q_gain, attn_k_gain, mlstm_gate_bias, mlstm_norm_g, w_branch_ssm, w_branch_attn, w_branch_mlstm, w_out, w_router, b_router, w_gate_up, b_gate_up, w_down, b_down, final_g):
    b, lx, d = x.shape
    lc = ctx.shape[1]
    s = lc + lx
    t = b * s
    depth = w_in.shape[0]
    assert b + 1 <= 8 and lc % MLSTM_CHUNK == 0 and lx % MLSTM_CHUNK == 0

    ssm_w = ssm_d.shape[1]
    attn_w = w_branch_attn.shape[1]
    kv_w = ATTN_KV_HEADS * ATTN_HEAD_DIM
    mqk_w = MLSTM_HEADS * MLSTM_QK_DIM
    mv_w = MLSTM_HEADS * MLSTM_V_DIM
    n_gates = 4 * MLSTM_HEADS
    col_q = ssm_w
    col_k = col_q + attn_w
    col_v = col_k + kv_w
    col_mq = col_v + kv_w
    col_mk = col_mq + mqk_w
    col_mv = col_mk + mqk_w
    col_mo = col_mv + mv_w
    col_mg = col_mo + mv_w
    col_bg = col_mg + n_gates
    main_pad = (-(col_mg + n_gates)) % 256

    xa = jnp.concatenate([ctx, x], axis=1)
    cond8 = jnp.zeros((8, d), F32).at[0].set(c_ctx).at[1:1 + b].set(c)
    mods = _ada_modulation(cond8, w_ada, b_ada).reshape(depth, 8, 6, d)
    cos, sin = _rope_tables(lc, lx)
    n_groups = ssm_w // SSM_GROUP

    def regroup(a, axis):
        shp = a.shape
        a = a.reshape(shp[:axis] + (n_groups, SSM_GROUP) + shp[axis + 1:])
        return jnp.swapaxes(a, axis, axis + 1).reshape(shp)

    ne, ff2 = w_gate_up.shape[1], w_gate_up.shape[3]
    wgl_all = _split_pairs(w_gate_up.reshape(depth * ne, d, ff2))
    bgl_all = b_gate_up.reshape(depth * ne, ff2)[:, _pair_split_index(ff2)][:, None, :]
    wd_all = w_down.reshape(depth * ne, ff2 // 2, d)
    bd_all = b_down.reshape(depth * ne, 1, d)

    for l in range(depth):
        last = l == depth - 1
        mod = jnp.stack([jnp.broadcast_to(mods[l, 0], (b, 6, d)), mods[l, 1:1 + b]], axis=1)

        h = _norm_mod(xa, norm_g[l, 0], mod, lc, 0).reshape(t, d)
        w_main = jnp.concatenate(
            [regroup(w_in[l, :, :ssm_w], 1), w_in[l, :, ssm_w:col_bg], jnp.zeros((d, main_pad), F32)],
            axis=1).astype(BF16)
        p_main = _gemm(h, w_main, F32)
        gates = _gemm(h, w_in[l, :, col_bg:].astype(BF16), BF16, act="sigmoid")
        p3 = p_main.reshape(b, s, -1)

        ops = _s5_operators(ssm_a_re[l], ssm_a_im[l], ssm_log_dt[l], ssm_b_re[l], ssm_b_im[l],
                            ssm_c_re[l], ssm_c_im[l], ssm_d[l])
        ys = _s5_mix(p3[:, :, :ssm_w], ops, b, lc)

        qh, kh, vh = _qk_prep(p3, cos, sin, attn_q_gain[l], attn_k_gain[l], col_q, col_k, col_v)
        att = _attention(qh, kh, vh, lc)

        hfb = _mlstm(p3, mlstm_gate_bias[l], lc, col_mq, col_mk, col_mv, col_mg)

        y = _merge(ys.reshape(t, ssm_w), att.reshape(t, attn_w), hfb.reshape(2, t, mv_w), p_main, gates,
                   regroup(regroup(ssm_w_glu[l], 0), 1).astype(BF16), regroup(ssm_b_glu[l], 0), mlstm_norm_g[l],
                   regroup(w_branch_ssm[l], 0).astype(BF16), w_branch_attn[l].astype(BF16),
                   w_branch_mlstm[l].astype(BF16), col_mo)
        xa = _out_proj(y.reshape(b, s, d), w_out[l].astype(BF16), xa, mod, lc)

        h2, top_idx, top_w = _norm_router(xa, norm_g[l, 1], mod, w_router[l], b_router[l], lc)
        n_slots = t * TOP_K + N_EXPERTS * EXPERT_TILE
        src, dest, tile_expert, n_used = _route(top_idx, n_slots)
        xs = h2.reshape(t, d).at[src].get(mode="promise_in_bounds")
        ysorted = _experts(xs, tile_expert + l * ne, n_used, wgl_all, bgl_all, wd_all, bd_all)
        yg = ysorted.at[dest.reshape(t, TOP_K).T].get(mode="promise_in_bounds").reshape(TOP_K, b, s, d)
        xa = _combine(yg, top_w, xa, mod, final_g, lc, last)

    return xa
```

```python
import functools
import math

import jax
import jax.numpy as jnp
from jax import lax
from jax.experimental import pallas as pl
from jax.experimental.pallas import tpu as pltpu

F32 = jnp.float32
BF16 = jnp.bfloat16

GRID_W = 64
EPS = 1e-6
SSM_GROUP = 16
SSM_STATE = 64
SSM_CHUNK = 16
ATTN_HEAD_DIM = 128
ATTN_KV_HEADS = 2
ATTN_Q_PER_KV = 4
ROPE_THETA = 10000.0
MLSTM_HEADS = 4
MLSTM_QK_DIM = 64
MLSTM_V_DIM = 128
MLSTM_CHUNK = 64
N_EXPERTS = 32
TOP_K = 4
SWIGLU_LIMIT = 7.0
SWIGLU_ALPHA = 1.702

LANES = 128
MXU_COLS = 256
VMEM_LIMIT_BYTES = 56 * 1024 * 1024
EXPERT_TILE = 256
EXPERT_WEIGHT_PARTS = 4

def _cparams(sem):
    return pltpu.CompilerParams(dimension_semantics=sem, vmem_limit_bytes=VMEM_LIMIT_BYTES)


def _pick(n, cands):
    for c in cands:
        if n % c == 0:
            return c
    raise ValueError(f"no tile for {n} in {cands}")


def _ada_kernel(c_ref, w_ref, b_ref, o_ref):
    c = c_ref[...]
    s = c * jax.nn.sigmoid(c)
    o_ref[0] = jnp.dot(s, w_ref[0], preferred_element_type=F32) + b_ref[0]


def _ada_modulation(cond8, w_ada, b_ada):
    nl, d, n = w_ada.shape
    tn = _pick(n, (1024, 512, 256, 128))
    return pl.pallas_call(
        _ada_kernel,
        grid=(nl, n // tn),
        in_specs=[
            pl.BlockSpec((8, d), lambda l, j: (0, 0)),
            pl.BlockSpec((1, d, tn), lambda l, j: (l, 0, j)),
            pl.BlockSpec((1, 1, tn), lambda l, j: (l, 0, j)),
        ],
        out_specs=pl.BlockSpec((1, 8, tn), lambda l, j: (l, 0, j)),
        out_shape=jax.ShapeDtypeStruct((nl, 8, n), F32),
        compiler_params=_cparams(("parallel", "parallel")),
        name="ada_modulation",
    )(cond8, w_ada, b_ada.reshape(nl, 1, n))


def _row_select(mod_ref, k, is_ctx):
    return jnp.where(is_ctx, mod_ref[0, 0, k:k + 1, :], mod_ref[0, 1, k:k + 1, :])


def _modulated_norm(x, g, mod_ref, k_shift, is_ctx):
    ms = jnp.mean(x * x, axis=-1, keepdims=True)
    y = x * lax.rsqrt(ms + EPS) * g
    return y * (1.0 + _row_select(mod_ref, k_shift + 1, is_ctx)) + _row_select(mod_ref, k_shift, is_ctx)


def _is_ctx_rows(tm, lc):
    pos = pl.program_id(1) * tm + lax.broadcasted_iota(jnp.int32, (tm, 1), 0)
    return pos < lc


def _norm_mod_kernel(x_ref, g_ref, mod_ref, o_ref, *, lc, tm, k_shift):
    h = _modulated_norm(x_ref[0], g_ref[...], mod_ref, k_shift, _is_ctx_rows(tm, lc))
    o_ref[0] = h.astype(o_ref.dtype)


def _norm_mod(xa, g, mod, lc, k_shift):
    b, s, d = xa.shape
    tm = _pick(s, (544, 256, 128))
    return pl.pallas_call(
        functools.partial(_norm_mod_kernel, lc=lc, tm=tm, k_shift=k_shift),
        grid=(b, s // tm),
        in_specs=[
            pl.BlockSpec((1, tm, d), lambda i, j: (i, j, 0)),
            pl.BlockSpec((1, d), lambda i, j: (0, 0)),
            pl.BlockSpec((1, 2, 6, d), lambda i, j: (i, 0, 0, 0)),
        ],
        out_specs=pl.BlockSpec((1, tm, d), lambda i, j: (i, j, 0)),
        out_shape=jax.ShapeDtypeStruct((b, s, d), BF16),
        compiler_params=_cparams(("parallel", "parallel")),
        name="norm_mod",
    )(xa, g.reshape(1, d), mod)


def _norm_router_kernel(x_ref, g_ref, mod_ref, wr_ref, br_ref, h_ref, idx_ref, w_ref, *, lc, tm):
    h = _modulated_norm(x_ref[0], g_ref[...], mod_ref, 3, _is_ctx_rows(tm, lc))
    for j in range(h.shape[1] // LANES):
        h_ref[0, :, j, :] = h[:, j * LANES:(j + 1) * LANES]
    logits = jnp.dot(h, wr_ref[...], preferred_element_type=F32,
                     precision=lax.Precision.HIGHEST) + br_ref[...]
    lane = lax.broadcasted_iota(jnp.int32, logits.shape, 1).astype(F32)
    vals, idxs = [], []
    cur = logits
    for _ in range(TOP_K):
        mx = jnp.max(cur, axis=-1, keepdims=True)
        ix = jnp.min(jnp.where(cur == mx, lane, float(N_EXPERTS)), axis=-1, keepdims=True)
        vals.append(mx)
        idxs.append(ix)
        cur = jnp.where(lane == ix, -jnp.inf, cur)
    k_iota = lax.broadcasted_iota(jnp.int32, (logits.shape[0], TOP_K), 1)
    top = jnp.zeros((logits.shape[0], TOP_K), F32)
    top_i = jnp.zeros((logits.shape[0], TOP_K), F32)
    for k in range(TOP_K):
        top = jnp.where(k_iota == k, vals[k], top)
        top_i = jnp.where(k_iota == k, idxs[k], top_i)
    e = jnp.exp(top - vals[0])
    w_ref[0] = e / jnp.sum(e, axis=-1, keepdims=True)
    idx_ref[0] = top_i.astype(jnp.int32)


def _norm_router(xa, g, mod, w_router, b_router, lc):
    b, s, d = xa.shape
    tm = _pick(s, (544, 256, 128))
    ne = w_router.shape[1]
    return pl.pallas_call(
        functools.partial(_norm_router_kernel, lc=lc, tm=tm),
        grid=(b, s // tm),
        in_specs=[
            pl.BlockSpec((1, tm, d), lambda i, j: (i, j, 0)),
            pl.BlockSpec((1, d), lambda i, j: (0, 0)),
            pl.BlockSpec((1, 2, 6, d), lambda i, j: (i, 0, 0, 0)),
            pl.BlockSpec((d, ne), lambda i, j: (0, 0)),
            pl.BlockSpec((1, ne), lambda i, j: (0, 0)),
        ],
        out_specs=[
            pl.BlockSpec((1, tm, d // LANES, LANES), lambda i, j: (i, j, 0, 0)),
            pl.BlockSpec((1, tm, TOP_K), lambda i, j: (i, j, 0)),
            pl.BlockSpec((1, tm, TOP_K), lambda i, j: (i, j, 0)),
        ],
        out_shape=[
            jax.ShapeDtypeStruct((b, s, d // LANES, LANES), F32),
            jax.ShapeDtypeStruct((b, s, TOP_K), jnp.int32),
            jax.ShapeDtypeStruct((b, s, TOP_K), F32),
        ],
        compiler_params=_cparams(("parallel", "parallel")),
        name="norm_router",
    )(xa, g.reshape(1, d), mod, w_router, b_router.reshape(1, ne))


def _gemm_kernel(a_ref, w_ref, o_ref, *, act):
    acc = jnp.dot(a_ref[...], w_ref[...], preferred_element_type=F32)
    if act == "sigmoid":
        acc = jax.nn.sigmoid(acc)
    o_ref[...] = acc.astype(o_ref.dtype)


def _gemm(a, w, out_dtype, act=None, tn_cands=(1920, 1024, 512, 256, 128)):
    t, k = a.shape
    n = w.shape[1]
    tm = _pick(t, (1024, 768, 512, 256, 128))
    tn = _pick(n, tn_cands)
    return pl.pallas_call(
        functools.partial(_gemm_kernel, act=act),
        grid=(t // tm, n // tn),
        in_specs=[
            pl.BlockSpec((tm, k), lambda i, j: (i, 0)),
            pl.BlockSpec((k, tn), lambda i, j: (0, j)),
        ],
        out_specs=pl.BlockSpec((tm, tn), lambda i, j: (i, j)),
        out_shape=jax.ShapeDtypeStruct((t, n), out_dtype),
        compiler_params=_cparams(("parallel", "parallel")),
        name="gemm_" + (act or "plain"),
    )(a, w)


def _s5_operators(a_re, a_im, log_dt, b_re, b_im, c_re, c_im, d_skip):
    tc = SSM_CHUNK
    hp = lax.Precision.HIGHEST
    k_idx = jnp.arange(tc + 1, dtype=F32)
    lag = jnp.arange(tc)[None, :] - jnp.arange(tc)[:, None]
    ms, bcs, ccs, a16s = [], [], [], []
    for dirn in (0, 1):
        lam = lax.complex(a_re[dirn].astype(F32), a_im[dirn].astype(F32))
        dt = jnp.exp(log_dt[dirn].astype(F32))[:, None]
        pw = jnp.exp((lam * dt)[None] * k_idx[:, None, None])
        a_bar = pw[1]
        b_bar = ((a_bar - 1.0) / lam)[..., None] * lax.complex(b_re[dirn].astype(F32), b_im[dirn].astype(F32))
        c_mat = lax.complex(c_re[dirn].astype(F32), c_im[dirn].astype(F32))
        k_lag = jnp.real(jnp.einsum("ghp,kgp,gpj->kghj", c_mat, pw[:tc], b_bar, precision=hp))
        if dirn == 0:
            dist, valid = lag, lag >= 0
            end_pow = (tc - 1) - jnp.arange(tc)
            out_pow = jnp.arange(tc) + 1
        else:
            dist, valid = -lag, lag <= 0
            end_pow = jnp.arange(tc)
            out_pow = tc - jnp.arange(tc)
        kk = k_lag[jnp.clip(dist, 0, tc - 1)]
        kk = jnp.where(valid[:, :, None, None, None], kk, 0.0)
        g = kk.shape[2]
        ms.append(jnp.transpose(kk, (2, 0, 4, 1, 3)).reshape(g, tc * SSM_GROUP, tc * SSM_GROUP))
        bc = pw[end_pow][:, :, :, None] * b_bar[None]
        bcs.append(jnp.transpose(bc, (1, 0, 3, 2)).reshape(g, tc * SSM_GROUP, SSM_STATE))
        cw = c_mat[None] * pw[out_pow][:, :, None, :]
        ccs.append(jnp.transpose(cw, (1, 3, 0, 2)).reshape(g, SSM_STATE, tc * SSM_GROUP))
        a16s.append(pw[tc])
    m = ms[0] + ms[1]
    bc = jnp.concatenate([jnp.real(bcs[0]), jnp.real(bcs[1]), jnp.imag(bcs[0]), jnp.imag(bcs[1])], axis=-1)
    cc = jnp.concatenate([jnp.real(ccs[0]), jnp.real(ccs[1]), -jnp.imag(ccs[0]), -jnp.imag(ccs[1])], axis=1)
    a16 = jnp.stack([jnp.concatenate([jnp.real(a16s[0]), jnp.real(a16s[1])], axis=-1),
                     jnp.concatenate([jnp.imag(a16s[0]), jnp.imag(a16s[1])], axis=-1)], axis=1)
    g = m.shape[0]
    dtile = jnp.tile(d_skip.astype(F32).reshape(g, 1, SSM_GROUP), (1, 1, tc))
    return m.astype(BF16), bc.astype(BF16), cc.astype(BF16), a16.astype(F32), dtile


def _s5_kernel(u_ref, m_ref, bc_ref, cc_ref, a_ref, d_ref, y_ref, vre_scr, vim_scr, hf_re, hb_re, hf_im, hb_im,
               *, nb, nchunk, nctx):
    p = SSM_STATE
    u = u_ref[0]
    ub = u.astype(BF16)
    v = jnp.dot(ub, bc_ref[0], preferred_element_type=F32)
    vre_scr[...] = v[:, 0:2 * p]
    vim_scr[...] = v[:, 2 * p:4 * p]
    ar = a_ref[0, 0:1, :]
    ai = a_ref[0, 1:2, :]
    fwd_lanes = lax.broadcasted_iota(jnp.int32, (nb, 2 * p), 1) < p
    hre = jnp.zeros((nb, 2 * p), F32)
    him = jnp.zeros((nb, 2 * p), F32)
    for s in range(nchunk):
        cf = s
        cb = (nctx - 1 - s) if s < nctx else (nchunk - 1 - (s - nctx))
        rf = pl.ds(cf, nb, stride=nchunk)
        rb = pl.ds(cb, nb, stride=nchunk)
        hf_re[rf, :] = hre
        hb_re[rb, :] = hre
        hf_im[rf, :] = him
        hb_im[rb, :] = him
        vre = jnp.where(fwd_lanes, vre_scr[rf, :], vre_scr[rb, :])
        vim = jnp.where(fwd_lanes, vim_scr[rf, :], vim_scr[rb, :])
        hre, him = ar * hre - ai * him + vre, ar * him + ai * hre + vim
    is_fwd = lax.broadcasted_iota(jnp.int32, (u.shape[0], 2 * p), 1) < p
    hp = jnp.concatenate([jnp.where(is_fwd, hf_re[...], hb_re[...]),
                          jnp.where(is_fwd, hf_im[...], hb_im[...])], axis=1)
    y = jnp.dot(ub, m_ref[0], preferred_element_type=F32)
    y = y + jnp.dot(hp.astype(BF16), cc_ref[0], preferred_element_type=F32)
    y_ref[0] = y + u * d_ref[0]


def _s5_mix(u, ops, nb, lc):
    m, bc, cc, a16, dtile = ops
    b, s, w = u.shape
    g = w // SSM_GROUP
    tc = SSM_CHUNK
    nchunk = s // tc
    r = nchunk * nb
    k = tc * SSM_GROUP
    ug = u.reshape(b * s * SSM_GROUP, g).T.reshape(g, r, k)
    yg = pl.pallas_call(
        functools.partial(_s5_kernel, nb=nb, nchunk=nchunk, nctx=lc // tc),
        grid=(g,),
        in_specs=[
            pl.BlockSpec((1, r, k), lambda i: (i, 0, 0)),
            pl.BlockSpec((1, k, k), lambda i: (i, 0, 0)),
            pl.BlockSpec((1, k, 4 * SSM_STATE), lambda i: (i, 0, 0)),
            pl.BlockSpec((1, 4 * SSM_STATE, k), lambda i: (i, 0, 0)),
            pl.BlockSpec((1, 2, 2 * SSM_STATE), lambda i: (i, 0, 0)),
            pl.BlockSpec((1, 1, k), lambda i: (i, 0, 0)),
        ],
        out_specs=pl.BlockSpec((1, r, k), lambda i: (i, 0, 0)),
        out_shape=jax.ShapeDtypeStruct((g, r, k), F32),
        scratch_shapes=[pltpu.VMEM((r, 2 * SSM_STATE), F32)] * 6,
        compiler_params=_cparams(("parallel",)),
        name="s5_mix",
    )(ug, m, bc, cc, a16, dtile)
    return yg.reshape(g, b * s * SSM_GROUP).T.reshape(b, s, w)


def _rope_tables(lc, lx):
    n = ATTN_HEAD_DIM // 4
    freq = ROPE_THETA ** (-jnp.arange(n, dtype=F32) / n)
    t = jnp.arange(lx, dtype=jnp.int32)
    rows = (t // GRID_W).astype(F32)[:, None] * freq[None, :]
    cols = (t % GRID_W).astype(F32)[:, None] * freq[None, :]
    cos = jnp.concatenate([jnp.cos(rows), jnp.cos(rows), jnp.cos(cols), jnp.cos(cols)], axis=-1)
    sin = jnp.concatenate([-jnp.sin(rows), jnp.sin(rows), -jnp.sin(cols), jnp.sin(cols)], axis=-1)
    cos = jnp.concatenate([jnp.ones((lc, ATTN_HEAD_DIM), F32), cos], axis=0)
    sin = jnp.concatenate([jnp.zeros((lc, ATTN_HEAD_DIM), F32), sin], axis=0)
    return cos, sin


def _qk_prep_kernel(q_ref, k_ref, v_ref, cos_ref, sin_ref, qg_ref, kg_ref, qo_ref, ko_ref, vo_ref):
    hd = ATTN_HEAD_DIM
    cos = cos_ref[...]
    sin = sin_ref[...]
    lane = lax.broadcasted_iota(jnp.int32, cos.shape, 1)
    first = (lane % (hd // 2)) < (hd // 4)

    def prep(x, g):
        y = x * lax.rsqrt(jnp.mean(x * x, axis=-1, keepdims=True) + EPS) * g
        partner = jnp.where(first, pltpu.roll(y, hd - hd // 4, 1), pltpu.roll(y, hd // 4, 1))
        return y * cos + partner * sin

    for h in range(ATTN_Q_PER_KV):
        sl = slice(h * hd, (h + 1) * hd)
        qo_ref[0, :, sl] = prep(q_ref[0, :, sl], qg_ref[...]).astype(qo_ref.dtype)
    ko_ref[0] = prep(k_ref[0], kg_ref[...]).astype(ko_ref.dtype)
    vo_ref[0] = v_ref[0].astype(vo_ref.dtype)


def _qk_prep(p3, cos, sin, q_gain, k_gain, col_q, col_k, col_v):
    b, s, _ = p3.shape
    hd = ATTN_HEAD_DIM
    qw = ATTN_Q_PER_KV * hd
    tq = _pick(s, (544, 256, 128))
    return pl.pallas_call(
        _qk_prep_kernel,
        grid=(b, s // tq, ATTN_KV_HEADS),
        in_specs=[
            pl.BlockSpec((1, tq, qw), lambda i, j, kv: (i, j, col_q // qw + kv)),
            pl.BlockSpec((1, tq, hd), lambda i, j, kv: (i, j, col_k // hd + kv)),
            pl.BlockSpec((1, tq, hd), lambda i, j, kv: (i, j, col_v // hd + kv)),
            pl.BlockSpec((tq, hd), lambda i, j, kv: (j, 0)),
            pl.BlockSpec((tq, hd), lambda i, j, kv: (j, 0)),
            pl.BlockSpec((1, hd), lambda i, j, kv: (0, 0)),
            pl.BlockSpec((1, hd), lambda i, j, kv: (0, 0)),
        ],
        out_specs=[
            pl.BlockSpec((1, tq, qw), lambda i, j, kv: (i, j, kv)),
            pl.BlockSpec((1, tq, hd), lambda i, j, kv: (i, j, kv)),
            pl.BlockSpec((1, tq, hd), lambda i, j, kv: (i, j, kv)),
        ],
        out_shape=[
            jax.ShapeDtypeStruct((b, s, ATTN_KV_HEADS * qw), BF16),
            jax.ShapeDtypeStruct((b, s, ATTN_KV_HEADS * hd), BF16),
            jax.ShapeDtypeStruct((b, s, ATTN_KV_HEADS * hd), BF16),
        ],
        compiler_params=_cparams(("parallel", "parallel", "parallel")),
        name="qk_prep",
    )(p3, p3, p3, cos, sin, q_gain.reshape(1, hd), k_gain.reshape(1, hd))


def _attn_kernel(q_ref, k_ref, v_ref, o_ref, *, lc, tq):
    hd = ATTN_HEAD_DIM
    c = (hd ** -0.5) * math.log2(math.e)

    def run(kk, vv):
        def scores(h):
            return lax.dot_general(q_ref[0, :, h * hd:(h + 1) * hd], kk, (((1,), (1,)), ((), ())),
                                   preferred_element_type=F32)

        s_next = scores(0)
        for h in range(ATTN_Q_PER_KV):
            s = s_next
            if h + 1 < ATTN_Q_PER_KV:
                s_next = scores(h + 1)
            m = jnp.max(s, axis=-1, keepdims=True)
            p = jnp.exp2((s - m) * c)
            l = jnp.sum(p, axis=-1, keepdims=True)
            o = jnp.dot(p.astype(BF16), vv, preferred_element_type=F32)
            o_ref[0, :, h * hd:(h + 1) * hd] = (o / l).astype(o_ref.dtype)

    is_ctx_tile = pl.program_id(2) * tq < lc

    @pl.when(jnp.logical_not(is_ctx_tile))
    def _():
        run(k_ref[0], v_ref[0])

    @pl.when(is_ctx_tile)
    def _():
        run(k_ref[0, :lc, :], v_ref[0, :lc, :])


def _attention(qh, kh, vh, lc):
    b, s, _ = qh.shape
    hd = ATTN_HEAD_DIM
    qw = ATTN_Q_PER_KV * hd
    tq = _pick(lc, (256, 128))
    assert s % tq == 0
    return pl.pallas_call(
        functools.partial(_attn_kernel, lc=lc, tq=tq),
        grid=(b, ATTN_KV_HEADS, s // tq),
        in_specs=[
            pl.BlockSpec((1, tq, qw), lambda i, kv, j: (i, j, kv)),
            pl.BlockSpec((1, s, hd), lambda i, kv, j: (i, 0, kv)),
            pl.BlockSpec((1, s, hd), lambda i, kv, j: (i, 0, kv)),
        ],
        out_specs=pl.BlockSpec((1, tq, qw), lambda i, kv, j: (i, j, kv)),
        out_shape=jax.ShapeDtypeStruct((b, s, ATTN_KV_HEADS * qw), BF16),
        compiler_params=_cparams(("parallel", "parallel", "parallel")),
        name="attention",
    )(qh, kh, vh)


def _split3(x):
    hi = x.astype(BF16)
    r = x - hi.astype(F32)
    mid = r.astype(BF16)
    lo = (r - mid.astype(F32)).astype(BF16)
    return hi, mid, lo


def _log_sigmoid(x):
    return jnp.minimum(x, 0.0) - jnp.log(1.0 + jnp.exp(-jnp.abs(x)))


def _mlstm_kernel(q_ref, k_ref, v_ref, gc_ref, gr_ref, bc_ref, br_ref, o_ref, c_scr, n_scr, m_scr):
    nh, dk, dv, cl = MLSTM_HEADS, MLSTM_QK_DIM, MLSTM_V_DIM, MLSTM_CHUNK
    nb = q_ref.shape[0]
    d = pl.program_id(0)
    c = pl.program_id(1)

    @pl.when(c == 0)
    def _():
        c_scr[...] = jnp.zeros_like(c_scr)
        n_scr[...] = jnp.zeros_like(n_scr)
        m_scr[...] = jnp.zeros_like(m_scr)

    row = lax.broadcasted_iota(jnp.int32, (cl, cl), 0)
    col = lax.broadcasted_iota(jnp.int32, (cl, cl), 1)
    sign = jnp.where(d == 0, 1, -1)
    mask = sign * (row - col) >= 0
    tm = jnp.where(mask, 1.0, 0.0).astype(BF16)
    tmt = jnp.where(sign * (col - row) >= 0, 1.0, 0.0).astype(BF16)

    def pick_c(a, k):
        return jnp.where(d == 0, a[:, k:k + 1], a[:, 2 * nh + k:2 * nh + k + 1])

    def pick_r(a, k):
        return jnp.where(d == 0, a[k:k + 1, :], a[2 * nh + k:2 * nh + k + 1, :])

    nt = (((1,), (1,)), ((), ()))
    tn = (((0,), (0,)), ((), ()))
    heads = [(bi, h) for bi in range(nb) for h in range(nh)]

    gcs = [gc_ref[bi] + bc_ref[...] for bi in range(nb)]
    grs = [gr_ref[bi, 0] + br_ref[...] for bi in range(nb)]
    lf_cs = [_log_sigmoid(g) for g in gcs]
    lf_rs = [_log_sigmoid(g) for g in grs]
    cum_cs = [sum(jnp.dot(tm, part, preferred_element_type=F32) for part in _split3(x)) for x in lf_cs]
    cum_rs = [sum(jnp.dot(part, tmt, preferred_element_type=F32) for part in _split3(x)) for x in lf_rs]
    totals = [jnp.sum(x, axis=0, keepdims=True) for x in lf_cs]

    qs = [q_ref[bi][:, h * dk:(h + 1) * dk] * (dk ** -0.5) for bi, h in heads]
    ks = [k_ref[bi][:, h * dk:(h + 1) * dk] for bi, h in heads]
    vs = [v_ref[bi][:, h * dv:(h + 1) * dv] for bi, h in heads]
    qbs = [x.astype(BF16) for x in qs]
    kbs = [x.astype(BF16) for x in ks]
    m_sts = [m_scr[i][:, 0:1] for i in range(len(heads))]
    c_sts = [c_scr[i] for i in range(len(heads))]
    n_sts = [n_scr[i] for i in range(len(heads))]

    s_raw = [lax.dot_general(qbs[i], kbs[i], nt, preferred_element_type=F32) for i in range(len(heads))]
    cqs = [lax.dot_general(qbs[i], c_sts[i].astype(BF16), nt, preferred_element_type=F32)
           for i in range(len(heads))]

    cf_c = [pick_c(cum_cs[bi], nh + h) for bi, h in heads]
    cf_r = [pick_r(cum_rs[bi], nh + h) for bi, h in heads]
    li_c = [pick_c(gcs[bi], h) for bi, h in heads]
    li_r = [pick_r(grs[bi], h) for bi, h in heads]
    tot = [pick_c(totals[bi], nh + h) for bi, h in heads]

    log_d = [jnp.where(mask, cf_c[i] - cf_r[i] + li_r[i], -jnp.inf) for i in range(len(heads))]
    log_inter = [cf_c[i] + m_sts[i] for i in range(len(heads))]
    m_row = [jnp.maximum(log_inter[i], jnp.max(log_d[i], axis=-1, keepdims=True)) for i in range(len(heads))]
    s = [s_raw[i] * jnp.exp(log_d[i] - m_row[i]) for i in range(len(heads))]
    w_inter = [jnp.exp(log_inter[i] - m_row[i]) for i in range(len(heads))]
    sv = [jnp.dot(s[i].astype(BF16), vs[i].astype(BF16), preferred_element_type=F32) for i in range(len(heads))]
    den = [jnp.sum(s[i], axis=-1, keepdims=True)
           + w_inter[i] * jnp.sum(qs[i] * n_sts[i], axis=-1, keepdims=True) for i in range(len(heads))]
    for i, (bi, h) in enumerate(heads):
        num = sv[i] + w_inter[i] * cqs[i]
        o_ref[0, bi, :, h * dv:(h + 1) * dv] = num / jnp.maximum(jnp.abs(den[i]), jnp.exp(-m_row[i]))

    log_w = [tot[i] - cf_c[i] + li_c[i] for i in range(len(heads))]
    m_new = [jnp.maximum(tot[i] + m_sts[i], jnp.max(log_w[i], axis=0, keepdims=True)) for i in range(len(heads))]
    decay = [jnp.exp(tot[i] + m_sts[i] - m_new[i]) for i in range(len(heads))]
    w = [jnp.exp(log_w[i] - m_new[i]) for i in range(len(heads))]
    kv = [lax.dot_general((w[i] * vs[i]).astype(BF16), kbs[i], tn, preferred_element_type=F32)
          for i in range(len(heads))]
    for i in range(len(heads)):
        c_scr[i] = decay[i] * c_sts[i] + kv[i]
        n_scr[i] = decay[i] * n_sts[i] + jnp.sum(w[i] * ks[i], axis=0, keepdims=True)
        m_scr[i] = jnp.broadcast_to(m_new[i], (1, LANES))


def _mlstm(p3, gate_bias, lc, col_q, col_k, col_v, col_g):
    b, s, _ = p3.shape
    nh, dk, dv, cl = MLSTM_HEADS, MLSTM_QK_DIM, MLSTM_V_DIM, MLSTM_CHUNK
    nc = s // cl
    nctx = lc // cl
    ng = 4 * nh
    g_rows = jnp.transpose(p3[:, :, col_g:col_g + ng].reshape(b, nc, cl, ng), (0, 1, 3, 2))
    bias = gate_bias.astype(F32).reshape(ng)

    def chunk(d, c):
        bwd = jnp.where(c < nctx, nctx - 1 - c, nc - 1 - (c - nctx))
        return jnp.where(d == 0, c, bwd)

    return pl.pallas_call(
        _mlstm_kernel,
        grid=(2, nc),
        in_specs=[
            pl.BlockSpec((b, cl, nh * dk), lambda d, c: (0, chunk(d, c), col_q // (nh * dk))),
            pl.BlockSpec((b, cl, nh * dk), lambda d, c: (0, chunk(d, c), col_k // (nh * dk))),
            pl.BlockSpec((b, cl, nh * dv), lambda d, c: (0, chunk(d, c), col_v // (nh * dv))),
            pl.BlockSpec((b, cl, LANES), lambda d, c: (0, chunk(d, c), col_g // LANES)),
            pl.BlockSpec((b, 1, ng, cl), lambda d, c: (0, chunk(d, c), 0, 0)),
            pl.BlockSpec((1, LANES), lambda d, c: (0, 0)),
            pl.BlockSpec((ng, 1), lambda d, c: (0, 0)),
        ],
        out_specs=pl.BlockSpec((1, b, cl, nh * dv), lambda d, c: (d, 0, chunk(d, c), 0)),
        out_shape=jax.ShapeDtypeStruct((2, b, s, nh * dv), F32),
        scratch_shapes=[pltpu.VMEM((b * nh, dv, dk), F32), pltpu.VMEM((b * nh, 1, dk), F32),
                        pltpu.VMEM((b * nh, 1, LANES), F32)],
        compiler_params=_cparams(("arbitrary", "arbitrary")),
        name="mlstm",
    )(p3, p3, p3, p3, g_rows, jnp.pad(bias, (0, LANES - ng)).reshape(1, LANES), bias.reshape(ng, 1))


def _gelu_tanh(x):
    return 0.5 * x * (1.0 + jnp.tanh(math.sqrt(2.0 / math.pi) * (x + 0.044715 * (x * x * x))))


def _merge_kernel(ys_ref, att_ref, hf_ref, hb_ref, og_ref, gl_ref, wglu_ref, bglu_ref, ng_ref,
                  wbs_ref, wba_ref, wbm_ref, y_ref):
    d = y_ref.shape[-1]
    z = _gelu_tanh(ys_ref[...])
    s5 = z * jax.nn.sigmoid(jnp.dot(z.astype(BF16), wglu_ref[...], preferred_element_type=F32) + bglu_ref[...])
    hsum = hf_ref[0] + hb_ref[0]
    dv = MLSTM_V_DIM
    parts = []
    for h in range(MLSTM_HEADS):
        blk = hsum[:, h * dv:(h + 1) * dv]
        parts.append(blk * lax.rsqrt(jnp.mean(blk * blk, axis=-1, keepdims=True) + EPS))
    ml = jnp.concatenate(parts, axis=-1) * ng_ref[...] * jax.nn.sigmoid(og_ref[...])
    y = gl_ref[:, 0:d].astype(F32) * jnp.dot(s5.astype(BF16), wbs_ref[...], preferred_element_type=F32)
    y = y + gl_ref[:, d:2 * d].astype(F32) * jnp.dot(att_ref[...], wba_ref[...], preferred_element_type=F32)
    y = y + gl_ref[:, 2 * d:3 * d].astype(F32) * jnp.dot(ml.astype(BF16), wbm_ref[...], preferred_element_type=F32)
    y_ref[...] = y.astype(y_ref.dtype)


def _merge(ys, att, hfb, p_main, gates, wglu, bglu, ng, wbs, wba, wbm, col_o):
    t, ws = ys.shape
    wa = att.shape[1]
    wm = hfb.shape[-1]
    d = wbs.shape[1]
    tm = _pick(t, (256, 128))
    full = lambda shape: pl.BlockSpec(shape, lambda i: (0,) * len(shape))
    return pl.pallas_call(
        _merge_kernel,
        grid=(t // tm,),
        in_specs=[
            pl.BlockSpec((tm, ws), lambda i: (i, 0)),
            pl.BlockSpec((tm, wa), lambda i: (i, 0)),
            pl.BlockSpec((1, tm, wm), lambda i: (0, i, 0)),
            pl.BlockSpec((1, tm, wm), lambda i: (1, i, 0)),
            pl.BlockSpec((tm, wm), lambda i: (i, col_o // wm)),
            pl.BlockSpec((tm, 3 * d), lambda i: (i, 0)),
            full((ws, ws)), full((1, ws)), full((1, wm)),
            full((ws, d)), full((wa, d)), full((wm, d)),
        ],
        out_specs=pl.BlockSpec((tm, d), lambda i: (i, 0)),
        out_shape=jax.ShapeDtypeStruct((t, d), BF16),
        compiler_params=_cparams(("parallel",)),
        name="branch_merge",
    )(ys, att, hfb, hfb, p_main, gates, wglu, bglu.reshape(1, ws), ng.reshape(1, wm), wbs, wba, wbm)


def _out_proj_kernel(y_ref, w_ref, x_ref, mod_ref, o_ref, *, lc, tm):
    out = jnp.dot(y_ref[0], w_ref[...], preferred_element_type=F32)
    gate = _row_select(mod_ref, 2, _is_ctx_rows(tm, lc))
    o_ref[0] = x_ref[0] + gate * out


def _out_proj(y3, w_out, xa, mod, lc):
    b, s, d = xa.shape
    tm = _pick(s, (544, 256, 128))
    return pl.pallas_call(
        functools.partial(_out_proj_kernel, lc=lc, tm=tm),
        grid=(b, s // tm),
        in_specs=[
            pl.BlockSpec((1, tm, d), lambda i, j: (i, j, 0)),
            pl.BlockSpec((d, d), lambda i, j: (0, 0)),
            pl.BlockSpec((1, tm, d), lambda i, j: (i, j, 0)),
            pl.BlockSpec((1, 2, 6, d), lambda i, j: (i, 0, 0, 0)),
        ],
        out_specs=pl.BlockSpec((1, tm, d), lambda i, j: (i, j, 0)),
        out_shape=jax.ShapeDtypeStruct((b, s, d), F32),
        compiler_params=_cparams(("parallel", "parallel")),
        name="out_proj_residual",
    )(y3, w_out, xa, mod)


def _pair_split_matrix():
    half = MXU_COLS // 2
    n = jnp.arange(MXU_COLS)
    src = jnp.where(n < half, 2 * n, 2 * (n - half) + 1)
    return (jnp.arange(MXU_COLS)[:, None] == src[None, :]).astype(BF16)


def _pair_split_index(width):
    half = MXU_COLS // 2
    n = jnp.arange(width)
    blk, r = n // MXU_COLS, n % MXU_COLS
    return blk * MXU_COLS + jnp.where(r < half, 2 * r, 2 * (r - half) + 1)


def _split_pairs_kernel(w_ref, p_ref, o_ref):
    for cb in range(w_ref.shape[-1] // MXU_COLS):
        sl = slice(cb * MXU_COLS, (cb + 1) * MXU_COLS)
        o_ref[0, :, sl] = jnp.dot(w_ref[0, :, sl].astype(BF16), p_ref[...],
                                  preferred_element_type=F32).astype(o_ref.dtype)


def _split_pairs(w):
    ne, d, n = w.shape
    tc = _pick(n, (2 * MXU_COLS, MXU_COLS))
    return pl.pallas_call(
        _split_pairs_kernel,
        grid=(ne, n // tc),
        in_specs=[
            pl.BlockSpec((1, d, tc), lambda e, i: (e, 0, i)),
            pl.BlockSpec((MXU_COLS, MXU_COLS), lambda e, i: (0, 0)),
        ],
        out_specs=pl.BlockSpec((1, d, tc), lambda e, i: (e, 0, i)),
        out_shape=jax.ShapeDtypeStruct((ne, d, n), BF16),
        compiler_params=_cparams(("parallel", "parallel")),
        name="split_gate_linear",
    )(w, _pair_split_matrix())


def _expert_kernel(te_ref, nu_ref, tok_ref, tok_next_ref, h_hbm, *refs):
    npart = EXPERT_WEIGHT_PARTS
    wgl_refs, bgl_ref = refs[0:npart], refs[npart]
    wd_refs, bd_ref = refs[npart + 1:2 * npart + 1], refs[2 * npart + 1]
    o_ref, wd_scr, xbuf, sem = refs[2 * npart + 2:2 * npart + 6]
    i = pl.program_id(0)
    half = MXU_COLS // 2
    tm = o_ref.shape[0]
    nseg = h_hbm.shape[1]
    n_used = nu_ref[0]
    used = i < n_used
    slot = lax.rem(i, 2)
    new_expert = jnp.logical_or(i == 0, te_ref[i] != te_ref[jnp.maximum(i - 1, 0)])
    wc = wd_refs[0].shape[2]

    def start_rows(idx_ref, dst_slot):
        for r in range(tm):
            dst = xbuf.at[pl.ds((dst_slot * tm + r) * nseg, nseg)]
            pltpu.make_async_copy(h_hbm.at[idx_ref[0, 0, r]], dst, sem.at[dst_slot]).start()

    def wait_rows(dst_slot):
        dst = xbuf.at[pl.ds(dst_slot * tm * nseg, tm * nseg)]
        pltpu.make_async_copy(dst, dst, sem.at[dst_slot]).wait()

    @pl.when(i == 0)
    def _():
        start_rows(tok_ref, 0)

    @pl.when(jnp.logical_and(used, new_expert))
    def _():
        for part in range(npart):
            wd_scr[:, part * wc:(part + 1) * wc] = wd_refs[part][0].astype(BF16)

    @pl.when(used)
    def _():
        start_rows(tok_next_ref, 1 - slot)
        wait_rows(slot)

    @pl.when(used)
    def _():
        base = slot * (tm * nseg)
        x = jnp.concatenate([xbuf[pl.ds(base + j, tm, stride=nseg), :] for j in range(nseg)],
                            axis=1).astype(BF16)
        gu = jnp.concatenate([jnp.dot(x, w[0], preferred_element_type=F32) for w in wgl_refs], axis=1)
        gu = gu + bgl_ref[0]
        nblk = gu.shape[1] // MXU_COLS
        gate = jnp.concatenate([gu[:, cb * MXU_COLS:cb * MXU_COLS + half] for cb in range(nblk)], axis=1)
        lin = jnp.concatenate([gu[:, cb * MXU_COLS + half:(cb + 1) * MXU_COLS] for cb in range(nblk)], axis=1)
        gate = jnp.minimum(gate, SWIGLU_LIMIT)
        lin = jnp.clip(lin, -SWIGLU_LIMIT, SWIGLU_LIMIT)
        act = gate * jax.nn.sigmoid(SWIGLU_ALPHA * gate) * (lin + 1.0)
        y = jnp.dot(act.astype(BF16), wd_scr[...], preferred_element_type=F32) + bd_ref[0]
        o_ref[...] = y.astype(o_ref.dtype)

    @pl.when(i == n_used - 1)
    def _():
        wait_rows(1 - slot)

    @pl.when(jnp.logical_not(used))
    def _():
        o_ref[...] = jnp.zeros_like(o_ref)


def _experts(h_rows, slot_tok, tile_expert, n_used, wgl, bgl, wd, bd):
    _, nseg, lanes = h_rows.shape
    d = nseg * lanes
    n_tiles, tm = slot_tok.shape
    _, _, ff2 = wgl.shape
    ff = ff2 // 2
    npart = EXPERT_WEIGHT_PARTS
    col_part = lambda part: (lambda i, te, nu: (te[i], 0, part))
    grid_spec = pltpu.PrefetchScalarGridSpec(
        num_scalar_prefetch=2,
        grid=(n_tiles,),
        in_specs=(
            [pl.BlockSpec((1, 1, tm), lambda i, te, nu: (i, 0, 0), memory_space=pltpu.SMEM),
             pl.BlockSpec((1, 1, tm), lambda i, te, nu: (jnp.minimum(i + 1, n_tiles - 1), 0, 0),
                          memory_space=pltpu.SMEM),
             pl.BlockSpec(memory_space=pl.ANY)]
            + [pl.BlockSpec((1, d, ff2 // npart), col_part(part)) for part in range(npart)]
            + [pl.BlockSpec((1, 1, ff2), lambda i, te, nu: (te[i], 0, 0))]
            + [pl.BlockSpec((1, ff, d // npart), col_part(part)) for part in range(npart)]
            + [pl.BlockSpec((1, 1, d), lambda i, te, nu: (te[i], 0, 0))]),
        out_specs=pl.BlockSpec((tm, d), lambda i, te, nu: (i, 0)),
        scratch_shapes=[pltpu.VMEM((ff, d), BF16),
                        pltpu.VMEM((2 * tm * nseg, lanes), F32),
                        pltpu.SemaphoreType.DMA((2,))],
    )
    tok3 = slot_tok[:, None, :]
    return pl.pallas_call(
        _expert_kernel,
        grid_spec=grid_spec,
        out_shape=jax.ShapeDtypeStruct((n_tiles * tm, d), BF16),
        compiler_params=_cparams(("arbitrary",)),
        name="experts",
    )(tile_expert, n_used, tok3, tok3, h_rows, *([wgl] * npart), bgl, *([wd] * npart), bd)


def _route(top_idx, n_slots):
    tm = EXPERT_TILE
    e = top_idx.reshape(-1)
    n = e.shape[0]
    onehot = (e[:, None] == jnp.arange(N_EXPERTS, dtype=jnp.int32)[None, :]).astype(jnp.int32)
    counts = jnp.sum(onehot, axis=0)
    rank = jnp.take_along_axis(jnp.cumsum(onehot, axis=0) - onehot, e[:, None], axis=1)[:, 0]
    padded = ((counts + tm - 1) // tm) * tm
    ends = jnp.cumsum(padded)
    dest = (ends - padded)[e] + rank
    n_used = (ends[-1] // tm).astype(jnp.int32)
    tile_start = jnp.arange(n_slots // tm, dtype=jnp.int32) * tm
    tile_expert = jnp.sum((ends[None, :] <= tile_start[:, None]).astype(jnp.int32), axis=1)
    last_used = jnp.sum((ends <= (n_used - 1) * tm).astype(jnp.int32))
    tile_expert = jnp.minimum(jnp.where(tile_start < ends[-1], tile_expert, last_used), N_EXPERTS - 1)
    src = jnp.zeros((n_slots,), jnp.int32).at[dest].set(jnp.arange(n, dtype=jnp.int32) // TOP_K)
    return src, dest, tile_expert.astype(jnp.int32), n_used.reshape(1)


def _combine_kernel(y0_ref, y1_ref, y2_ref, y3_ref, tw_ref, x_ref, mod_ref, fg_ref, o_ref, *, lc, tm, final):
    tw = tw_ref[0]
    moe = tw[:, 0:1] * y0_ref[0, 0].astype(F32)
    for k, y_ref in enumerate((y1_ref, y2_ref, y3_ref), start=1):
        moe = moe + tw[:, k:k + 1] * y_ref[0, 0].astype(F32)
    if final:
        xn = x_ref[0] + mod_ref[0, 1, 5:6, :] * moe
        xn = xn * lax.rsqrt(jnp.mean(xn * xn, axis=-1, keepdims=True) + EPS) * fg_ref[...]
    else:
        xn = x_ref[0] + _row_select(mod_ref, 5, _is_ctx_rows(tm, lc)) * moe
    o_ref[0] = xn


def _combine(yg, top_w, xa, mod, final_g, lc, final):
    b, s, d = xa.shape
    assert TOP_K == 4
    if final:
        tm = _pick(lc, (256, 128))
        assert (s - lc) % tm == 0
        off, rows = lc // tm, s - lc
    else:
        tm = _pick(s, (544, 256, 128))
        off, rows = 0, s
    y_specs = [pl.BlockSpec((1, 1, tm, d), functools.partial(lambda i, j, k: (k, i, j + off, 0), k=k))
               for k in range(TOP_K)]
    return pl.pallas_call(
        functools.partial(_combine_kernel, lc=lc, tm=tm, final=final),
        grid=(b, rows // tm),
        in_specs=y_specs + [
            pl.BlockSpec((1, tm, TOP_K), lambda i, j: (i, j + off, 0)),
            pl.BlockSpec((1, tm, d), lambda i, j: (i, j + off, 0)),
            pl.BlockSpec((1, 2, 6, d), lambda i, j: (i, 0, 0, 0)),
            pl.BlockSpec((1, d), lambda i, j: (0, 0)),
        ],
        out_specs=pl.BlockSpec((1, tm, d), lambda i, j: (i, j, 0)),
        out_shape=jax.ShapeDtypeStruct((b, rows, d), F32),
        compiler_params=_cparams(("parallel", "parallel")),
        name="moe_combine",
    )(yg, yg, yg, yg, top_w, xa, mod, final_g.reshape(1, d))


def kernel(x, c, ctx, c_ctx, w_ada, b_ada, norm_g, w_in, ssm_a_re, ssm_a_im, ssm_log_dt, ssm_b_re, ssm_b_im, ssm_c_re, ssm_c_im, ssm_d, ssm_w_glu, ssm_b_glu, attn_q_gain, attn_k_gain, mlstm_gate_bias, mlstm_norm_g, w_branch_ssm, w_branch_attn, w_branch_mlstm, w_out, w_router, b_router, w_gate_up, b_gate_up, w_down, b_down, final_g):
    b, lx, d = x.shape
    lc = ctx.shape[1]
    s = lc + lx
    t = b * s
    depth = w_in.shape[0]
    assert b + 1 <= 8 and lc % MLSTM_CHUNK == 0 and lx % MLSTM_CHUNK == 0

    ssm_w = ssm_d.shape[1]
    attn_w = w_branch_attn.shape[1]
    kv_w = ATTN_KV_HEADS * ATTN_HEAD_DIM
    mqk_w = MLSTM_HEADS * MLSTM_QK_DIM
    mv_w = MLSTM_HEADS * MLSTM_V_DIM
    n_gates = 4 * MLSTM_HEADS
    col_q = ssm_w
    col_k = col_q + attn_w
    col_v = col_k + kv_w
    col_mq = col_v + kv_w
    col_mk = col_mq + mqk_w
    col_mv = col_mk + mqk_w
    col_mo = col_mv + mv_w
    col_mg = col_mo + mv_w
    col_bg = col_mg + n_gates
    main_pad = (-(col_mg + n_gates)) % 256

    xa = jnp.concatenate([ctx, x], axis=1)
    cond8 = jnp.zeros((8, d), F32).at[0].set(c_ctx).at[1:1 + b].set(c)
    mods = _ada_modulation(cond8, w_ada, b_ada).reshape(depth, 8, 6, d)
    cos, sin = _rope_tables(lc, lx)
    n_groups = ssm_w // SSM_GROUP

    def regroup(a, axis):
        shp = a.shape
        a = a.reshape(shp[:axis] + (n_groups, SSM_GROUP) + shp[axis + 1:])
        return jnp.swapaxes(a, axis, axis + 1).reshape(shp)

    ne, ff2 = w_gate_up.shape[1], w_gate_up.shape[3]
    wgl_all = _split_pairs(w_gate_up.reshape(depth * ne, d, ff2))
    bgl_all = b_gate_up.reshape(depth * ne, ff2)[:, _pair_split_index(ff2)][:, None, :]
    wd_all = w_down.reshape(depth * ne, ff2 // 2, d)
    bd_all = b_down.reshape(depth * ne, 1, d)

    for l in range(depth):
        last = l == depth - 1
        mod = jnp.stack([jnp.broadcast_to(mods[l, 0], (b, 6, d)), mods[l, 1:1 + b]], axis=1)

        h = _norm_mod(xa, norm_g[l, 0], mod, lc, 0).reshape(t, d)
        w_main = jnp.concatenate(
            [regroup(w_in[l, :, :ssm_w], 1), w_in[l, :, ssm_w:col_bg], jnp.zeros((d, main_pad), F32)],
            axis=1).astype(BF16)
        p_main = _gemm(h, w_main, F32)
        gates = _gemm(h, w_in[l, :, col_bg:].astype(BF16), BF16, act="sigmoid")
        p3 = p_main.reshape(b, s, -1)

        ops = _s5_operators(ssm_a_re[l], ssm_a_im[l], ssm_log_dt[l], ssm_b_re[l], ssm_b_im[l],
                            ssm_c_re[l], ssm_c_im[l], ssm_d[l])
        ys = _s5_mix(p3[:, :, :ssm_w], ops, b, lc)

        qh, kh, vh = _qk_prep(p3, cos, sin, attn_q_gain[l], attn_k_gain[l], col_q, col_k, col_v)
        att = _attention(qh, kh, vh, lc)

        hfb = _mlstm(p3, mlstm_gate_bias[l], lc, col_mq, col_mk, col_mv, col_mg)

        y = _merge(ys.reshape(t, ssm_w), att.reshape(t, attn_w), hfb.reshape(2, t, mv_w), p_main, gates,
                   regroup(regroup(ssm_w_glu[l], 0), 1).astype(BF16), regroup(ssm_b_glu[l], 0), mlstm_norm_g[l],
                   regroup(w_branch_ssm[l], 0).astype(BF16), w_branch_attn[l].astype(BF16),
                   w_branch_mlstm[l].astype(BF16), col_mo)
        xa = _out_proj(y.reshape(b, s, d), w_out[l].astype(BF16), xa, mod, lc)

        h2, top_idx, top_w = _norm_router(xa, norm_g[l, 1], mod, w_router[l], b_router[l], lc)
        n_slots = t * TOP_K + N_EXPERTS * EXPERT_TILE
        src, dest, tile_expert, n_used = _route(top_idx, n_slots)
        ysorted = _experts(h2.reshape(t, d // LANES, LANES), src.reshape(-1, EXPERT_TILE),
                           tile_expert + l * ne, n_used, wgl_all, bgl_all, wd_all, bd_all)
        yg = ysorted.at[dest.reshape(t, TOP_K).T].get(mode="promise_in_bounds").reshape(TOP_K, b, s, d)
        xa = _combine(yg, top_w, xa, mod, final_g, lc, last)

    return xa
```

```python
import functools
import math

import jax
import jax.numpy as jnp
from jax import lax
from jax.experimental import pallas as pl
from jax.experimental.pallas import tpu as pltpu

F32 = jnp.float32
BF16 = jnp.bfloat16

GRID_W = 64
EPS = 1e-6
SSM_GROUP = 16
SSM_STATE = 64
SSM_CHUNK = 8
SSM_PACK = 8
ATTN_HEAD_DIM = 128
ATTN_KV_HEADS = 2
ATTN_Q_PER_KV = 4
ROPE_THETA = 10000.0
MLSTM_HEADS = 4
MLSTM_QK_DIM = 64
MLSTM_V_DIM = 128
MLSTM_CHUNK = 64
N_EXPERTS = 32
TOP_K = 4
SWIGLU_LIMIT = 7.0
SWIGLU_ALPHA = 1.702

LANES = 128
MXU_COLS = 256
VMEM_LIMIT_BYTES = 56 * 1024 * 1024
EXPERT_TILE = 256
EXPERT_WEIGHT_PARTS = 4

def _cparams(sem):
    return pltpu.CompilerParams(dimension_semantics=sem, vmem_limit_bytes=VMEM_LIMIT_BYTES)


def _pick(n, cands):
    for c in cands:
        if n % c == 0:
            return c
    raise ValueError(f"no tile for {n} in {cands}")


def _ada_kernel(c_ref, w_ref, b_ref, o_ref):
    c = c_ref[...]
    s = c * jax.nn.sigmoid(c)
    o_ref[0] = jnp.dot(s, w_ref[0], preferred_element_type=F32) + b_ref[0]


def _ada_modulation(cond8, w_ada, b_ada):
    nl, d, n = w_ada.shape
    tn = _pick(n, (1024, 512, 256, 128))
    return pl.pallas_call(
        _ada_kernel,
        grid=(nl, n // tn),
        in_specs=[
            pl.BlockSpec((8, d), lambda l, j: (0, 0)),
            pl.BlockSpec((1, d, tn), lambda l, j: (l, 0, j)),
            pl.BlockSpec((1, 1, tn), lambda l, j: (l, 0, j)),
        ],
        out_specs=pl.BlockSpec((1, 8, tn), lambda l, j: (l, 0, j)),
        out_shape=jax.ShapeDtypeStruct((nl, 8, n), F32),
        compiler_params=_cparams(("parallel", "parallel")),
        name="ada_modulation",
    )(cond8, w_ada, b_ada.reshape(nl, 1, n))


def _row_select(mod_ref, k, is_ctx):
    return jnp.where(is_ctx, mod_ref[0, 0, k:k + 1, :], mod_ref[0, 1, k:k + 1, :])


def _modulated_norm(x, g, mod_ref, k_shift, is_ctx):
    ms = jnp.mean(x * x, axis=-1, keepdims=True)
    y = x * lax.rsqrt(ms + EPS) * g
    return y * (1.0 + _row_select(mod_ref, k_shift + 1, is_ctx)) + _row_select(mod_ref, k_shift, is_ctx)


def _is_ctx_rows(tm, lc):
    pos = pl.program_id(1) * tm + lax.broadcasted_iota(jnp.int32, (tm, 1), 0)
    return pos < lc


def _norm_mod_kernel(x_ref, g_ref, mod_ref, o_ref, *, lc, tm, k_shift):
    h = _modulated_norm(x_ref[0], g_ref[...], mod_ref, k_shift, _is_ctx_rows(tm, lc))
    o_ref[0] = h.astype(o_ref.dtype)


def _norm_mod(xa, g, mod, lc, k_shift):
    b, s, d = xa.shape
    tm = _pick(s, (544, 256, 128))
    return pl.pallas_call(
        functools.partial(_norm_mod_kernel, lc=lc, tm=tm, k_shift=k_shift),
        grid=(b, s // tm),
        in_specs=[
            pl.BlockSpec((1, tm, d), lambda i, j: (i, j, 0)),
            pl.BlockSpec((1, d), lambda i, j: (0, 0)),
            pl.BlockSpec((1, 2, 6, d), lambda i, j: (i, 0, 0, 0)),
        ],
        out_specs=pl.BlockSpec((1, tm, d), lambda i, j: (i, j, 0)),
        out_shape=jax.ShapeDtypeStruct((b, s, d), BF16),
        compiler_params=_cparams(("parallel", "parallel")),
        name="norm_mod",
    )(xa, g.reshape(1, d), mod)


def _norm_and_route(x, g_ref, mod_ref, wr_ref, br_ref, h_ref, idx_ref, w_ref, is_ctx):
    h = _modulated_norm(x, g_ref[...], mod_ref, 3, is_ctx)
    for j in range(h.shape[1] // LANES):
        h_ref[0, :, j, :] = h[:, j * LANES:(j + 1) * LANES]
    logits = jnp.dot(h, wr_ref[...], preferred_element_type=F32,
                     precision=lax.Precision.HIGHEST) + br_ref[...]
    lane = lax.broadcasted_iota(jnp.int32, logits.shape, 1).astype(F32)
    vals, idxs = [], []
    cur = logits
    for _ in range(TOP_K):
        mx = jnp.max(cur, axis=-1, keepdims=True)
        ix = jnp.min(jnp.where(cur == mx, lane, float(N_EXPERTS)), axis=-1, keepdims=True)
        vals.append(mx)
        idxs.append(ix)
        cur = jnp.where(lane == ix, -jnp.inf, cur)
    k_iota = lax.broadcasted_iota(jnp.int32, (logits.shape[0], TOP_K), 1)
    top = jnp.zeros((logits.shape[0], TOP_K), F32)
    top_i = jnp.zeros((logits.shape[0], TOP_K), F32)
    for k in range(TOP_K):
        top = jnp.where(k_iota == k, vals[k], top)
        top_i = jnp.where(k_iota == k, idxs[k], top_i)
    e = jnp.exp(top - vals[0])
    w_ref[0] = e / jnp.sum(e, axis=-1, keepdims=True)
    idx_ref[0] = top_i.astype(jnp.int32)


def _gemm_kernel(a_ref, w_ref, o_ref, *, act):
    acc = jnp.dot(a_ref[...], w_ref[...], preferred_element_type=F32)
    if act == "sigmoid":
        acc = jax.nn.sigmoid(acc)
    o_ref[...] = acc.astype(o_ref.dtype)


def _gemm(a, w, out_dtype, act=None, tn_cands=(1920, 1024, 512, 256, 128)):
    t, k = a.shape
    n = w.shape[1]
    tm = _pick(t, (1024, 768, 512, 256, 128))
    tn = _pick(n, tn_cands)
    return pl.pallas_call(
        functools.partial(_gemm_kernel, act=act),
        grid=(t // tm, n // tn),
        in_specs=[
            pl.BlockSpec((tm, k), lambda i, j: (i, 0)),
            pl.BlockSpec((k, tn), lambda i, j: (0, j)),
        ],
        out_specs=pl.BlockSpec((tm, tn), lambda i, j: (i, j)),
        out_shape=jax.ShapeDtypeStruct((t, n), out_dtype),
        compiler_params=_cparams(("parallel", "parallel")),
        name="gemm_" + (act or "plain"),
    )(a, w)


def _s5_operators(a_re, a_im, log_dt, b_re, b_im, c_re, c_im, d_skip):
    tc = SSM_CHUNK
    hp = lax.Precision.HIGHEST
    k_idx = jnp.arange(tc + 1, dtype=F32)
    lag = jnp.arange(tc)[None, :] - jnp.arange(tc)[:, None]
    ms, bcs, ccs, a16s = [], [], [], []
    for dirn in (0, 1):
        lam = lax.complex(a_re[dirn].astype(F32), a_im[dirn].astype(F32))
        dt = jnp.exp(log_dt[dirn].astype(F32))[:, None]
        pw = jnp.exp((lam * dt)[None] * k_idx[:, None, None])
        a_bar = pw[1]
        b_bar = ((a_bar - 1.0) / lam)[..., None] * lax.complex(b_re[dirn].astype(F32), b_im[dirn].astype(F32))
        c_mat = lax.complex(c_re[dirn].astype(F32), c_im[dirn].astype(F32))
        k_lag = jnp.real(jnp.einsum("ghp,kgp,gpj->kghj", c_mat, pw[:tc], b_bar, precision=hp))
        if dirn == 0:
            dist, valid = lag, lag >= 0
            end_pow = (tc - 1) - jnp.arange(tc)
            out_pow = jnp.arange(tc) + 1
        else:
            dist, valid = -lag, lag <= 0
            end_pow = jnp.arange(tc)
            out_pow = tc - jnp.arange(tc)
        kk = k_lag[jnp.clip(dist, 0, tc - 1)]
        kk = jnp.where(valid[:, :, None, None, None], kk, 0.0)
        ms.append(jnp.transpose(kk, (2, 0, 4, 1, 3)))
        bc = pw[end_pow][:, :, :, None] * b_bar[None]
        bcs.append(jnp.transpose(bc, (1, 0, 3, 2)))
        cw = c_mat[None] * pw[out_pow][:, :, None, :]
        ccs.append(jnp.transpose(cw, (1, 3, 0, 2)))
        a16s.append(pw[tc])
    gs = SSM_PACK
    n_groups = ms[0].shape[0]
    nsg = n_groups // gs
    eye = jnp.eye(gs, dtype=F32)

    def pack_rows_cols(x):
        x = x.reshape((nsg, gs) + x.shape[1:])
        x = jnp.transpose(x, (0, 2, 3, 4, 1, 5))[:, :, None]
        x = x * eye[None, None, :, None, None, :, None]
        return x.reshape(nsg, tc * gs * SSM_GROUP, tc * gs * SSM_GROUP)

    def pack_rows(x):
        x = x.reshape((nsg, gs) + x.shape[1:])
        x = jnp.transpose(x, (0, 2, 3, 1, 4))[:, :, None]
        x = x * eye[None, None, :, None, :, None]
        return x.reshape(nsg, tc * gs * SSM_GROUP, gs * SSM_STATE)

    def pack_cols(x):
        x = x.reshape((nsg, gs) + x.shape[1:])
        x = jnp.transpose(x, (0, 2, 3, 1, 4))[:, None]
        x = x * eye[None, :, None, None, :, None]
        return x.reshape(nsg, gs * SSM_STATE, tc * gs * SSM_GROUP)

    m = pack_rows_cols(ms[0] + ms[1])
    bc = jnp.concatenate([pack_rows(jnp.real(bcs[0])), pack_rows(jnp.real(bcs[1])),
                          pack_rows(jnp.imag(bcs[0])), pack_rows(jnp.imag(bcs[1]))], axis=-1)
    cc = jnp.concatenate([pack_cols(jnp.real(ccs[0])), pack_cols(jnp.real(ccs[1])),
                          pack_cols(-jnp.imag(ccs[0])), pack_cols(-jnp.imag(ccs[1]))], axis=1)
    a_f, a_b = a16s[0].reshape(nsg, gs * SSM_STATE), a16s[1].reshape(nsg, gs * SSM_STATE)
    a_chunk = jnp.stack([jnp.concatenate([jnp.real(a_f), jnp.real(a_b)], axis=-1),
                         jnp.concatenate([jnp.imag(a_f), jnp.imag(a_b)], axis=-1)], axis=1)
    dtile = jnp.tile(d_skip.astype(F32).reshape(nsg, 1, gs * SSM_GROUP), (1, 1, tc))
    return m.astype(BF16), bc.astype(BF16), cc.astype(BF16), a_chunk.astype(F32), dtile


def _s5_kernel(u_ref, m_ref, bc_ref, cc_ref, a_ref, d_ref, y_ref, vre_scr, vim_scr, hf_re, hb_re, hf_im, hb_im,
               *, nchunk, nctx):
    tc = SSM_CHUNK
    w = hf_re.shape[1]
    u = jnp.concatenate([u_ref[0, pl.ds(i, nchunk, stride=tc), :] for i in range(tc)], axis=1)
    ub = u.astype(BF16)
    v = jnp.dot(ub, bc_ref[0], preferred_element_type=F32)
    vre_scr[...] = v[:, 0:2 * w]
    vim_scr[...] = v[:, 2 * w:4 * w]
    ar = a_ref[0, 0:1, :]
    ai = a_ref[0, 1:2, :]
    hre = jnp.zeros((1, 2 * w), F32)
    him = jnp.zeros((1, 2 * w), F32)
    for s in range(nchunk):
        cf = s
        cb = (nctx - 1 - s) if s < nctx else (nchunk - 1 - (s - nctx))
        hf_re[cf:cf + 1, :] = hre[:, 0:w]
        hb_re[cb:cb + 1, :] = hre[:, w:2 * w]
        hf_im[cf:cf + 1, :] = him[:, 0:w]
        hb_im[cb:cb + 1, :] = him[:, w:2 * w]
        vre = jnp.concatenate([vre_scr[cf:cf + 1, 0:w], vre_scr[cb:cb + 1, w:2 * w]], axis=1)
        vim = jnp.concatenate([vim_scr[cf:cf + 1, 0:w], vim_scr[cb:cb + 1, w:2 * w]], axis=1)
        hre, him = ar * hre - ai * him + vre, ar * him + ai * hre + vim
    hp = jnp.concatenate([hf_re[...], hb_re[...], hf_im[...], hb_im[...]], axis=1)
    y = jnp.dot(ub, m_ref[0], preferred_element_type=F32)
    y = y + jnp.dot(hp.astype(BF16), cc_ref[0], preferred_element_type=F32)
    y = y + u * d_ref[0]
    for i in range(tc):
        y_ref[0, pl.ds(i, nchunk, stride=tc), :] = y[:, i * LANES:(i + 1) * LANES]


def _s5_mix(p3, ops, width, lc):
    m, bc, cc, a_chunk, dtile = ops
    b, s, _ = p3.shape
    tc = SSM_CHUNK
    nchunk = s // tc
    cw = SSM_PACK * SSM_GROUP
    assert cw == LANES and width % cw == 0 and s % tc == 0 and lc % tc == 0
    k = tc * cw
    sw = SSM_PACK * SSM_STATE
    return pl.pallas_call(
        functools.partial(_s5_kernel, nchunk=nchunk, nctx=lc // tc),
        grid=(width // cw, b),
        in_specs=[
            pl.BlockSpec((1, s, cw), lambda g, i: (i, 0, g)),
            pl.BlockSpec((1, k, k), lambda g, i: (g, 0, 0)),
            pl.BlockSpec((1, k, 4 * sw), lambda g, i: (g, 0, 0)),
            pl.BlockSpec((1, 4 * sw, k), lambda g, i: (g, 0, 0)),
            pl.BlockSpec((1, 2, 2 * sw), lambda g, i: (g, 0, 0)),
            pl.BlockSpec((1, 1, k), lambda g, i: (g, 0, 0)),
        ],
        out_specs=pl.BlockSpec((1, s, cw), lambda g, i: (i, 0, g)),
        out_shape=jax.ShapeDtypeStruct((b, s, width), F32),
        scratch_shapes=[pltpu.VMEM((nchunk, 2 * sw), F32)] * 2 + [pltpu.VMEM((nchunk, sw), F32)] * 4,
        compiler_params=_cparams(("parallel", "parallel")),
        name="s5_mix",
    )(p3, m, bc, cc, a_chunk, dtile)


def _rope_tables(lc, lx):
    n = ATTN_HEAD_DIM // 4
    freq = ROPE_THETA ** (-jnp.arange(n, dtype=F32) / n)
    t = jnp.arange(lx, dtype=jnp.int32)
    rows = (t // GRID_W).astype(F32)[:, None] * freq[None, :]
    cols = (t % GRID_W).astype(F32)[:, None] * freq[None, :]
    cos = jnp.concatenate([jnp.cos(rows), jnp.cos(rows), jnp.cos(cols), jnp.cos(cols)], axis=-1)
    sin = jnp.concatenate([-jnp.sin(rows), jnp.sin(rows), -jnp.sin(cols), jnp.sin(cols)], axis=-1)
    cos = jnp.concatenate([jnp.ones((lc, ATTN_HEAD_DIM), F32), cos], axis=0)
    sin = jnp.concatenate([jnp.zeros((lc, ATTN_HEAD_DIM), F32), sin], axis=0)
    return cos, sin


def _qk_prep_kernel(q_ref, k_ref, v_ref, cos_ref, sin_ref, qg_ref, kg_ref, qo_ref, ko_ref, vo_ref):
    hd = ATTN_HEAD_DIM
    cos = cos_ref[...]
    sin = sin_ref[...]
    lane = lax.broadcasted_iota(jnp.int32, cos.shape, 1)
    first = (lane % (hd // 2)) < (hd // 4)

    def prep(x, g):
        y = x * lax.rsqrt(jnp.mean(x * x, axis=-1, keepdims=True) + EPS) * g
        partner = jnp.where(first, pltpu.roll(y, hd - hd // 4, 1), pltpu.roll(y, hd // 4, 1))
        return y * cos + partner * sin

    for h in range(ATTN_Q_PER_KV):
        sl = slice(h * hd, (h + 1) * hd)
        qo_ref[0, :, sl] = prep(q_ref[0, :, sl], qg_ref[...]).astype(qo_ref.dtype)
    ko_ref[0] = prep(k_ref[0], kg_ref[...]).astype(ko_ref.dtype)
    vo_ref[0] = v_ref[0].astype(vo_ref.dtype)


def _qk_prep(p3, cos, sin, q_gain, k_gain, col_q, col_k, col_v):
    b, s, _ = p3.shape
    hd = ATTN_HEAD_DIM
    qw = ATTN_Q_PER_KV * hd
    tq = _pick(s, (544, 256, 128))
    return pl.pallas_call(
        _qk_prep_kernel,
        grid=(b, s // tq, ATTN_KV_HEADS),
        in_specs=[
            pl.BlockSpec((1, tq, qw), lambda i, j, kv: (i, j, col_q // qw + kv)),
            pl.BlockSpec((1, tq, hd), lambda i, j, kv: (i, j, col_k // hd + kv)),
            pl.BlockSpec((1, tq, hd), lambda i, j, kv: (i, j, col_v // hd + kv)),
            pl.BlockSpec((tq, hd), lambda i, j, kv: (j, 0)),
            pl.BlockSpec((tq, hd), lambda i, j, kv: (j, 0)),
            pl.BlockSpec((1, hd), lambda i, j, kv: (0, 0)),
            pl.BlockSpec((1, hd), lambda i, j, kv: (0, 0)),
        ],
        out_specs=[
            pl.BlockSpec((1, tq, qw), lambda i, j, kv: (i, j, kv)),
            pl.BlockSpec((1, tq, hd), lambda i, j, kv: (i, j, kv)),
            pl.BlockSpec((1, tq, hd), lambda i, j, kv: (i, j, kv)),
        ],
        out_shape=[
            jax.ShapeDtypeStruct((b, s, ATTN_KV_HEADS * qw), BF16),
            jax.ShapeDtypeStruct((b, s, ATTN_KV_HEADS * hd), BF16),
            jax.ShapeDtypeStruct((b, s, ATTN_KV_HEADS * hd), BF16),
        ],
        compiler_params=_cparams(("parallel", "parallel", "parallel")),
        name="qk_prep",
    )(p3, p3, p3, cos, sin, q_gain.reshape(1, hd), k_gain.reshape(1, hd))


def _attn_kernel(q_ref, k_ref, v_ref, o_ref, *, lc, tq):
    hd = ATTN_HEAD_DIM
    c = (hd ** -0.5) * math.log2(math.e)

    def run(kk, vv):
        def scores(h):
            return lax.dot_general(q_ref[0, :, h * hd:(h + 1) * hd], kk, (((1,), (1,)), ((), ())),
                                   preferred_element_type=F32)

        s_next = scores(0)
        for h in range(ATTN_Q_PER_KV):
            s = s_next
            if h + 1 < ATTN_Q_PER_KV:
                s_next = scores(h + 1)
            m = jnp.max(s, axis=-1, keepdims=True)
            p = jnp.exp2((s - m) * c)
            l = jnp.sum(p, axis=-1, keepdims=True)
            o = jnp.dot(p.astype(BF16), vv, preferred_element_type=F32)
            o_ref[0, :, h * hd:(h + 1) * hd] = (o / l).astype(o_ref.dtype)

    is_ctx_tile = pl.program_id(2) * tq < lc

    @pl.when(jnp.logical_not(is_ctx_tile))
    def _():
        run(k_ref[0], v_ref[0])

    @pl.when(is_ctx_tile)
    def _():
        run(k_ref[0, :lc, :], v_ref[0, :lc, :])


def _attention(qh, kh, vh, lc):
    b, s, _ = qh.shape
    hd = ATTN_HEAD_DIM
    qw = ATTN_Q_PER_KV * hd
    tq = _pick(lc, (256, 128))
    assert s % tq == 0
    return pl.pallas_call(
        functools.partial(_attn_kernel, lc=lc, tq=tq),
        grid=(b, ATTN_KV_HEADS, s // tq),
        in_specs=[
            pl.BlockSpec((1, tq, qw), lambda i, kv, j: (i, j, kv)),
            pl.BlockSpec((1, s, hd), lambda i, kv, j: (i, 0, kv)),
            pl.BlockSpec((1, s, hd), lambda i, kv, j: (i, 0, kv)),
        ],
        out_specs=pl.BlockSpec((1, tq, qw), lambda i, kv, j: (i, j, kv)),
        out_shape=jax.ShapeDtypeStruct((b, s, ATTN_KV_HEADS * qw), BF16),
        compiler_params=_cparams(("parallel", "parallel", "parallel")),
        name="attention",
    )(qh, kh, vh)


def _split3(x):
    hi = x.astype(BF16)
    r = x - hi.astype(F32)
    mid = r.astype(BF16)
    lo = (r - mid.astype(F32)).astype(BF16)
    return hi, mid, lo


def _log_sigmoid(x):
    return jnp.minimum(x, 0.0) - jnp.log(1.0 + jnp.exp(-jnp.abs(x)))


def _mlstm_kernel(q_ref, k_ref, v_ref, gc_ref, gr_ref, bc_ref, br_ref, o_ref, c_scr, n_scr, m_scr):
    nh, dk, dv, cl = MLSTM_HEADS, MLSTM_QK_DIM, MLSTM_V_DIM, MLSTM_CHUNK
    nb = q_ref.shape[0]
    d = pl.program_id(0)
    c = pl.program_id(1)

    @pl.when(c == 0)
    def _():
        c_scr[...] = jnp.zeros_like(c_scr)
        n_scr[...] = jnp.zeros_like(n_scr)
        m_scr[...] = jnp.zeros_like(m_scr)

    row = lax.broadcasted_iota(jnp.int32, (cl, cl), 0)
    col = lax.broadcasted_iota(jnp.int32, (cl, cl), 1)
    sign = jnp.where(d == 0, 1, -1)
    mask = sign * (row - col) >= 0
    tm = jnp.where(mask, 1.0, 0.0).astype(BF16)
    tmt = jnp.where(sign * (col - row) >= 0, 1.0, 0.0).astype(BF16)

    def pick_c(a, k):
        return jnp.where(d == 0, a[:, k:k + 1], a[:, 2 * nh + k:2 * nh + k + 1])

    def pick_r(a, k):
        return jnp.where(d == 0, a[k:k + 1, :], a[2 * nh + k:2 * nh + k + 1, :])

    nt = (((1,), (1,)), ((), ()))
    tn = (((0,), (0,)), ((), ()))
    heads = [(bi, h) for bi in range(nb) for h in range(nh)]

    gcs = [gc_ref[bi] + bc_ref[...] for bi in range(nb)]
    grs = [gr_ref[bi, 0] + br_ref[...] for bi in range(nb)]
    lf_cs = [_log_sigmoid(g) for g in gcs]
    lf_rs = [_log_sigmoid(g) for g in grs]
    cum_cs = [sum(jnp.dot(tm, part, preferred_element_type=F32) for part in _split3(x)) for x in lf_cs]
    cum_rs = [sum(jnp.dot(part, tmt, preferred_element_type=F32) for part in _split3(x)) for x in lf_rs]
    totals = [jnp.sum(x, axis=0, keepdims=True) for x in lf_cs]

    qs = [q_ref[bi][:, h * dk:(h + 1) * dk] * (dk ** -0.5) for bi, h in heads]
    ks = [k_ref[bi][:, h * dk:(h + 1) * dk] for bi, h in heads]
    vs = [v_ref[bi][:, h * dv:(h + 1) * dv] for bi, h in heads]
    qbs = [x.astype(BF16) for x in qs]
    kbs = [x.astype(BF16) for x in ks]
    m_sts = [m_scr[i][:, 0:1] for i in range(len(heads))]
    c_sts = [c_scr[i] for i in range(len(heads))]
    n_sts = [n_scr[i] for i in range(len(heads))]

    s_raw = [lax.dot_general(qbs[i], kbs[i], nt, preferred_element_type=F32) for i in range(len(heads))]
    cqs = [lax.dot_general(qbs[i], c_sts[i].astype(BF16), nt, preferred_element_type=F32)
           for i in range(len(heads))]

    cf_c = [pick_c(cum_cs[bi], nh + h) for bi, h in heads]
    cf_r = [pick_r(cum_rs[bi], nh + h) for bi, h in heads]
    li_c = [pick_c(gcs[bi], h) for bi, h in heads]
    li_r = [pick_r(grs[bi], h) for bi, h in heads]
    tot = [pick_c(totals[bi], nh + h) for bi, h in heads]

    log_d = [jnp.where(mask, cf_c[i] - cf_r[i] + li_r[i], -jnp.inf) for i in range(len(heads))]
    log_inter = [cf_c[i] + m_sts[i] for i in range(len(heads))]
    m_row = [jnp.maximum(log_inter[i], jnp.max(log_d[i], axis=-1, keepdims=True)) for i in range(len(heads))]
    s = [s_raw[i] * jnp.exp(log_d[i] - m_row[i]) for i in range(len(heads))]
    w_inter = [jnp.exp(log_inter[i] - m_row[i]) for i in range(len(heads))]
    sv = [jnp.dot(s[i].astype(BF16), vs[i].astype(BF16), preferred_element_type=F32) for i in range(len(heads))]
    den = [jnp.sum(s[i], axis=-1, keepdims=True)
           + w_inter[i] * jnp.sum(qs[i] * n_sts[i], axis=-1, keepdims=True) for i in range(len(heads))]
    for i, (bi, h) in enumerate(heads):
        num = sv[i] + w_inter[i] * cqs[i]
        o_ref[0, bi, :, h * dv:(h + 1) * dv] = num / jnp.maximum(jnp.abs(den[i]), jnp.exp(-m_row[i]))

    log_w = [tot[i] - cf_c[i] + li_c[i] for i in range(len(heads))]
    m_new = [jnp.maximum(tot[i] + m_sts[i], jnp.max(log_w[i], axis=0, keepdims=True)) for i in range(len(heads))]
    decay = [jnp.exp(tot[i] + m_sts[i] - m_new[i]) for i in range(len(heads))]
    w = [jnp.exp(log_w[i] - m_new[i]) for i in range(len(heads))]
    kv = [lax.dot_general((w[i] * vs[i]).astype(BF16), kbs[i], tn, preferred_element_type=F32)
          for i in range(len(heads))]
    for i in range(len(heads)):
        c_scr[i] = decay[i] * c_sts[i] + kv[i]
        n_scr[i] = decay[i] * n_sts[i] + jnp.sum(w[i] * ks[i], axis=0, keepdims=True)
        m_scr[i] = jnp.broadcast_to(m_new[i], (1, LANES))


def _mlstm(p3, gate_bias, lc, col_q, col_k, col_v, col_g):
    b, s, _ = p3.shape
    nh, dk, dv, cl = MLSTM_HEADS, MLSTM_QK_DIM, MLSTM_V_DIM, MLSTM_CHUNK
    nc = s // cl
    nctx = lc // cl
    ng = 4 * nh
    g_rows = jnp.transpose(p3[:, :, col_g:col_g + ng].reshape(b, nc, cl, ng), (0, 1, 3, 2))
    bias = gate_bias.astype(F32).reshape(ng)

    def chunk(d, c):
        bwd = jnp.where(c < nctx, nctx - 1 - c, nc - 1 - (c - nctx))
        return jnp.where(d == 0, c, bwd)

    return pl.pallas_call(
        _mlstm_kernel,
        grid=(2, nc),
        in_specs=[
            pl.BlockSpec((b, cl, nh * dk), lambda d, c: (0, chunk(d, c), col_q // (nh * dk))),
            pl.BlockSpec((b, cl, nh * dk), lambda d, c: (0, chunk(d, c), col_k // (nh * dk))),
            pl.BlockSpec((b, cl, nh * dv), lambda d, c: (0, chunk(d, c), col_v // (nh * dv))),
            pl.BlockSpec((b, cl, LANES), lambda d, c: (0, chunk(d, c), col_g // LANES)),
            pl.BlockSpec((b, 1, ng, cl), lambda d, c: (0, chunk(d, c), 0, 0)),
            pl.BlockSpec((1, LANES), lambda d, c: (0, 0)),
            pl.BlockSpec((ng, 1), lambda d, c: (0, 0)),
        ],
        out_specs=pl.BlockSpec((1, b, cl, nh * dv), lambda d, c: (d, 0, chunk(d, c), 0)),
        out_shape=jax.ShapeDtypeStruct((2, b, s, nh * dv), F32),
        scratch_shapes=[pltpu.VMEM((b * nh, dv, dk), F32), pltpu.VMEM((b * nh, 1, dk), F32),
                        pltpu.VMEM((b * nh, 1, LANES), F32)],
        compiler_params=_cparams(("arbitrary", "arbitrary")),
        name="mlstm",
    )(p3, p3, p3, p3, g_rows, jnp.pad(bias, (0, LANES - ng)).reshape(1, LANES), bias.reshape(ng, 1))


def _gelu_tanh(x):
    return 0.5 * x * (1.0 + jnp.tanh(math.sqrt(2.0 / math.pi) * (x + 0.044715 * (x * x * x))))


def _merge_kernel(ys_ref, att_ref, hf_ref, hb_ref, og_ref, gl_ref, wglu_ref, bglu_ref, ng_ref,
                  wbs_ref, wba_ref, wbm_ref, y_ref):
    d = y_ref.shape[-1]
    z = _gelu_tanh(ys_ref[...])
    s5 = z * jax.nn.sigmoid(jnp.dot(z.astype(BF16), wglu_ref[...], preferred_element_type=F32) + bglu_ref[...])
    hsum = hf_ref[0] + hb_ref[0]
    dv = MLSTM_V_DIM
    parts = []
    for h in range(MLSTM_HEADS):
        blk = hsum[:, h * dv:(h + 1) * dv]
        parts.append(blk * lax.rsqrt(jnp.mean(blk * blk, axis=-1, keepdims=True) + EPS))
    ml = jnp.concatenate(parts, axis=-1) * ng_ref[...] * jax.nn.sigmoid(og_ref[...])
    y = gl_ref[:, 0:d].astype(F32) * jnp.dot(s5.astype(BF16), wbs_ref[...], preferred_element_type=F32)
    y = y + gl_ref[:, d:2 * d].astype(F32) * jnp.dot(att_ref[...], wba_ref[...], preferred_element_type=F32)
    y = y + gl_ref[:, 2 * d:3 * d].astype(F32) * jnp.dot(ml.astype(BF16), wbm_ref[...], preferred_element_type=F32)
    y_ref[...] = y.astype(y_ref.dtype)


def _merge(ys, att, hfb, p_main, gates, wglu, bglu, ng, wbs, wba, wbm, col_o):
    t, ws = ys.shape
    wa = att.shape[1]
    wm = hfb.shape[-1]
    d = wbs.shape[1]
    tm = _pick(t, (256, 128))
    full = lambda shape: pl.BlockSpec(shape, lambda i: (0,) * len(shape))
    return pl.pallas_call(
        _merge_kernel,
        grid=(t // tm,),
        in_specs=[
            pl.BlockSpec((tm, ws), lambda i: (i, 0)),
            pl.BlockSpec((tm, wa), lambda i: (i, 0)),
            pl.BlockSpec((1, tm, wm), lambda i: (0, i, 0)),
            pl.BlockSpec((1, tm, wm), lambda i: (1, i, 0)),
            pl.BlockSpec((tm, wm), lambda i: (i, col_o // wm)),
            pl.BlockSpec((tm, 3 * d), lambda i: (i, 0)),
            full((ws, ws)), full((1, ws)), full((1, wm)),
            full((ws, d)), full((wa, d)), full((wm, d)),
        ],
        out_specs=pl.BlockSpec((tm, d), lambda i: (i, 0)),
        out_shape=jax.ShapeDtypeStruct((t, d), BF16),
        compiler_params=_cparams(("parallel",)),
        name="branch_merge",
    )(ys, att, hfb, hfb, p_main, gates, wglu, bglu.reshape(1, ws), ng.reshape(1, wm), wbs, wba, wbm)


def _out_proj_router_kernel(y_ref, w_ref, x_ref, mod_ref, g_ref, wr_ref, br_ref,
                            o_ref, h_ref, idx_ref, tw_ref, *, lc, tm):
    is_ctx = _is_ctx_rows(tm, lc)
    out = jnp.dot(y_ref[0], w_ref[...], preferred_element_type=F32)
    x_mid = x_ref[0] + _row_select(mod_ref, 2, is_ctx) * out
    o_ref[0] = x_mid
    _norm_and_route(x_mid, g_ref, mod_ref, wr_ref, br_ref, h_ref, idx_ref, tw_ref, is_ctx)


def _out_proj_router(y3, w_out, xa, mod, g, w_router, b_router, lc):
    b, s, d = xa.shape
    tm = _pick(s, (272, 256, 128))
    ne = w_router.shape[1]
    row = lambda i, j: (i, j, 0)
    return pl.pallas_call(
        functools.partial(_out_proj_router_kernel, lc=lc, tm=tm),
        grid=(b, s // tm),
        in_specs=[
            pl.BlockSpec((1, tm, d), row),
            pl.BlockSpec((d, d), lambda i, j: (0, 0)),
            pl.BlockSpec((1, tm, d), row),
            pl.BlockSpec((1, 2, 6, d), lambda i, j: (i, 0, 0, 0)),
            pl.BlockSpec((1, d), lambda i, j: (0, 0)),
            pl.BlockSpec((d, ne), lambda i, j: (0, 0)),
            pl.BlockSpec((1, ne), lambda i, j: (0, 0)),
        ],
        out_specs=[
            pl.BlockSpec((1, tm, d), row),
            pl.BlockSpec((1, tm, d // LANES, LANES), lambda i, j: (i, j, 0, 0)),
            pl.BlockSpec((1, tm, TOP_K), row),
            pl.BlockSpec((1, tm, TOP_K), row),
        ],
        out_shape=[
            jax.ShapeDtypeStruct((b, s, d), F32),
            jax.ShapeDtypeStruct((b, s, d // LANES, LANES), F32),
            jax.ShapeDtypeStruct((b, s, TOP_K), jnp.int32),
            jax.ShapeDtypeStruct((b, s, TOP_K), F32),
        ],
        compiler_params=_cparams(("parallel", "parallel")),
        name="out_proj_router",
    )(y3, w_out, xa, mod, g.reshape(1, d), w_router, b_router.reshape(1, ne))


def _pair_split_matrix():
    half = MXU_COLS // 2
    n = jnp.arange(MXU_COLS)
    src = jnp.where(n < half, 2 * n, 2 * (n - half) + 1)
    return (jnp.arange(MXU_COLS)[:, None] == src[None, :]).astype(BF16)


def _pair_split_index(width):
    half = MXU_COLS // 2
    n = jnp.arange(width)
    blk, r = n // MXU_COLS, n % MXU_COLS
    return blk * MXU_COLS + jnp.where(r < half, 2 * r, 2 * (r - half) + 1)


def _split_pairs_kernel(w_ref, p_ref, o_ref):
    for cb in range(w_ref.shape[-1] // MXU_COLS):
        sl = slice(cb * MXU_COLS, (cb + 1) * MXU_COLS)
        o_ref[0, :, sl] = jnp.dot(w_ref[0, :, sl].astype(BF16), p_ref[...],
                                  preferred_element_type=F32).astype(o_ref.dtype)


def _split_pairs(w):
    ne, d, n = w.shape
    tc = _pick(n, (2 * MXU_COLS, MXU_COLS))
    return pl.pallas_call(
        _split_pairs_kernel,
        grid=(ne, n // tc),
        in_specs=[
            pl.BlockSpec((1, d, tc), lambda e, i: (e, 0, i)),
            pl.BlockSpec((MXU_COLS, MXU_COLS), lambda e, i: (0, 0)),
        ],
        out_specs=pl.BlockSpec((1, d, tc), lambda e, i: (e, 0, i)),
        out_shape=jax.ShapeDtypeStruct((ne, d, n), BF16),
        compiler_params=_cparams(("parallel", "parallel")),
        name="split_gate_linear",
    )(w, _pair_split_matrix())


def _expert_kernel(te_ref, nu_ref, tok_ref, tok_next_ref, h_hbm, *refs):
    npart = EXPERT_WEIGHT_PARTS
    wgl_refs, bgl_ref = refs[0:npart], refs[npart]
    wd_refs, bd_ref = refs[npart + 1:2 * npart + 1], refs[2 * npart + 1]
    o_ref, wd_scr, xbuf, sem = refs[2 * npart + 2:2 * npart + 6]
    i = pl.program_id(0)
    half = MXU_COLS // 2
    tm = o_ref.shape[0]
    nseg = h_hbm.shape[1]
    n_used = nu_ref[0]
    used = i < n_used
    slot = lax.rem(i, 2)
    new_expert = jnp.logical_or(i == 0, te_ref[i] != te_ref[jnp.maximum(i - 1, 0)])
    wc = wd_refs[0].shape[2]

    def start_rows(idx_ref, dst_slot):
        for r in range(tm):
            dst = xbuf.at[pl.ds((dst_slot * tm + r) * nseg, nseg)]
            pltpu.make_async_copy(h_hbm.at[idx_ref[0, 0, r]], dst, sem.at[dst_slot]).start()

    def wait_rows(dst_slot):
        dst = xbuf.at[pl.ds(dst_slot * tm * nseg, tm * nseg)]
        pltpu.make_async_copy(dst, dst, sem.at[dst_slot]).wait()

    @pl.when(i == 0)
    def _():
        start_rows(tok_ref, 0)

    @pl.when(jnp.logical_and(used, new_expert))
    def _():
        for part in range(npart):
            wd_scr[:, part * wc:(part + 1) * wc] = wd_refs[part][0].astype(BF16)

    @pl.when(used)
    def _():
        start_rows(tok_next_ref, 1 - slot)
        wait_rows(slot)

    @pl.when(used)
    def _():
        base = slot * (tm * nseg)
        x = jnp.concatenate([xbuf[pl.ds(base + j, tm, stride=nseg), :] for j in range(nseg)],
                            axis=1).astype(BF16)
        gu = jnp.concatenate([jnp.dot(x, w[0], preferred_element_type=F32) for w in wgl_refs], axis=1)
        gu = gu + bgl_ref[0]
        nblk = gu.shape[1] // MXU_COLS
        gate = jnp.concatenate([gu[:, cb * MXU_COLS:cb * MXU_COLS + half] for cb in range(nblk)], axis=1)
        lin = jnp.concatenate([gu[:, cb * MXU_COLS + half:(cb + 1) * MXU_COLS] for cb in range(nblk)], axis=1)
        gate = jnp.minimum(gate, SWIGLU_LIMIT)
        lin = jnp.clip(lin, -SWIGLU_LIMIT, SWIGLU_LIMIT)
        act = gate * jax.nn.sigmoid(SWIGLU_ALPHA * gate) * (lin + 1.0)
        y = jnp.dot(act.astype(BF16), wd_scr[...], preferred_element_type=F32) + bd_ref[0]
        o_ref[...] = y.astype(o_ref.dtype)

    @pl.when(i == n_used - 1)
    def _():
        wait_rows(1 - slot)

    @pl.when(jnp.logical_not(used))
    def _():
        o_ref[...] = jnp.zeros_like(o_ref)


def _experts(h_rows, slot_tok, tile_expert, n_used, wgl, bgl, wd, bd):
    _, nseg, lanes = h_rows.shape
    d = nseg * lanes
    n_tiles, tm = slot_tok.shape
    _, _, ff2 = wgl.shape
    ff = ff2 // 2
    npart = EXPERT_WEIGHT_PARTS
    col_part = lambda part: (lambda i, te, nu: (te[i], 0, part))
    grid_spec = pltpu.PrefetchScalarGridSpec(
        num_scalar_prefetch=2,
        grid=(n_tiles,),
        in_specs=(
            [pl.BlockSpec((1, 1, tm), lambda i, te, nu: (i, 0, 0), memory_space=pltpu.SMEM),
             pl.BlockSpec((1, 1, tm), lambda i, te, nu: (jnp.minimum(i + 1, n_tiles - 1), 0, 0),
                          memory_space=pltpu.SMEM),
             pl.BlockSpec(memory_space=pl.ANY)]
            + [pl.BlockSpec((1, d, ff2 // npart), col_part(part)) for part in range(npart)]
            + [pl.BlockSpec((1, 1, ff2), lambda i, te, nu: (te[i], 0, 0))]
            + [pl.BlockSpec((1, ff, d // npart), col_part(part)) for part in range(npart)]
            + [pl.BlockSpec((1, 1, d), lambda i, te, nu: (te[i], 0, 0))]),
        out_specs=pl.BlockSpec((tm, d), lambda i, te, nu: (i, 0)),
        scratch_shapes=[pltpu.VMEM((ff, d), BF16),
                        pltpu.VMEM((2 * tm * nseg, lanes), F32),
                        pltpu.SemaphoreType.DMA((2,))],
    )
    tok3 = slot_tok[:, None, :]
    return pl.pallas_call(
        _expert_kernel,
        grid_spec=grid_spec,
        out_shape=jax.ShapeDtypeStruct((n_tiles * tm, d), BF16),
        compiler_params=_cparams(("arbitrary",)),
        name="experts",
    )(tile_expert, n_used, tok3, tok3, h_rows, *([wgl] * npart), bgl, *([wd] * npart), bd)


def _route(top_idx, n_slots):
    tm = EXPERT_TILE
    e = top_idx.reshape(-1)
    n = e.shape[0]
    onehot = (e[:, None] == jnp.arange(N_EXPERTS, dtype=jnp.int32)[None, :]).astype(jnp.int32)
    counts = jnp.sum(onehot, axis=0)
    rank = jnp.take_along_axis(jnp.cumsum(onehot, axis=0) - onehot, e[:, None], axis=1)[:, 0]
    padded = ((counts + tm - 1) // tm) * tm
    ends = jnp.cumsum(padded)
    dest = (ends - padded)[e] + rank
    n_used = (ends[-1] // tm).astype(jnp.int32)
    tile_start = jnp.arange(n_slots // tm, dtype=jnp.int32) * tm
    tile_expert = jnp.sum((ends[None, :] <= tile_start[:, None]).astype(jnp.int32), axis=1)
    last_used = jnp.sum((ends <= (n_used - 1) * tm).astype(jnp.int32))
    tile_expert = jnp.minimum(jnp.where(tile_start < ends[-1], tile_expert, last_used), N_EXPERTS - 1)
    src = jnp.zeros((n_slots,), jnp.int32).at[dest].set(jnp.arange(n, dtype=jnp.int32) // TOP_K)
    return src, dest, tile_expert.astype(jnp.int32), n_used.reshape(1)


def _combine_kernel(y0_ref, y1_ref, y2_ref, y3_ref, tw_ref, x_ref, mod_ref, fg_ref, o_ref, *, lc, tm, final):
    tw = tw_ref[0]
    moe = tw[:, 0:1] * y0_ref[0, 0].astype(F32)
    for k, y_ref in enumerate((y1_ref, y2_ref, y3_ref), start=1):
        moe = moe + tw[:, k:k + 1] * y_ref[0, 0].astype(F32)
    if final:
        xn = x_ref[0] + mod_ref[0, 1, 5:6, :] * moe
        xn = xn * lax.rsqrt(jnp.mean(xn * xn, axis=-1, keepdims=True) + EPS) * fg_ref[...]
    else:
        xn = x_ref[0] + _row_select(mod_ref, 5, _is_ctx_rows(tm, lc)) * moe
    o_ref[0] = xn


def _combine(yg, top_w, xa, mod, final_g, lc, final):
    b, s, d = xa.shape
    assert TOP_K == 4
    if final:
        tm = _pick(lc, (256, 128))
        assert (s - lc) % tm == 0
        off, rows = lc // tm, s - lc
    else:
        tm = _pick(s, (544, 256, 128))
        off, rows = 0, s
    y_specs = [pl.BlockSpec((1, 1, tm, d), functools.partial(lambda i, j, k: (k, i, j + off, 0), k=k))
               for k in range(TOP_K)]
    return pl.pallas_call(
        functools.partial(_combine_kernel, lc=lc, tm=tm, final=final),
        grid=(b, rows // tm),
        in_specs=y_specs + [
            pl.BlockSpec((1, tm, TOP_K), lambda i, j: (i, j + off, 0)),
            pl.BlockSpec((1, tm, d), lambda i, j: (i, j + off, 0)),
            pl.BlockSpec((1, 2, 6, d), lambda i, j: (i, 0, 0, 0)),
            pl.BlockSpec((1, d), lambda i, j: (0, 0)),
        ],
        out_specs=pl.BlockSpec((1, tm, d), lambda i, j: (i, j, 0)),
        out_shape=jax.ShapeDtypeStruct((b, rows, d), F32),
        compiler_params=_cparams(("parallel", "parallel")),
        name="moe_combine",
    )(yg, yg, yg, yg, top_w, xa, mod, final_g.reshape(1, d))


def kernel(x, c, ctx, c_ctx, w_ada, b_ada, norm_g, w_in, ssm_a_re, ssm_a_im, ssm_log_dt, ssm_b_re, ssm_b_im, ssm_c_re, ssm_c_im, ssm_d, ssm_w_glu, ssm_b_glu, attn_q_gain, attn_k_gain, mlstm_gate_bias, mlstm_norm_g, w_branch_ssm, w_branch_attn, w_branch_mlstm, w_out, w_router, b_router, w_gate_up, b_gate_up, w_down, b_down, final_g):
    b, lx, d = x.shape
    lc = ctx.shape[1]
    s = lc + lx
    t = b * s
    depth = w_in.shape[0]
    assert b + 1 <= 8 and lc % MLSTM_CHUNK == 0 and lx % MLSTM_CHUNK == 0

    ssm_w = ssm_d.shape[1]
    attn_w = w_branch_attn.shape[1]
    kv_w = ATTN_KV_HEADS * ATTN_HEAD_DIM
    mqk_w = MLSTM_HEADS * MLSTM_QK_DIM
    mv_w = MLSTM_HEADS * MLSTM_V_DIM
    n_gates = 4 * MLSTM_HEADS
    col_q = ssm_w
    col_k = col_q + attn_w
    col_v = col_k + kv_w
    col_mq = col_v + kv_w
    col_mk = col_mq + mqk_w
    col_mv = col_mk + mqk_w
    col_mo = col_mv + mv_w
    col_mg = col_mo + mv_w
    col_bg = col_mg + n_gates
    main_pad = (-(col_mg + n_gates)) % 256

    xa = jnp.concatenate([ctx, x], axis=1)
    cond8 = jnp.zeros((8, d), F32).at[0].set(c_ctx).at[1:1 + b].set(c)
    mods = _ada_modulation(cond8, w_ada, b_ada).reshape(depth, 8, 6, d)
    cos, sin = _rope_tables(lc, lx)
    ne, ff2 = w_gate_up.shape[1], w_gate_up.shape[3]
    wgl_all = _split_pairs(w_gate_up.reshape(depth * ne, d, ff2))
    bgl_all = b_gate_up.reshape(depth * ne, ff2)[:, _pair_split_index(ff2)][:, None, :]
    wd_all = w_down.reshape(depth * ne, ff2 // 2, d)
    bd_all = b_down.reshape(depth * ne, 1, d)

    for l in range(depth):
        last = l == depth - 1
        mod = jnp.stack([jnp.broadcast_to(mods[l, 0], (b, 6, d)), mods[l, 1:1 + b]], axis=1)

        h = _norm_mod(xa, norm_g[l, 0], mod, lc, 0).reshape(t, d)
        w_main = jnp.concatenate(
            [w_in[l, :, :col_bg], jnp.zeros((d, main_pad), F32)], axis=1).astype(BF16)
        p_main = _gemm(h, w_main, F32)
        gates = _gemm(h, w_in[l, :, col_bg:].astype(BF16), BF16, act="sigmoid")
        p3 = p_main.reshape(b, s, -1)

        ops = _s5_operators(ssm_a_re[l], ssm_a_im[l], ssm_log_dt[l], ssm_b_re[l], ssm_b_im[l],
                            ssm_c_re[l], ssm_c_im[l], ssm_d[l])
        ys = _s5_mix(p3, ops, ssm_w, lc)

        qh, kh, vh = _qk_prep(p3, cos, sin, attn_q_gain[l], attn_k_gain[l], col_q, col_k, col_v)
        att = _attention(qh, kh, vh, lc)

        hfb = _mlstm(p3, mlstm_gate_bias[l], lc, col_mq, col_mk, col_mv, col_mg)

        y = _merge(ys.reshape(t, ssm_w), att.reshape(t, attn_w), hfb.reshape(2, t, mv_w), p_main, gates,
                   ssm_w_glu[l].astype(BF16), ssm_b_glu[l], mlstm_norm_g[l],
                   w_branch_ssm[l].astype(BF16), w_branch_attn[l].astype(BF16),
                   w_branch_mlstm[l].astype(BF16), col_mo)
        xa, h2, top_idx, top_w = _out_proj_router(y.reshape(b, s, d), w_out[l].astype(BF16), xa, mod,
                                                  norm_g[l, 1], w_router[l], b_router[l], lc)
        n_slots = t * TOP_K + N_EXPERTS * EXPERT_TILE
        src, dest, tile_expert, n_used = _route(top_idx, n_slots)
        ysorted = _experts(h2.reshape(t, d // LANES, LANES), src.reshape(-1, EXPERT_TILE),
                           tile_expert + l * ne, n_used, wgl_all, bgl_all, wd_all, bd_all)
        yg = ysorted.at[dest.reshape(t, TOP_K).T].get(mode="promise_in_bounds").reshape(TOP_K, b, s, d)
        xa = _combine(yg, top_w, xa, mod, final_g, lc, last)

    return xa
```

```python
import functools
import math

import jax
import jax.numpy as jnp
from jax import lax
from jax.experimental import pallas as pl
from jax.experimental.pallas import tpu as pltpu

F32 = jnp.float32
BF16 = jnp.bfloat16

GRID_W = 64
EPS = 1e-6
SSM_GROUP = 16
SSM_STATE = 64
SSM_CHUNK = 8
SSM_PACK = 8
ATTN_HEAD_DIM = 128
ATTN_KV_HEADS = 2
ATTN_Q_PER_KV = 4
ROPE_THETA = 10000.0
MLSTM_HEADS = 4
MLSTM_QK_DIM = 64
MLSTM_V_DIM = 128
MLSTM_CHUNK = 64
N_EXPERTS = 32
TOP_K = 4
SWIGLU_LIMIT = 7.0
SWIGLU_ALPHA = 1.702

LANES = 128
MXU_COLS = 256
VMEM_LIMIT_BYTES = 56 * 1024 * 1024
EXPERT_TILE = 256
EXPERT_WEIGHT_PARTS = 4

def _cparams(sem):
    return pltpu.CompilerParams(dimension_semantics=sem, vmem_limit_bytes=VMEM_LIMIT_BYTES)


def _pick(n, cands):
    for c in cands:
        if n % c == 0:
            return c
    raise ValueError(f"no tile for {n} in {cands}")


def _ada_kernel(c_ref, w_ref, b_ref, o_ref):
    c = c_ref[...]
    s = c * jax.nn.sigmoid(c)
    o_ref[0] = jnp.dot(s, w_ref[0], preferred_element_type=F32) + b_ref[0]


def _ada_modulation(cond8, w_ada, b_ada):
    nl, d, n = w_ada.shape
    tn = _pick(n, (1024, 512, 256, 128))
    return pl.pallas_call(
        _ada_kernel,
        grid=(nl, n // tn),
        in_specs=[
            pl.BlockSpec((8, d), lambda l, j: (0, 0)),
            pl.BlockSpec((1, d, tn), lambda l, j: (l, 0, j)),
            pl.BlockSpec((1, 1, tn), lambda l, j: (l, 0, j)),
        ],
        out_specs=pl.BlockSpec((1, 8, tn), lambda l, j: (l, 0, j)),
        out_shape=jax.ShapeDtypeStruct((nl, 8, n), F32),
        compiler_params=_cparams(("parallel", "parallel")),
        name="ada_modulation",
    )(cond8, w_ada, b_ada.reshape(nl, 1, n))


def _row_select(mod_ref, k, is_ctx):
    return jnp.where(is_ctx, mod_ref[0, 0, k:k + 1, :], mod_ref[0, 1, k:k + 1, :])


def _modulated_norm(x, g, mod_ref, k_shift, is_ctx):
    ms = jnp.mean(x * x, axis=-1, keepdims=True)
    y = x * lax.rsqrt(ms + EPS) * g
    return y * (1.0 + _row_select(mod_ref, k_shift + 1, is_ctx)) + _row_select(mod_ref, k_shift, is_ctx)


def _is_ctx_rows(tm, lc):
    pos = pl.program_id(1) * tm + lax.broadcasted_iota(jnp.int32, (tm, 1), 0)
    return pos < lc


def _norm_mod_kernel(x_ref, g_ref, mod_ref, o_ref, *, lc, tm, k_shift):
    h = _modulated_norm(x_ref[0], g_ref[...], mod_ref, k_shift, _is_ctx_rows(tm, lc))
    o_ref[0] = h.astype(o_ref.dtype)


def _norm_mod(xa, g, mod, lc, k_shift):
    b, s, d = xa.shape
    tm = _pick(s, (544, 256, 128))
    return pl.pallas_call(
        functools.partial(_norm_mod_kernel, lc=lc, tm=tm, k_shift=k_shift),
        grid=(b, s // tm),
        in_specs=[
            pl.BlockSpec((1, tm, d), lambda i, j: (i, j, 0)),
            pl.BlockSpec((1, d), lambda i, j: (0, 0)),
            pl.BlockSpec((1, 2, 6, d), lambda i, j: (i, 0, 0, 0)),
        ],
        out_specs=pl.BlockSpec((1, tm, d), lambda i, j: (i, j, 0)),
        out_shape=jax.ShapeDtypeStruct((b, s, d), BF16),
        compiler_params=_cparams(("parallel", "parallel")),
        name="norm_mod",
    )(xa, g.reshape(1, d), mod)


def _norm_and_route(x, g_ref, mod_ref, wr_ref, br_ref, h_ref, idx_ref, w_ref, is_ctx):
    h = _modulated_norm(x, g_ref[...], mod_ref, 3, is_ctx)
    for j in range(h.shape[1] // LANES):
        h_ref[0, :, j, :] = h[:, j * LANES:(j + 1) * LANES]
    logits = jnp.dot(h, wr_ref[...], preferred_element_type=F32,
                     precision=lax.Precision.HIGHEST) + br_ref[...]
    lane = lax.broadcasted_iota(jnp.int32, logits.shape, 1).astype(F32)
    vals, idxs = [], []
    cur = logits
    for _ in range(TOP_K):
        mx = jnp.max(cur, axis=-1, keepdims=True)
        ix = jnp.min(jnp.where(cur == mx, lane, float(N_EXPERTS)), axis=-1, keepdims=True)
        vals.append(mx)
        idxs.append(ix)
        cur = jnp.where(lane == ix, -jnp.inf, cur)
    k_iota = lax.broadcasted_iota(jnp.int32, (logits.shape[0], TOP_K), 1)
    top = jnp.zeros((logits.shape[0], TOP_K), F32)
    top_i = jnp.zeros((logits.shape[0], TOP_K), F32)
    for k in range(TOP_K):
        top = jnp.where(k_iota == k, vals[k], top)
        top_i = jnp.where(k_iota == k, idxs[k], top_i)
    e = jnp.exp(top - vals[0])
    w_ref[0] = e / jnp.sum(e, axis=-1, keepdims=True)
    idx_ref[0] = top_i.astype(jnp.int32)


def _gemm_kernel(a_ref, w_ref, o_ref, *, act):
    acc = jnp.dot(a_ref[...], w_ref[...], preferred_element_type=F32)
    if act == "sigmoid":
        acc = jax.nn.sigmoid(acc)
    o_ref[...] = acc.astype(o_ref.dtype)


def _gemm(a, w, out_dtype, act=None, tn_cands=(1920, 1024, 512, 256, 128)):
    t, k = a.shape
    n = w.shape[1]
    tm = _pick(t, (1024, 768, 512, 256, 128))
    tn = _pick(n, tn_cands)
    return pl.pallas_call(
        functools.partial(_gemm_kernel, act=act),
        grid=(t // tm, n // tn),
        in_specs=[
            pl.BlockSpec((tm, k), lambda i, j: (i, 0)),
            pl.BlockSpec((k, tn), lambda i, j: (0, j)),
        ],
        out_specs=pl.BlockSpec((tm, tn), lambda i, j: (i, j)),
        out_shape=jax.ShapeDtypeStruct((t, n), out_dtype),
        compiler_params=_cparams(("parallel", "parallel")),
        name="gemm_" + (act or "plain"),
    )(a, w)


def _s5_operators(a_re, a_im, log_dt, b_re, b_im, c_re, c_im, d_skip):
    tc = SSM_CHUNK
    hp = lax.Precision.HIGHEST
    gs = SSM_PACK
    nsg = a_re.shape[1] // gs
    eye = jnp.eye(gs, dtype=F32)
    k_idx = jnp.arange(tc + 1, dtype=F32)
    lag = jnp.arange(tc)[None, :] - jnp.arange(tc)[:, None]
    ms, bcs, ccs, a16s = [], [], [], []
    for dirn in (0, 1):
        lam = lax.complex(a_re[dirn].astype(F32), a_im[dirn].astype(F32))
        dt = jnp.exp(log_dt[dirn].astype(F32))[:, None]
        pw = jnp.exp((lam * dt)[None] * k_idx[:, None, None])
        a_bar = pw[1]
        b_bar = ((a_bar - 1.0) / lam)[..., None] * lax.complex(b_re[dirn].astype(F32), b_im[dirn].astype(F32))
        c_mat = lax.complex(c_re[dirn].astype(F32), c_im[dirn].astype(F32))
        k_lag = jnp.real(jnp.einsum("ghp,kgp,gpj->kghj", c_mat, pw[:tc], b_bar, precision=hp))
        if dirn == 0:
            dist, valid = lag, lag >= 0
            end_pow = (tc - 1) - jnp.arange(tc)
            out_pow = jnp.arange(tc) + 1
        else:
            dist, valid = -lag, lag <= 0
            end_pow = jnp.arange(tc)
            out_pow = tc - jnp.arange(tc)
        kt = jnp.transpose(k_lag, (0, 1, 3, 2)).reshape(tc, nsg, gs, SSM_GROUP, SSM_GROUP)
        bd = (kt[:, :, :, :, None, :] * eye[None, None, :, None, :, None]).reshape(tc, nsg, LANES, LANES)
        blocks = jnp.where(valid[:, :, None, None, None], bd[jnp.clip(dist, 0, tc - 1)], 0.0)
        ms.append(jnp.transpose(blocks, (2, 0, 3, 1, 4)).reshape(nsg, tc * LANES, tc * LANES))
        bc = pw[end_pow][:, :, :, None] * b_bar[None]
        bcs.append(jnp.transpose(bc, (1, 0, 3, 2)))
        cw = c_mat[None] * pw[out_pow][:, :, None, :]
        ccs.append(jnp.transpose(cw, (1, 3, 0, 2)))
        a16s.append(pw[tc])

    def pack_rows(x):
        x = x.reshape((nsg, gs) + x.shape[1:])
        x = jnp.transpose(x, (0, 2, 3, 1, 4))[:, :, None]
        x = x * eye[None, None, :, None, :, None]
        return x.reshape(nsg, tc * gs * SSM_GROUP, gs * SSM_STATE)

    def pack_cols(x):
        x = x.reshape((nsg, gs) + x.shape[1:])
        x = jnp.transpose(x, (0, 2, 3, 1, 4))[:, None]
        x = x * eye[None, :, None, None, :, None]
        return x.reshape(nsg, gs * SSM_STATE, tc * gs * SSM_GROUP)

    m = ms[0] + ms[1]
    bc = jnp.concatenate([pack_rows(jnp.real(bcs[0])), pack_rows(jnp.real(bcs[1])),
                          pack_rows(jnp.imag(bcs[0])), pack_rows(jnp.imag(bcs[1]))], axis=-1)
    cc = jnp.concatenate([pack_cols(jnp.real(ccs[0])), pack_cols(jnp.real(ccs[1])),
                          pack_cols(-jnp.imag(ccs[0])), pack_cols(-jnp.imag(ccs[1]))], axis=1)
    a_f, a_b = a16s[0].reshape(nsg, gs * SSM_STATE), a16s[1].reshape(nsg, gs * SSM_STATE)
    a_chunk = jnp.stack([jnp.concatenate([jnp.real(a_f), jnp.real(a_b)], axis=-1),
                         jnp.concatenate([jnp.imag(a_f), jnp.imag(a_b)], axis=-1)], axis=1)
    dtile = jnp.tile(d_skip.astype(F32).reshape(nsg, 1, gs * SSM_GROUP), (1, 1, tc))
    return m.astype(BF16), bc.astype(BF16), cc.astype(BF16), a_chunk.astype(F32), dtile


def _s5_kernel(u_ref, m_ref, bc_ref, cc_ref, a_ref, d_ref, y_ref, vre_scr, vim_scr, hf_re, hb_re, hf_im, hb_im,
               *, nchunk, nctx):
    tc = SSM_CHUNK
    w = hf_re.shape[1]
    u = jnp.concatenate([u_ref[0, pl.ds(i, nchunk, stride=tc), :] for i in range(tc)], axis=1)
    ub = u.astype(BF16)
    v = jnp.dot(ub, bc_ref[0], preferred_element_type=F32)
    vre_scr[...] = v[:, 0:2 * w]
    vim_scr[...] = v[:, 2 * w:4 * w]
    ar = a_ref[0, 0:1, :]
    ai = a_ref[0, 1:2, :]
    hre = jnp.zeros((1, 2 * w), F32)
    him = jnp.zeros((1, 2 * w), F32)
    for s in range(nchunk):
        cf = s
        cb = (nctx - 1 - s) if s < nctx else (nchunk - 1 - (s - nctx))
        hf_re[cf:cf + 1, :] = hre[:, 0:w]
        hb_re[cb:cb + 1, :] = hre[:, w:2 * w]
        hf_im[cf:cf + 1, :] = him[:, 0:w]
        hb_im[cb:cb + 1, :] = him[:, w:2 * w]
        vre = jnp.concatenate([vre_scr[cf:cf + 1, 0:w], vre_scr[cb:cb + 1, w:2 * w]], axis=1)
        vim = jnp.concatenate([vim_scr[cf:cf + 1, 0:w], vim_scr[cb:cb + 1, w:2 * w]], axis=1)
        hre, him = ar * hre - ai * him + vre, ar * him + ai * hre + vim
    hp = jnp.concatenate([hf_re[...], hb_re[...], hf_im[...], hb_im[...]], axis=1)
    y = jnp.dot(ub, m_ref[0], preferred_element_type=F32)
    y = y + jnp.dot(hp.astype(BF16), cc_ref[0], preferred_element_type=F32)
    y = y + u * d_ref[0]
    for i in range(tc):
        y_ref[0, pl.ds(i, nchunk, stride=tc), :] = y[:, i * LANES:(i + 1) * LANES]


def _s5_mix(p3, ops, width, lc):
    m, bc, cc, a_chunk, dtile = ops
    b, s, _ = p3.shape
    tc = SSM_CHUNK
    nchunk = s // tc
    cw = SSM_PACK * SSM_GROUP
    assert cw == LANES and width % cw == 0 and s % tc == 0 and lc % tc == 0
    k = tc * cw
    sw = SSM_PACK * SSM_STATE
    return pl.pallas_call(
        functools.partial(_s5_kernel, nchunk=nchunk, nctx=lc // tc),
        grid=(width // cw, b),
        in_specs=[
            pl.BlockSpec((1, s, cw), lambda g, i: (i, 0, g)),
            pl.BlockSpec((1, k, k), lambda g, i: (g, 0, 0)),
            pl.BlockSpec((1, k, 4 * sw), lambda g, i: (g, 0, 0)),
            pl.BlockSpec((1, 4 * sw, k), lambda g, i: (g, 0, 0)),
            pl.BlockSpec((1, 2, 2 * sw), lambda g, i: (g, 0, 0)),
            pl.BlockSpec((1, 1, k), lambda g, i: (g, 0, 0)),
        ],
        out_specs=pl.BlockSpec((1, s, cw), lambda g, i: (i, 0, g)),
        out_shape=jax.ShapeDtypeStruct((b, s, width), F32),
        scratch_shapes=[pltpu.VMEM((nchunk, 2 * sw), F32)] * 2 + [pltpu.VMEM((nchunk, sw), F32)] * 4,
        compiler_params=_cparams(("parallel", "parallel")),
        name="s5_mix",
    )(p3, m, bc, cc, a_chunk, dtile)


def _rope_tables(lc, lx):
    n = ATTN_HEAD_DIM // 4
    freq = ROPE_THETA ** (-jnp.arange(n, dtype=F32) / n)
    t = jnp.arange(lx, dtype=jnp.int32)
    rows = (t // GRID_W).astype(F32)[:, None] * freq[None, :]
    cols = (t % GRID_W).astype(F32)[:, None] * freq[None, :]
    cos = jnp.concatenate([jnp.cos(rows), jnp.cos(rows), jnp.cos(cols), jnp.cos(cols)], axis=-1)
    sin = jnp.concatenate([-jnp.sin(rows), jnp.sin(rows), -jnp.sin(cols), jnp.sin(cols)], axis=-1)
    cos = jnp.concatenate([jnp.ones((lc, ATTN_HEAD_DIM), F32), cos], axis=0)
    sin = jnp.concatenate([jnp.zeros((lc, ATTN_HEAD_DIM), F32), sin], axis=0)
    return cos, sin


def _qk_prep_kernel(q_ref, k_ref, v_ref, cos_ref, sin_ref, qg_ref, kg_ref, qo_ref, ko_ref, vo_ref):
    hd = ATTN_HEAD_DIM
    cos = cos_ref[...]
    sin = sin_ref[...]
    lane = lax.broadcasted_iota(jnp.int32, cos.shape, 1)
    first = (lane % (hd // 2)) < (hd // 4)

    def prep(x, g):
        y = x * lax.rsqrt(jnp.mean(x * x, axis=-1, keepdims=True) + EPS) * g
        partner = jnp.where(first, pltpu.roll(y, hd - hd // 4, 1), pltpu.roll(y, hd // 4, 1))
        return y * cos + partner * sin

    for h in range(ATTN_Q_PER_KV):
        sl = slice(h * hd, (h + 1) * hd)
        qo_ref[0, :, sl] = prep(q_ref[0, :, sl], qg_ref[...]).astype(qo_ref.dtype)
    ko_ref[0] = prep(k_ref[0], kg_ref[...]).astype(ko_ref.dtype)
    vo_ref[0] = v_ref[0].astype(vo_ref.dtype)


def _qk_prep(p3, cos, sin, q_gain, k_gain, col_q, col_k, col_v):
    b, s, _ = p3.shape
    hd = ATTN_HEAD_DIM
    qw = ATTN_Q_PER_KV * hd
    tq = _pick(s, (544, 256, 128))
    return pl.pallas_call(
        _qk_prep_kernel,
        grid=(b, s // tq, ATTN_KV_HEADS),
        in_specs=[
            pl.BlockSpec((1, tq, qw), lambda i, j, kv: (i, j, col_q // qw + kv)),
            pl.BlockSpec((1, tq, hd), lambda i, j, kv: (i, j, col_k // hd + kv)),
            pl.BlockSpec((1, tq, hd), lambda i, j, kv: (i, j, col_v // hd + kv)),
            pl.BlockSpec((tq, hd), lambda i, j, kv: (j, 0)),
            pl.BlockSpec((tq, hd), lambda i, j, kv: (j, 0)),
            pl.BlockSpec((1, hd), lambda i, j, kv: (0, 0)),
            pl.BlockSpec((1, hd), lambda i, j, kv: (0, 0)),
        ],
        out_specs=[
            pl.BlockSpec((1, tq, qw), lambda i, j, kv: (i, j, kv)),
            pl.BlockSpec((1, tq, hd), lambda i, j, kv: (i, j, kv)),
            pl.BlockSpec((1, tq, hd), lambda i, j, kv: (i, j, kv)),
        ],
        out_shape=[
            jax.ShapeDtypeStruct((b, s, ATTN_KV_HEADS * qw), BF16),
            jax.ShapeDtypeStruct((b, s, ATTN_KV_HEADS * hd), BF16),
            jax.ShapeDtypeStruct((b, s, ATTN_KV_HEADS * hd), BF16),
        ],
        compiler_params=_cparams(("parallel", "parallel", "parallel")),
        name="qk_prep",
    )(p3, p3, p3, cos, sin, q_gain.reshape(1, hd), k_gain.reshape(1, hd))


def _attn_kernel(q_ref, k_ref, v_ref, o_ref, *, lc, tq):
    hd = ATTN_HEAD_DIM
    c = (hd ** -0.5) * math.log2(math.e)

    def run(kk, vv):
        def scores(h):
            return lax.dot_general(q_ref[0, :, h * hd:(h + 1) * hd], kk, (((1,), (1,)), ((), ())),
                                   preferred_element_type=F32)

        s_next = scores(0)
        for h in range(ATTN_Q_PER_KV):
            s = s_next
            if h + 1 < ATTN_Q_PER_KV:
                s_next = scores(h + 1)
            m = jnp.max(s, axis=-1, keepdims=True)
            p = jnp.exp2((s - m) * c)
            l = jnp.sum(p, axis=-1, keepdims=True)
            o = jnp.dot(p.astype(BF16), vv, preferred_element_type=F32)
            o_ref[0, :, h * hd:(h + 1) * hd] = (o / l).astype(o_ref.dtype)

    is_ctx_tile = pl.program_id(2) * tq < lc

    @pl.when(jnp.logical_not(is_ctx_tile))
    def _():
        run(k_ref[0], v_ref[0])

    @pl.when(is_ctx_tile)
    def _():
        run(k_ref[0, :lc, :], v_ref[0, :lc, :])


def _attention(qh, kh, vh, lc):
    b, s, _ = qh.shape
    hd = ATTN_HEAD_DIM
    qw = ATTN_Q_PER_KV * hd
    tq = _pick(lc, (256, 128))
    assert s % tq == 0
    return pl.pallas_call(
        functools.partial(_attn_kernel, lc=lc, tq=tq),
        grid=(b, ATTN_KV_HEADS, s // tq),
        in_specs=[
            pl.BlockSpec((1, tq, qw), lambda i, kv, j: (i, j, kv)),
            pl.BlockSpec((1, s, hd), lambda i, kv, j: (i, 0, kv)),
            pl.BlockSpec((1, s, hd), lambda i, kv, j: (i, 0, kv)),
        ],
        out_specs=pl.BlockSpec((1, tq, qw), lambda i, kv, j: (i, j, kv)),
        out_shape=jax.ShapeDtypeStruct((b, s, ATTN_KV_HEADS * qw), BF16),
        compiler_params=_cparams(("parallel", "parallel", "parallel")),
        name="attention",
    )(qh, kh, vh)


def _split3(x):
    hi = x.astype(BF16)
    r = x - hi.astype(F32)
    mid = r.astype(BF16)
    lo = (r - mid.astype(F32)).astype(BF16)
    return hi, mid, lo


def _log_sigmoid(x):
    return jnp.minimum(x, 0.0) - jnp.log(1.0 + jnp.exp(-jnp.abs(x)))


def _mlstm_kernel(q_ref, k_ref, v_ref, gc_ref, gr_ref, bc_ref, br_ref, o_ref, c_scr, n_scr, m_scr):
    nh, dk, dv, cl = MLSTM_HEADS, MLSTM_QK_DIM, MLSTM_V_DIM, MLSTM_CHUNK
    nb = q_ref.shape[0]
    d = pl.program_id(0)
    c = pl.program_id(1)

    @pl.when(c == 0)
    def _():
        c_scr[...] = jnp.zeros_like(c_scr)
        n_scr[...] = jnp.zeros_like(n_scr)
        m_scr[...] = jnp.zeros_like(m_scr)

    row = lax.broadcasted_iota(jnp.int32, (cl, cl), 0)
    col = lax.broadcasted_iota(jnp.int32, (cl, cl), 1)
    sign = jnp.where(d == 0, 1, -1)
    mask = sign * (row - col) >= 0
    tm = jnp.where(mask, 1.0, 0.0).astype(BF16)
    tmt = jnp.where(sign * (col - row) >= 0, 1.0, 0.0).astype(BF16)

    def pick_c(a, k):
        return jnp.where(d == 0, a[:, k:k + 1], a[:, 2 * nh + k:2 * nh + k + 1])

    def pick_r(a, k):
        return jnp.where(d == 0, a[k:k + 1, :], a[2 * nh + k:2 * nh + k + 1, :])

    nt = (((1,), (1,)), ((), ()))
    tn = (((0,), (0,)), ((), ()))
    heads = [(bi, h) for bi in range(nb) for h in range(nh)]

    gcs = [gc_ref[bi] + bc_ref[...] for bi in range(nb)]
    grs = [gr_ref[bi, 0] + br_ref[...] for bi in range(nb)]
    lf_cs = [_log_sigmoid(g) for g in gcs]
    lf_rs = [_log_sigmoid(g) for g in grs]
    cum_cs = [sum(jnp.dot(tm, part, preferred_element_type=F32) for part in _split3(x)) for x in lf_cs]
    cum_rs = [sum(jnp.dot(part, tmt, preferred_element_type=F32) for part in _split3(x)) for x in lf_rs]
    totals = [jnp.sum(x, axis=0, keepdims=True) for x in lf_cs]

    qs = [q_ref[bi][:, h * dk:(h + 1) * dk] * (dk ** -0.5) for bi, h in heads]
    ks = [k_ref[bi][:, h * dk:(h + 1) * dk] for bi, h in heads]
    vs = [v_ref[bi][:, h * dv:(h + 1) * dv] for bi, h in heads]
    qbs = [x.astype(BF16) for x in qs]
    kbs = [x.astype(BF16) for x in ks]
    m_sts = [m_scr[i][:, 0:1] for i in range(len(heads))]
    c_sts = [c_scr[i] for i in range(len(heads))]
    n_sts = [n_scr[i] for i in range(len(heads))]

    s_raw = [lax.dot_general(qbs[i], kbs[i], nt, preferred_element_type=F32) for i in range(len(heads))]
    cqs = [lax.dot_general(qbs[i], c_sts[i].astype(BF16), nt, preferred_element_type=F32)
           for i in range(len(heads))]

    cf_c = [pick_c(cum_cs[bi], nh + h) for bi, h in heads]
    cf_r = [pick_r(cum_rs[bi], nh + h) for bi, h in heads]
    li_c = [pick_c(gcs[bi], h) for bi, h in heads]
    li_r = [pick_r(grs[bi], h) for bi, h in heads]
    tot = [pick_c(totals[bi], nh + h) for bi, h in heads]

    log_d = [jnp.where(mask, cf_c[i] - cf_r[i] + li_r[i], -jnp.inf) for i in range(len(heads))]
    log_inter = [cf_c[i] + m_sts[i] for i in range(len(heads))]
    m_row = [jnp.maximum(log_inter[i], jnp.max(log_d[i], axis=-1, keepdims=True)) for i in range(len(heads))]
    s = [s_raw[i] * jnp.exp(log_d[i] - m_row[i]) for i in range(len(heads))]
    w_inter = [jnp.exp(log_inter[i] - m_row[i]) for i in range(len(heads))]
    sv = [jnp.dot(s[i].astype(BF16), vs[i].astype(BF16), preferred_element_type=F32) for i in range(len(heads))]
    den = [jnp.sum(s[i], axis=-1, keepdims=True)
           + w_inter[i] * jnp.sum(qs[i] * n_sts[i], axis=-1, keepdims=True) for i in range(len(heads))]
    for i, (bi, h) in enumerate(heads):
        num = sv[i] + w_inter[i] * cqs[i]
        o_ref[0, bi, :, h * dv:(h + 1) * dv] = num / jnp.maximum(jnp.abs(den[i]), jnp.exp(-m_row[i]))

    log_w = [tot[i] - cf_c[i] + li_c[i] for i in range(len(heads))]
    m_new = [jnp.maximum(tot[i] + m_sts[i], jnp.max(log_w[i], axis=0, keepdims=True)) for i in range(len(heads))]
    decay = [jnp.exp(tot[i] + m_sts[i] - m_new[i]) for i in range(len(heads))]
    w = [jnp.exp(log_w[i] - m_new[i]) for i in range(len(heads))]
    kv = [lax.dot_general((w[i] * vs[i]).astype(BF16), kbs[i], tn, preferred_element_type=F32)
          for i in range(len(heads))]
    for i in range(len(heads)):
        c_scr[i] = decay[i] * c_sts[i] + kv[i]
        n_scr[i] = decay[i] * n_sts[i] + jnp.sum(w[i] * ks[i], axis=0, keepdims=True)
        m_scr[i] = jnp.broadcast_to(m_new[i], (1, LANES))


def _mlstm(p3, gate_bias, lc, col_q, col_k, col_v, col_g):
    b, s, _ = p3.shape
    nh, dk, dv, cl = MLSTM_HEADS, MLSTM_QK_DIM, MLSTM_V_DIM, MLSTM_CHUNK
    nc = s // cl
    nctx = lc // cl
    ng = 4 * nh
    g_rows = jnp.transpose(p3[:, :, col_g:col_g + ng].reshape(b, nc, cl, ng), (0, 1, 3, 2))
    bias = gate_bias.astype(F32).reshape(ng)

    def chunk(d, c):
        bwd = jnp.where(c < nctx, nctx - 1 - c, nc - 1 - (c - nctx))
        return jnp.where(d == 0, c, bwd)

    return pl.pallas_call(
        _mlstm_kernel,
        grid=(2, nc),
        in_specs=[
            pl.BlockSpec((b, cl, nh * dk), lambda d, c: (0, chunk(d, c), col_q // (nh * dk))),
            pl.BlockSpec((b, cl, nh * dk), lambda d, c: (0, chunk(d, c), col_k // (nh * dk))),
            pl.BlockSpec((b, cl, nh * dv), lambda d, c: (0, chunk(d, c), col_v // (nh * dv))),
            pl.BlockSpec((b, cl, LANES), lambda d, c: (0, chunk(d, c), col_g // LANES)),
            pl.BlockSpec((b, 1, ng, cl), lambda d, c: (0, chunk(d, c), 0, 0)),
            pl.BlockSpec((1, LANES), lambda d, c: (0, 0)),
            pl.BlockSpec((ng, 1), lambda d, c: (0, 0)),
        ],
        out_specs=pl.BlockSpec((1, b, cl, nh * dv), lambda d, c: (d, 0, chunk(d, c), 0)),
        out_shape=jax.ShapeDtypeStruct((2, b, s, nh * dv), F32),
        scratch_shapes=[pltpu.VMEM((b * nh, dv, dk), F32), pltpu.VMEM((b * nh, 1, dk), F32),
                        pltpu.VMEM((b * nh, 1, LANES), F32)],
        compiler_params=_cparams(("arbitrary", "arbitrary")),
        name="mlstm",
    )(p3, p3, p3, p3, g_rows, jnp.pad(bias, (0, LANES - ng)).reshape(1, LANES), bias.reshape(ng, 1))


def _gelu_tanh(x):
    return 0.5 * x * (1.0 + jnp.tanh(math.sqrt(2.0 / math.pi) * (x + 0.044715 * (x * x * x))))


def _merge_kernel(ys_ref, att_ref, hf_ref, hb_ref, og_ref, gl_ref, wglu_ref, bglu_ref, ng_ref,
                  wbs_ref, wba_ref, wbm_ref, y_ref):
    d = y_ref.shape[-1]
    z = _gelu_tanh(ys_ref[...])
    s5 = z * jax.nn.sigmoid(jnp.dot(z.astype(BF16), wglu_ref[...], preferred_element_type=F32) + bglu_ref[...])
    hsum = hf_ref[0] + hb_ref[0]
    dv = MLSTM_V_DIM
    parts = []
    for h in range(MLSTM_HEADS):
        blk = hsum[:, h * dv:(h + 1) * dv]
        parts.append(blk * lax.rsqrt(jnp.mean(blk * blk, axis=-1, keepdims=True) + EPS))
    ml = jnp.concatenate(parts, axis=-1) * ng_ref[...] * jax.nn.sigmoid(og_ref[...])
    y = gl_ref[:, 0:d].astype(F32) * jnp.dot(s5.astype(BF16), wbs_ref[...], preferred_element_type=F32)
    y = y + gl_ref[:, d:2 * d].astype(F32) * jnp.dot(att_ref[...], wba_ref[...], preferred_element_type=F32)
    y = y + gl_ref[:, 2 * d:3 * d].astype(F32) * jnp.dot(ml.astype(BF16), wbm_ref[...], preferred_element_type=F32)
    y_ref[...] = y.astype(y_ref.dtype)


def _merge(ys, att, hfb, p_main, gates, wglu, bglu, ng, wbs, wba, wbm, col_o):
    t, ws = ys.shape
    wa = att.shape[1]
    wm = hfb.shape[-1]
    d = wbs.shape[1]
    tm = _pick(t, (256, 128))
    full = lambda shape: pl.BlockSpec(shape, lambda i: (0,) * len(shape))
    return pl.pallas_call(
        _merge_kernel,
        grid=(t // tm,),
        in_specs=[
            pl.BlockSpec((tm, ws), lambda i: (i, 0)),
            pl.BlockSpec((tm, wa), lambda i: (i, 0)),
            pl.BlockSpec((1, tm, wm), lambda i: (0, i, 0)),
            pl.BlockSpec((1, tm, wm), lambda i: (1, i, 0)),
            pl.BlockSpec((tm, wm), lambda i: (i, col_o // wm)),
            pl.BlockSpec((tm, 3 * d), lambda i: (i, 0)),
            full((ws, ws)), full((1, ws)), full((1, wm)),
            full((ws, d)), full((wa, d)), full((wm, d)),
        ],
        out_specs=pl.BlockSpec((tm, d), lambda i: (i, 0)),
        out_shape=jax.ShapeDtypeStruct((t, d), BF16),
        compiler_params=_cparams(("parallel",)),
        name="branch_merge",
    )(ys, att, hfb, hfb, p_main, gates, wglu, bglu.reshape(1, ws), ng.reshape(1, wm), wbs, wba, wbm)


def _out_proj_kernel(y_ref, w_ref, x_ref, mod_ref, o_ref, *, lc, tm):
    out = jnp.dot(y_ref[0], w_ref[...], preferred_element_type=F32)
    gate = _row_select(mod_ref, 2, _is_ctx_rows(tm, lc))
    o_ref[0] = x_ref[0] + gate * out


def _out_proj(y3, w_out, xa, mod, lc):
    b, s, d = xa.shape
    tm = _pick(s, (544, 256, 128))
    return pl.pallas_call(
        functools.partial(_out_proj_kernel, lc=lc, tm=tm),
        grid=(b, s // tm),
        in_specs=[
            pl.BlockSpec((1, tm, d), lambda i, j: (i, j, 0)),
            pl.BlockSpec((d, d), lambda i, j: (0, 0)),
            pl.BlockSpec((1, tm, d), lambda i, j: (i, j, 0)),
            pl.BlockSpec((1, 2, 6, d), lambda i, j: (i, 0, 0, 0)),
        ],
        out_specs=pl.BlockSpec((1, tm, d), lambda i, j: (i, j, 0)),
        out_shape=jax.ShapeDtypeStruct((b, s, d), F32),
        compiler_params=_cparams(("parallel", "parallel")),
        name="out_proj_residual",
    )(y3, w_out, xa, mod)


def _norm_router_kernel(x_ref, g_ref, mod_ref, wr_ref, br_ref, h_ref, idx_ref, w_ref, *, lc, tm):
    _norm_and_route(x_ref[0], g_ref, mod_ref, wr_ref, br_ref, h_ref, idx_ref, w_ref, _is_ctx_rows(tm, lc))


def _norm_router(xa, g, mod, w_router, b_router, lc):
    b, s, d = xa.shape
    tm = _pick(s, (544, 256, 128))
    ne = w_router.shape[1]
    return pl.pallas_call(
        functools.partial(_norm_router_kernel, lc=lc, tm=tm),
        grid=(b, s // tm),
        in_specs=[
            pl.BlockSpec((1, tm, d), lambda i, j: (i, j, 0)),
            pl.BlockSpec((1, d), lambda i, j: (0, 0)),
            pl.BlockSpec((1, 2, 6, d), lambda i, j: (i, 0, 0, 0)),
            pl.BlockSpec((d, ne), lambda i, j: (0, 0)),
            pl.BlockSpec((1, ne), lambda i, j: (0, 0)),
        ],
        out_specs=[
            pl.BlockSpec((1, tm, d // LANES, LANES), lambda i, j: (i, j, 0, 0)),
            pl.BlockSpec((1, tm, TOP_K), lambda i, j: (i, j, 0)),
            pl.BlockSpec((1, tm, TOP_K), lambda i, j: (i, j, 0)),
        ],
        out_shape=[
            jax.ShapeDtypeStruct((b, s, d // LANES, LANES), F32),
            jax.ShapeDtypeStruct((b, s, TOP_K), jnp.int32),
            jax.ShapeDtypeStruct((b, s, TOP_K), F32),
        ],
        compiler_params=_cparams(("parallel", "parallel")),
        name="norm_router",
    )(xa, g.reshape(1, d), mod, w_router, b_router.reshape(1, ne))


def _pair_split_matrix():
    half = MXU_COLS // 2
    n = jnp.arange(MXU_COLS)
    src = jnp.where(n < half, 2 * n, 2 * (n - half) + 1)
    return (jnp.arange(MXU_COLS)[:, None] == src[None, :]).astype(BF16)


def _pair_split_index(width):
    half = MXU_COLS // 2
    n = jnp.arange(width)
    blk, r = n // MXU_COLS, n % MXU_COLS
    return blk * MXU_COLS + jnp.where(r < half, 2 * r, 2 * (r - half) + 1)


def _split_pairs_kernel(w_ref, p_ref, o_ref):
    for cb in range(w_ref.shape[-1] // MXU_COLS):
        sl = slice(cb * MXU_COLS, (cb + 1) * MXU_COLS)
        o_ref[0, :, sl] = jnp.dot(w_ref[0, :, sl].astype(BF16), p_ref[...],
                                  preferred_element_type=F32).astype(o_ref.dtype)


def _split_pairs(w):
    ne, d, n = w.shape
    tc = _pick(n, (2 * MXU_COLS, MXU_COLS))
    return pl.pallas_call(
        _split_pairs_kernel,
        grid=(ne, n // tc),
        in_specs=[
            pl.BlockSpec((1, d, tc), lambda e, i: (e, 0, i)),
            pl.BlockSpec((MXU_COLS, MXU_COLS), lambda e, i: (0, 0)),
        ],
        out_specs=pl.BlockSpec((1, d, tc), lambda e, i: (e, 0, i)),
        out_shape=jax.ShapeDtypeStruct((ne, d, n), BF16),
        compiler_params=_cparams(("parallel", "parallel")),
        name="split_gate_linear",
    )(w, _pair_split_matrix())


def _expert_kernel(te_ref, nu_ref, tok_ref, tok_next_ref, h_hbm, *refs):
    npart = EXPERT_WEIGHT_PARTS
    wgl_refs, bgl_ref = refs[0:npart], refs[npart]
    wd_refs, bd_ref = refs[npart + 1:2 * npart + 1], refs[2 * npart + 1]
    o_ref, wd_scr, xbuf, sem = refs[2 * npart + 2:2 * npart + 6]
    i = pl.program_id(0)
    half = MXU_COLS // 2
    tm = o_ref.shape[0]
    nseg = h_hbm.shape[1]
    n_used = nu_ref[0]
    used = i < n_used
    slot = lax.rem(i, 2)
    new_expert = jnp.logical_or(i == 0, te_ref[i] != te_ref[jnp.maximum(i - 1, 0)])
    wc = wd_refs[0].shape[2]

    def start_rows(idx_ref, dst_slot):
        for r in range(tm):
            dst = xbuf.at[pl.ds((dst_slot * tm + r) * nseg, nseg)]
            pltpu.make_async_copy(h_hbm.at[idx_ref[0, 0, r]], dst, sem.at[dst_slot]).start()

    def wait_rows(dst_slot):
        dst = xbuf.at[pl.ds(dst_slot * tm * nseg, tm * nseg)]
        pltpu.make_async_copy(dst, dst, sem.at[dst_slot]).wait()

    @pl.when(i == 0)
    def _():
        start_rows(tok_ref, 0)

    @pl.when(jnp.logical_and(used, new_expert))
    def _():
        for part in range(npart):
            wd_scr[:, part * wc:(part + 1) * wc] = wd_refs[part][0].astype(BF16)

    @pl.when(used)
    def _():
        start_rows(tok_next_ref, 1 - slot)
        wait_rows(slot)

    @pl.when(used)
    def _():
        base = slot * (tm * nseg)
        x = jnp.concatenate([xbuf[pl.ds(base + j, tm, stride=nseg), :] for j in range(nseg)],
                            axis=1).astype(BF16)
        gu = jnp.concatenate([jnp.dot(x, w[0], preferred_element_type=F32) for w in wgl_refs], axis=1)
        gu = gu + bgl_ref[0]
        nblk = gu.shape[1] // MXU_COLS
        gate = jnp.concatenate([gu[:, cb * MXU_COLS:cb * MXU_COLS + half] for cb in range(nblk)], axis=1)
        lin = jnp.concatenate([gu[:, cb * MXU_COLS + half:(cb + 1) * MXU_COLS] for cb in range(nblk)], axis=1)
        gate = jnp.minimum(gate, SWIGLU_LIMIT)
        lin = jnp.clip(lin, -SWIGLU_LIMIT, SWIGLU_LIMIT)
        act = gate * jax.nn.sigmoid(SWIGLU_ALPHA * gate) * (lin + 1.0)
        y = jnp.dot(act.astype(BF16), wd_scr[...], preferred_element_type=F32) + bd_ref[0]
        o_ref[...] = y.astype(o_ref.dtype)

    @pl.when(i == n_used - 1)
    def _():
        wait_rows(1 - slot)

    @pl.when(jnp.logical_not(used))
    def _():
        o_ref[...] = jnp.zeros_like(o_ref)


def _experts(h_rows, slot_tok, tile_expert, n_used, wgl, bgl, wd, bd):
    _, nseg, lanes = h_rows.shape
    d = nseg * lanes
    n_tiles, tm = slot_tok.shape
    _, _, ff2 = wgl.shape
    ff = ff2 // 2
    npart = EXPERT_WEIGHT_PARTS
    col_part = lambda part: (lambda i, te, nu: (te[i], 0, part))
    grid_spec = pltpu.PrefetchScalarGridSpec(
        num_scalar_prefetch=2,
        grid=(n_tiles,),
        in_specs=(
            [pl.BlockSpec((1, 1, tm), lambda i, te, nu: (i, 0, 0), memory_space=pltpu.SMEM),
             pl.BlockSpec((1, 1, tm), lambda i, te, nu: (jnp.minimum(i + 1, n_tiles - 1), 0, 0),
                          memory_space=pltpu.SMEM),
             pl.BlockSpec(memory_space=pl.ANY)]
            + [pl.BlockSpec((1, d, ff2 // npart), col_part(part)) for part in range(npart)]
            + [pl.BlockSpec((1, 1, ff2), lambda i, te, nu: (te[i], 0, 0))]
            + [pl.BlockSpec((1, ff, d // npart), col_part(part)) for part in range(npart)]
            + [pl.BlockSpec((1, 1, d), lambda i, te, nu: (te[i], 0, 0))]),
        out_specs=pl.BlockSpec((tm, d), lambda i, te, nu: (i, 0)),
        scratch_shapes=[pltpu.VMEM((ff, d), BF16),
                        pltpu.VMEM((2 * tm * nseg, lanes), F32),
                        pltpu.SemaphoreType.DMA((2,))],
    )
    tok3 = slot_tok[:, None, :]
    return pl.pallas_call(
        _expert_kernel,
        grid_spec=grid_spec,
        out_shape=jax.ShapeDtypeStruct((n_tiles * tm, d), BF16),
        compiler_params=_cparams(("arbitrary",)),
        name="experts",
    )(tile_expert, n_used, tok3, tok3, h_rows, *([wgl] * npart), bgl, *([wd] * npart), bd)


def _route(top_idx, n_slots):
    tm = EXPERT_TILE
    e = top_idx.reshape(-1)
    n = e.shape[0]
    onehot = (e[:, None] == jnp.arange(N_EXPERTS, dtype=jnp.int32)[None, :]).astype(jnp.int32)
    counts = jnp.sum(onehot, axis=0)
    rank = jnp.take_along_axis(jnp.cumsum(onehot, axis=0) - onehot, e[:, None], axis=1)[:, 0]
    padded = ((counts + tm - 1) // tm) * tm
    ends = jnp.cumsum(padded)
    dest = (ends - padded)[e] + rank
    n_used = (ends[-1] // tm).astype(jnp.int32)
    tile_start = jnp.arange(n_slots // tm, dtype=jnp.int32) * tm
    tile_expert = jnp.sum((ends[None, :] <= tile_start[:, None]).astype(jnp.int32), axis=1)
    last_used = jnp.sum((ends <= (n_used - 1) * tm).astype(jnp.int32))
    tile_expert = jnp.minimum(jnp.where(tile_start < ends[-1], tile_expert, last_used), N_EXPERTS - 1)
    src = jnp.zeros((n_slots,), jnp.int32).at[dest].set(jnp.arange(n, dtype=jnp.int32) // TOP_K)
    return src, dest, tile_expert.astype(jnp.int32), n_used.reshape(1)


def _combine_kernel(y0_ref, y1_ref, y2_ref, y3_ref, tw_ref, x_ref, mod_ref, fg_ref, o_ref, *, lc, tm, final):
    tw = tw_ref[0]
    moe = tw[:, 0:1] * y0_ref[0, 0].astype(F32)
    for k, y_ref in enumerate((y1_ref, y2_ref, y3_ref), start=1):
        moe = moe + tw[:, k:k + 1] * y_ref[0, 0].astype(F32)
    if final:
        xn = x_ref[0] + mod_ref[0, 1, 5:6, :] * moe
        xn = xn * lax.rsqrt(jnp.mean(xn * xn, axis=-1, keepdims=True) + EPS) * fg_ref[...]
    else:
        xn = x_ref[0] + _row_select(mod_ref, 5, _is_ctx_rows(tm, lc)) * moe
    o_ref[0] = xn


def _combine(yg, top_w, xa, mod, final_g, lc, final):
    b, s, d = xa.shape
    assert TOP_K == 4
    if final:
        tm = _pick(lc, (256, 128))
        assert (s - lc) % tm == 0
        off, rows = lc // tm, s - lc
    else:
        tm = _pick(s, (544, 256, 128))
        off, rows = 0, s
    y_specs = [pl.BlockSpec((1, 1, tm, d), functools.partial(lambda i, j, k: (k, i, j + off, 0), k=k))
               for k in range(TOP_K)]
    return pl.pallas_call(
        functools.partial(_combine_kernel, lc=lc, tm=tm, final=final),
        grid=(b, rows // tm),
        in_specs=y_specs + [
            pl.BlockSpec((1, tm, TOP_K), lambda i, j: (i, j + off, 0)),
            pl.BlockSpec((1, tm, d), lambda i, j: (i, j + off, 0)),
            pl.BlockSpec((1, 2, 6, d), lambda i, j: (i, 0, 0, 0)),
            pl.BlockSpec((1, d), lambda i, j: (0, 0)),
        ],
        out_specs=pl.BlockSpec((1, tm, d), lambda i, j: (i, j, 0)),
        out_shape=jax.ShapeDtypeStruct((b, rows, d), F32),
        compiler_params=_cparams(("parallel", "parallel")),
        name="moe_combine",
    )(yg, yg, yg, yg, top_w, xa, mod, final_g.reshape(1, d))


def kernel(x, c, ctx, c_ctx, w_ada, b_ada, norm_g, w_in, ssm_a_re, ssm_a_im, ssm_log_dt, ssm_b_re, ssm_b_im, ssm_c_re, ssm_c_im, ssm_d, ssm_w_glu, ssm_b_glu, attn_q_gain, attn_k_gain, mlstm_gate_bias, mlstm_norm_g, w_branch_ssm, w_branch_attn, w_branch_mlstm, w_out, w_router, b_router, w_gate_up, b_gate_up, w_down, b_down, final_g):
    b, lx, d = x.shape
    lc = ctx.shape[1]
    s = lc + lx
    t = b * s
    depth = w_in.shape[0]
    assert b + 1 <= 8 and lc % MLSTM_CHUNK == 0 and lx % MLSTM_CHUNK == 0

    ssm_w = ssm_d.shape[1]
    attn_w = w_branch_attn.shape[1]
    kv_w = ATTN_KV_HEADS * ATTN_HEAD_DIM
    mqk_w = MLSTM_HEADS * MLSTM_QK_DIM
    mv_w = MLSTM_HEADS * MLSTM_V_DIM
    n_gates = 4 * MLSTM_HEADS
    col_q = ssm_w
    col_k = col_q + attn_w
    col_v = col_k + kv_w
    col_mq = col_v + kv_w
    col_mk = col_mq + mqk_w
    col_mv = col_mk + mqk_w
    col_mo = col_mv + mv_w
    col_mg = col_mo + mv_w
    col_bg = col_mg + n_gates
    main_pad = (-(col_mg + n_gates)) % 256

    xa = jnp.concatenate([ctx, x], axis=1)
    cond8 = jnp.zeros((8, d), F32).at[0].set(c_ctx).at[1:1 + b].set(c)
    mods = _ada_modulation(cond8, w_ada, b_ada).reshape(depth, 8, 6, d)
    cos, sin = _rope_tables(lc, lx)
    ne, ff2 = w_gate_up.shape[1], w_gate_up.shape[3]
    wgl_all = _split_pairs(w_gate_up.reshape(depth * ne, d, ff2))
    bgl_all = b_gate_up.reshape(depth * ne, ff2)[:, _pair_split_index(ff2)][:, None, :]
    wd_all = w_down.reshape(depth * ne, ff2 // 2, d)
    bd_all = b_down.reshape(depth * ne, 1, d)

    for l in range(depth):
        last = l == depth - 1
        mod = jnp.stack([jnp.broadcast_to(mods[l, 0], (b, 6, d)), mods[l, 1:1 + b]], axis=1)

        h = _norm_mod(xa, norm_g[l, 0], mod, lc, 0).reshape(t, d)
        w_main = jnp.concatenate(
            [w_in[l, :, :col_bg], jnp.zeros((d, main_pad), F32)], axis=1).astype(BF16)
        p_main = _gemm(h, w_main, F32)
        gates = _gemm(h, w_in[l, :, col_bg:].astype(BF16), BF16, act="sigmoid")
        p3 = p_main.reshape(b, s, -1)

        ops = _s5_operators(ssm_a_re[l], ssm_a_im[l], ssm_log_dt[l], ssm_b_re[l], ssm_b_im[l],
                            ssm_c_re[l], ssm_c_im[l], ssm_d[l])
        ys = _s5_mix(p3, ops, ssm_w, lc)

        qh, kh, vh = _qk_prep(p3, cos, sin, attn_q_gain[l], attn_k_gain[l], col_q, col_k, col_v)
        att = _attention(qh, kh, vh, lc)

        hfb = _mlstm(p3, mlstm_gate_bias[l], lc, col_mq, col_mk, col_mv, col_mg)

        y = _merge(ys.reshape(t, ssm_w), att.reshape(t, attn_w), hfb.reshape(2, t, mv_w), p_main, gates,
                   ssm_w_glu[l].astype(BF16), ssm_b_glu[l], mlstm_norm_g[l],
                   w_branch_ssm[l].astype(BF16), w_branch_attn[l].astype(BF16),
                   w_branch_mlstm[l].astype(BF16), col_mo)
        xa = _out_proj(y.reshape(b, s, d), w_out[l].astype(BF16), xa, mod, lc)

        h2, top_idx, top_w = _norm_router(xa, norm_g[l, 1], mod, w_router[l], b_router[l], lc)
        n_slots = t * TOP_K + N_EXPERTS * EXPERT_TILE
        src, dest, tile_expert, n_used = _route(top_idx, n_slots)
        ysorted = _experts(h2.reshape(t, d // LANES, LANES), src.reshape(-1, EXPERT_TILE),
                           tile_expert + l * ne, n_used, wgl_all, bgl_all, wd_all, bd_all)
        yg = ysorted.at[dest.reshape(t, TOP_K).T].get(mode="promise_in_bounds").reshape(TOP_K, b, s, d)
        xa = _combine(yg, top_w, xa, mod, final_g, lc, last)

    return xa
```

```python
import functools
import math

import jax
import jax.numpy as jnp
from jax import lax
from jax.experimental import pallas as pl
from jax.experimental.pallas import tpu as pltpu

F32 = jnp.float32
BF16 = jnp.bfloat16

GRID_W = 64
EPS = 1e-6
SSM_GROUP = 16
SSM_STATE = 64
SSM_CHUNK = 8
SSM_PACK = 8
ATTN_HEAD_DIM = 128
ATTN_KV_HEADS = 2
ATTN_Q_PER_KV = 4
ATTN_HEADS_PER_PASS = 2
ROPE_THETA = 10000.0
MLSTM_HEADS = 4
MLSTM_QK_DIM = 64
MLSTM_V_DIM = 128
MLSTM_CHUNK = 64
N_EXPERTS = 32
TOP_K = 4
SWIGLU_LIMIT = 7.0
SWIGLU_ALPHA = 1.702

LANES = 128
MXU_COLS = 256
VMEM_LIMIT_BYTES = 56 * 1024 * 1024
EXPERT_TILE = 256
EXPERT_WEIGHT_PARTS = 4

def _cparams(sem):
    return pltpu.CompilerParams(dimension_semantics=sem, vmem_limit_bytes=VMEM_LIMIT_BYTES)


def _pick(n, cands):
    for c in cands:
        if n % c == 0:
            return c
    raise ValueError(f"no tile for {n} in {cands}")


def _ada_kernel(c_ref, w_ref, b_ref, o_ref):
    c = c_ref[...]
    s = c * jax.nn.sigmoid(c)
    o_ref[0] = jnp.dot(s, w_ref[0], preferred_element_type=F32) + b_ref[0]


def _ada_modulation(cond8, w_ada, b_ada):
    nl, d, n = w_ada.shape
    tn = _pick(n, (1024, 512, 256, 128))
    return pl.pallas_call(
        _ada_kernel,
        grid=(nl, n // tn),
        in_specs=[
            pl.BlockSpec((8, d), lambda l, j: (0, 0)),
            pl.BlockSpec((1, d, tn), lambda l, j: (l, 0, j)),
            pl.BlockSpec((1, 1, tn), lambda l, j: (l, 0, j)),
        ],
        out_specs=pl.BlockSpec((1, 8, tn), lambda l, j: (l, 0, j)),
        out_shape=jax.ShapeDtypeStruct((nl, 8, n), F32),
        compiler_params=_cparams(("parallel", "parallel")),
        name="ada_modulation",
    )(cond8, w_ada, b_ada.reshape(nl, 1, n))


def _row_select(mod_ref, k, is_ctx):
    return jnp.where(is_ctx, mod_ref[0, 0, k:k + 1, :], mod_ref[0, 1, k:k + 1, :])


def _modulated_norm(x, g, mod_ref, k_shift, is_ctx):
    ms = jnp.mean(x * x, axis=-1, keepdims=True)
    y = x * lax.rsqrt(ms + EPS) * g
    return y * (1.0 + _row_select(mod_ref, k_shift + 1, is_ctx)) + _row_select(mod_ref, k_shift, is_ctx)


def _is_ctx_rows(tm, lc):
    pos = pl.program_id(1) * tm + lax.broadcasted_iota(jnp.int32, (tm, 1), 0)
    return pos < lc


def _norm_mod_kernel(x_ref, g_ref, mod_ref, o_ref, *, lc, tm, k_shift):
    h = _modulated_norm(x_ref[0], g_ref[...], mod_ref, k_shift, _is_ctx_rows(tm, lc))
    o_ref[0] = h.astype(o_ref.dtype)


def _norm_mod(xa, g, mod, lc, k_shift):
    b, s, d = xa.shape
    tm = _pick(s, (544, 256, 128))
    return pl.pallas_call(
        functools.partial(_norm_mod_kernel, lc=lc, tm=tm, k_shift=k_shift),
        grid=(b, s // tm),
        in_specs=[
            pl.BlockSpec((1, tm, d), lambda i, j: (i, j, 0)),
            pl.BlockSpec((1, d), lambda i, j: (0, 0)),
            pl.BlockSpec((1, 2, 6, d), lambda i, j: (i, 0, 0, 0)),
        ],
        out_specs=pl.BlockSpec((1, tm, d), lambda i, j: (i, j, 0)),
        out_shape=jax.ShapeDtypeStruct((b, s, d), BF16),
        compiler_params=_cparams(("parallel", "parallel")),
        name="norm_mod",
    )(xa, g.reshape(1, d), mod)


def _norm_and_route(x, g_ref, mod_ref, wr_ref, br_ref, h_ref, idx_ref, w_ref, is_ctx):
    h = _modulated_norm(x, g_ref[...], mod_ref, 3, is_ctx)
    for j in range(h.shape[1] // LANES):
        h_ref[0, :, j, :] = h[:, j * LANES:(j + 1) * LANES]
    logits = jnp.dot(h, wr_ref[...], preferred_element_type=F32,
                     precision=lax.Precision.HIGHEST) + br_ref[...]
    lane = lax.broadcasted_iota(jnp.int32, logits.shape, 1).astype(F32)
    vals, idxs = [], []
    cur = logits
    for _ in range(TOP_K):
        mx = jnp.max(cur, axis=-1, keepdims=True)
        ix = jnp.min(jnp.where(cur == mx, lane, float(N_EXPERTS)), axis=-1, keepdims=True)
        vals.append(mx)
        idxs.append(ix)
        cur = jnp.where(lane == ix, -jnp.inf, cur)
    k_iota = lax.broadcasted_iota(jnp.int32, (logits.shape[0], TOP_K), 1)
    top = jnp.zeros((logits.shape[0], TOP_K), F32)
    top_i = jnp.zeros((logits.shape[0], TOP_K), F32)
    for k in range(TOP_K):
        top = jnp.where(k_iota == k, vals[k], top)
        top_i = jnp.where(k_iota == k, idxs[k], top_i)
    e = jnp.exp(top - vals[0])
    w_ref[0] = e / jnp.sum(e, axis=-1, keepdims=True)
    idx_ref[0] = top_i.astype(jnp.int32)


def _gemm_kernel(a_ref, w_ref, o_ref, *, act):
    acc = jnp.dot(a_ref[...], w_ref[...], preferred_element_type=F32)
    if act == "sigmoid":
        acc = jax.nn.sigmoid(acc)
    o_ref[...] = acc.astype(o_ref.dtype)


def _gemm(a, w, out_dtype, act=None, tn_cands=(1920, 1024, 512, 256, 128)):
    t, k = a.shape
    n = w.shape[1]
    tm = _pick(t, (1024, 768, 512, 256, 128))
    tn = _pick(n, tn_cands)
    return pl.pallas_call(
        functools.partial(_gemm_kernel, act=act),
        grid=(t // tm, n // tn),
        in_specs=[
            pl.BlockSpec((tm, k), lambda i, j: (i, 0)),
            pl.BlockSpec((k, tn), lambda i, j: (0, j)),
        ],
        out_specs=pl.BlockSpec((tm, tn), lambda i, j: (i, j)),
        out_shape=jax.ShapeDtypeStruct((t, n), out_dtype),
        compiler_params=_cparams(("parallel", "parallel")),
        name="gemm_" + (act or "plain"),
    )(a, w)


def _s5_operators(a_re, a_im, log_dt, b_re, b_im, c_re, c_im, d_skip):
    tc = SSM_CHUNK
    hp = lax.Precision.HIGHEST
    gs = SSM_PACK
    nsg = a_re.shape[1] // gs
    eye = jnp.eye(gs, dtype=F32)
    k_idx = jnp.arange(tc + 1, dtype=F32)
    lag = jnp.arange(tc)[None, :] - jnp.arange(tc)[:, None]
    ms, bcs, ccs, a16s = [], [], [], []
    for dirn in (0, 1):
        lam = lax.complex(a_re[dirn].astype(F32), a_im[dirn].astype(F32))
        dt = jnp.exp(log_dt[dirn].astype(F32))[:, None]
        pw = jnp.exp((lam * dt)[None] * k_idx[:, None, None])
        a_bar = pw[1]
        b_bar = ((a_bar - 1.0) / lam)[..., None] * lax.complex(b_re[dirn].astype(F32), b_im[dirn].astype(F32))
        c_mat = lax.complex(c_re[dirn].astype(F32), c_im[dirn].astype(F32))
        k_lag = jnp.real(jnp.einsum("ghp,kgp,gpj->kghj", c_mat, pw[:tc], b_bar, precision=hp))
        if dirn == 0:
            dist, valid = lag, lag >= 0
            end_pow = (tc - 1) - jnp.arange(tc)
            out_pow = jnp.arange(tc) + 1
        else:
            dist, valid = -lag, lag <= 0
            end_pow = jnp.arange(tc)
            out_pow = tc - jnp.arange(tc)
        kt = jnp.transpose(k_lag, (0, 1, 3, 2)).reshape(tc, nsg, gs, SSM_GROUP, SSM_GROUP)
        bd = (kt[:, :, :, :, None, :] * eye[None, None, :, None, :, None]).reshape(tc, nsg, LANES, LANES)
        blocks = jnp.where(valid[:, :, None, None, None], bd[jnp.clip(dist, 0, tc - 1)], 0.0)
        ms.append(jnp.transpose(blocks, (2, 0, 3, 1, 4)).reshape(nsg, tc * LANES, tc * LANES))
        bc = pw[end_pow][:, :, :, None] * b_bar[None]
        bcs.append(jnp.transpose(bc, (1, 0, 3, 2)))
        cw = c_mat[None] * pw[out_pow][:, :, None, :]
        ccs.append(jnp.transpose(cw, (1, 3, 0, 2)))
        a16s.append(pw[tc])

    def pack_rows(x):
        x = x.reshape((nsg, gs) + x.shape[1:])
        x = jnp.transpose(x, (0, 2, 3, 1, 4))[:, :, None]
        x = x * eye[None, None, :, None, :, None]
        return x.reshape(nsg, tc * gs * SSM_GROUP, gs * SSM_STATE)

    def pack_cols(x):
        x = x.reshape((nsg, gs) + x.shape[1:])
        x = jnp.transpose(x, (0, 2, 3, 1, 4))[:, None]
        x = x * eye[None, :, None, None, :, None]
        return x.reshape(nsg, gs * SSM_STATE, tc * gs * SSM_GROUP)

    m = ms[0] + ms[1]
    bc = jnp.concatenate([pack_rows(jnp.real(bcs[0])), pack_rows(jnp.real(bcs[1])),
                          pack_rows(jnp.imag(bcs[0])), pack_rows(jnp.imag(bcs[1]))], axis=-1)
    cc = jnp.concatenate([pack_cols(jnp.real(ccs[0])), pack_cols(jnp.real(ccs[1])),
                          pack_cols(-jnp.imag(ccs[0])), pack_cols(-jnp.imag(ccs[1]))], axis=1)
    a_f, a_b = a16s[0].reshape(nsg, gs * SSM_STATE), a16s[1].reshape(nsg, gs * SSM_STATE)
    a_chunk = jnp.stack([jnp.concatenate([jnp.real(a_f), jnp.real(a_b)], axis=-1),
                         jnp.concatenate([jnp.imag(a_f), jnp.imag(a_b)], axis=-1)], axis=1)
    dtile = jnp.tile(d_skip.astype(F32).reshape(nsg, 1, gs * SSM_GROUP), (1, 1, tc))
    return m.astype(BF16), bc.astype(BF16), cc.astype(BF16), a_chunk.astype(F32), dtile


def _s5_kernel(u_ref, m_ref, bc_ref, cc_ref, a_ref, d_ref, y_ref, vre_scr, vim_scr, hf_re, hb_re, hf_im, hb_im,
               *, nchunk, nctx):
    tc = SSM_CHUNK
    w = hf_re.shape[1]
    u = jnp.concatenate([u_ref[0, pl.ds(i, nchunk, stride=tc), :] for i in range(tc)], axis=1)
    ub = u.astype(BF16)
    v = jnp.dot(ub, bc_ref[0], preferred_element_type=F32)
    vre_scr[...] = v[:, 0:2 * w]
    vim_scr[...] = v[:, 2 * w:4 * w]
    ar = a_ref[0, 0:1, :]
    ai = a_ref[0, 1:2, :]
    hre = jnp.zeros((1, 2 * w), F32)
    him = jnp.zeros((1, 2 * w), F32)
    for s in range(nchunk):
        cf = s
        cb = (nctx - 1 - s) if s < nctx else (nchunk - 1 - (s - nctx))
        hf_re[cf:cf + 1, :] = hre[:, 0:w]
        hb_re[cb:cb + 1, :] = hre[:, w:2 * w]
        hf_im[cf:cf + 1, :] = him[:, 0:w]
        hb_im[cb:cb + 1, :] = him[:, w:2 * w]
        vre = jnp.concatenate([vre_scr[cf:cf + 1, 0:w], vre_scr[cb:cb + 1, w:2 * w]], axis=1)
        vim = jnp.concatenate([vim_scr[cf:cf + 1, 0:w], vim_scr[cb:cb + 1, w:2 * w]], axis=1)
        hre, him = ar * hre - ai * him + vre, ar * him + ai * hre + vim
    hp = jnp.concatenate([hf_re[...], hb_re[...], hf_im[...], hb_im[...]], axis=1)
    y = jnp.dot(ub, m_ref[0], preferred_element_type=F32)
    y = y + jnp.dot(hp.astype(BF16), cc_ref[0], preferred_element_type=F32)
    y = y + u * d_ref[0]
    for i in range(tc):
        y_ref[0, pl.ds(i, nchunk, stride=tc), :] = y[:, i * LANES:(i + 1) * LANES]


def _s5_mix(p3, ops, width, lc):
    m, bc, cc, a_chunk, dtile = ops
    b, s, _ = p3.shape
    tc = SSM_CHUNK
    nchunk = s // tc
    cw = SSM_PACK * SSM_GROUP
    assert cw == LANES and width % cw == 0 and s % tc == 0 and lc % tc == 0
    k = tc * cw
    sw = SSM_PACK * SSM_STATE
    return pl.pallas_call(
        functools.partial(_s5_kernel, nchunk=nchunk, nctx=lc // tc),
        grid=(width // cw, b),
        in_specs=[
            pl.BlockSpec((1, s, cw), lambda g, i: (i, 0, g)),
            pl.BlockSpec((1, k, k), lambda g, i: (g, 0, 0)),
            pl.BlockSpec((1, k, 4 * sw), lambda g, i: (g, 0, 0)),
            pl.BlockSpec((1, 4 * sw, k), lambda g, i: (g, 0, 0)),
            pl.BlockSpec((1, 2, 2 * sw), lambda g, i: (g, 0, 0)),
            pl.BlockSpec((1, 1, k), lambda g, i: (g, 0, 0)),
        ],
        out_specs=pl.BlockSpec((1, s, cw), lambda g, i: (i, 0, g)),
        out_shape=jax.ShapeDtypeStruct((b, s, width), F32),
        scratch_shapes=[pltpu.VMEM((nchunk, 2 * sw), F32)] * 2 + [pltpu.VMEM((nchunk, sw), F32)] * 4,
        compiler_params=_cparams(("parallel", "parallel")),
        name="s5_mix",
    )(p3, m, bc, cc, a_chunk, dtile)


def _rope_tables(lc, lx):
    n = ATTN_HEAD_DIM // 4
    freq = ROPE_THETA ** (-jnp.arange(n, dtype=F32) / n)
    t = jnp.arange(lx, dtype=jnp.int32)
    rows = (t // GRID_W).astype(F32)[:, None] * freq[None, :]
    cols = (t % GRID_W).astype(F32)[:, None] * freq[None, :]
    cos = jnp.concatenate([jnp.cos(rows), jnp.cos(rows), jnp.cos(cols), jnp.cos(cols)], axis=-1)
    sin = jnp.concatenate([-jnp.sin(rows), jnp.sin(rows), -jnp.sin(cols), jnp.sin(cols)], axis=-1)
    cos = jnp.concatenate([jnp.ones((lc, ATTN_HEAD_DIM), F32), cos], axis=0)
    sin = jnp.concatenate([jnp.zeros((lc, ATTN_HEAD_DIM), F32), sin], axis=0)
    return cos, sin


def _qk_prep_kernel(q_ref, k_ref, v_ref, cos_ref, sin_ref, qg_ref, kg_ref, qo_ref, ko_ref, vo_ref):
    hd = ATTN_HEAD_DIM
    cos = cos_ref[...]
    sin = sin_ref[...]
    lane = lax.broadcasted_iota(jnp.int32, cos.shape, 1)
    first = (lane % (hd // 2)) < (hd // 4)

    def prep(x, g):
        y = x * lax.rsqrt(jnp.mean(x * x, axis=-1, keepdims=True) + EPS) * g
        partner = jnp.where(first, pltpu.roll(y, hd - hd // 4, 1), pltpu.roll(y, hd // 4, 1))
        return y * cos + partner * sin

    for h in range(ATTN_Q_PER_KV):
        sl = slice(h * hd, (h + 1) * hd)
        qo_ref[0, :, sl] = prep(q_ref[0, :, sl], qg_ref[...]).astype(qo_ref.dtype)
    ko_ref[0] = prep(k_ref[0], kg_ref[...]).astype(ko_ref.dtype)
    vo_ref[0] = v_ref[0].astype(vo_ref.dtype)


def _qk_prep(p3, cos, sin, q_gain, k_gain, col_q, col_k, col_v):
    b, s, _ = p3.shape
    hd = ATTN_HEAD_DIM
    qw = ATTN_Q_PER_KV * hd
    tq = _pick(s, (544, 256, 128))
    return pl.pallas_call(
        _qk_prep_kernel,
        grid=(b, s // tq, ATTN_KV_HEADS),
        in_specs=[
            pl.BlockSpec((1, tq, qw), lambda i, j, kv: (i, j, col_q // qw + kv)),
            pl.BlockSpec((1, tq, hd), lambda i, j, kv: (i, j, col_k // hd + kv)),
            pl.BlockSpec((1, tq, hd), lambda i, j, kv: (i, j, col_v // hd + kv)),
            pl.BlockSpec((tq, hd), lambda i, j, kv: (j, 0)),
            pl.BlockSpec((tq, hd), lambda i, j, kv: (j, 0)),
            pl.BlockSpec((1, hd), lambda i, j, kv: (0, 0)),
            pl.BlockSpec((1, hd), lambda i, j, kv: (0, 0)),
        ],
        out_specs=[
            pl.BlockSpec((1, tq, qw), lambda i, j, kv: (i, j, kv)),
            pl.BlockSpec((1, tq, hd), lambda i, j, kv: (i, j, kv)),
            pl.BlockSpec((1, tq, hd), lambda i, j, kv: (i, j, kv)),
        ],
        out_shape=[
            jax.ShapeDtypeStruct((b, s, ATTN_KV_HEADS * qw), BF16),
            jax.ShapeDtypeStruct((b, s, ATTN_KV_HEADS * hd), BF16),
            jax.ShapeDtypeStruct((b, s, ATTN_KV_HEADS * hd), BF16),
        ],
        compiler_params=_cparams(("parallel", "parallel", "parallel")),
        name="qk_prep",
    )(p3, p3, p3, cos, sin, q_gain.reshape(1, hd), k_gain.reshape(1, hd))


def _attn_kernel(q_ref, k_ref, v_ref, o_ref, *, lc, tq):
    hd = ATTN_HEAD_DIM
    c = (hd ** -0.5) * math.log2(math.e)

    def run(kk, vv):
        hs = ATTN_HEADS_PER_PASS

        def scores(g):
            q = jnp.concatenate([q_ref[0, :, (g * hs + a) * hd:(g * hs + a + 1) * hd] for a in range(hs)], axis=0)
            return lax.dot_general(q, kk, (((1,), (1,)), ((), ())), preferred_element_type=F32)

        v_ext = jnp.concatenate([vv, jnp.ones((vv.shape[0], LANES), BF16)], axis=1)

        n_pass = ATTN_Q_PER_KV // hs
        s_next = scores(0)
        for g in range(n_pass):
            s = s_next
            if g + 1 < n_pass:
                s_next = scores(g + 1)
            m = jnp.max(s, axis=-1, keepdims=True)
            p = jnp.exp2((s - m) * c).astype(BF16)
            oe = jnp.dot(p, v_ext, preferred_element_type=F32)
            o = (oe[:, 0:hd] / oe[:, hd:hd + 1]).astype(o_ref.dtype)
            for a in range(hs):
                o_ref[0, :, (g * hs + a) * hd:(g * hs + a + 1) * hd] = o[a * tq:(a + 1) * tq, :]

    is_ctx_tile = pl.program_id(2) * tq < lc

    @pl.when(jnp.logical_not(is_ctx_tile))
    def _():
        run(k_ref[0], v_ref[0])

    @pl.when(is_ctx_tile)
    def _():
        run(k_ref[0, :lc, :], v_ref[0, :lc, :])


def _attention(qh, kh, vh, lc):
    b, s, _ = qh.shape
    hd = ATTN_HEAD_DIM
    qw = ATTN_Q_PER_KV * hd
    tq = _pick(lc, (256, 128))
    assert s % tq == 0
    return pl.pallas_call(
        functools.partial(_attn_kernel, lc=lc, tq=tq),
        grid=(b, ATTN_KV_HEADS, s // tq),
        in_specs=[
            pl.BlockSpec((1, tq, qw), lambda i, kv, j: (i, j, kv)),
            pl.BlockSpec((1, s, hd), lambda i, kv, j: (i, 0, kv)),
            pl.BlockSpec((1, s, hd), lambda i, kv, j: (i, 0, kv)),
        ],
        out_specs=pl.BlockSpec((1, tq, qw), lambda i, kv, j: (i, j, kv)),
        out_shape=jax.ShapeDtypeStruct((b, s, ATTN_KV_HEADS * qw), BF16),
        compiler_params=_cparams(("parallel", "parallel", "parallel")),
        name="attention",
    )(qh, kh, vh)


def _split3(x):
    hi = x.astype(BF16)
    r = x - hi.astype(F32)
    mid = r.astype(BF16)
    lo = (r - mid.astype(F32)).astype(BF16)
    return hi, mid, lo


def _log_sigmoid(x):
    return jnp.minimum(x, 0.0) - jnp.log(1.0 + jnp.exp(-jnp.abs(x)))


def _mlstm_kernel(q_ref, k_ref, v_ref, gc_ref, gr_ref, bc_ref, br_ref, o_ref, c_scr, n_scr, m_scr):
    nh, dk, dv, cl = MLSTM_HEADS, MLSTM_QK_DIM, MLSTM_V_DIM, MLSTM_CHUNK
    nb = q_ref.shape[0]
    d = pl.program_id(0)
    c = pl.program_id(1)

    @pl.when(c == 0)
    def _():
        c_scr[...] = jnp.zeros_like(c_scr)
        n_scr[...] = jnp.zeros_like(n_scr)
        m_scr[...] = jnp.zeros_like(m_scr)

    row = lax.broadcasted_iota(jnp.int32, (cl, cl), 0)
    col = lax.broadcasted_iota(jnp.int32, (cl, cl), 1)
    sign = jnp.where(d == 0, 1, -1)
    mask = sign * (row - col) >= 0
    tm = jnp.where(mask, 1.0, 0.0).astype(BF16)
    tmt = jnp.where(sign * (col - row) >= 0, 1.0, 0.0).astype(BF16)

    def pick_c(a, k):
        return jnp.where(d == 0, a[:, k:k + 1], a[:, 2 * nh + k:2 * nh + k + 1])

    def pick_r(a, k):
        return jnp.where(d == 0, a[k:k + 1, :], a[2 * nh + k:2 * nh + k + 1, :])

    nt = (((1,), (1,)), ((), ()))
    tn = (((0,), (0,)), ((), ()))
    heads = [(bi, h) for bi in range(nb) for h in range(nh)]

    gcs = [gc_ref[bi] + bc_ref[...] for bi in range(nb)]
    grs = [gr_ref[bi, 0] + br_ref[...] for bi in range(nb)]
    lf_cs = [_log_sigmoid(g) for g in gcs]
    lf_rs = [_log_sigmoid(g) for g in grs]
    cum_cs = [sum(jnp.dot(tm, part, preferred_element_type=F32) for part in _split3(x)) for x in lf_cs]
    cum_rs = [sum(jnp.dot(part, tmt, preferred_element_type=F32) for part in _split3(x)) for x in lf_rs]
    totals = [jnp.sum(x, axis=0, keepdims=True) for x in lf_cs]

    qs = [q_ref[bi][:, h * dk:(h + 1) * dk] * (dk ** -0.5) for bi, h in heads]
    ks = [k_ref[bi][:, h * dk:(h + 1) * dk] for bi, h in heads]
    vs = [v_ref[bi][:, h * dv:(h + 1) * dv] for bi, h in heads]
    qbs = [x.astype(BF16) for x in qs]
    kbs = [x.astype(BF16) for x in ks]
    m_sts = [m_scr[i][:, 0:1] for i in range(len(heads))]
    c_sts = [c_scr[i] for i in range(len(heads))]
    n_sts = [n_scr[i] for i in range(len(heads))]

    s_raw = [lax.dot_general(qbs[i], kbs[i], nt, preferred_element_type=F32) for i in range(len(heads))]
    cqs = [lax.dot_general(qbs[i], c_sts[i].astype(BF16), nt, preferred_element_type=F32)
           for i in range(len(heads))]

    cf_c = [pick_c(cum_cs[bi], nh + h) for bi, h in heads]
    cf_r = [pick_r(cum_rs[bi], nh + h) for bi, h in heads]
    li_c = [pick_c(gcs[bi], h) for bi, h in heads]
    li_r = [pick_r(grs[bi], h) for bi, h in heads]
    tot = [pick_c(totals[bi], nh + h) for bi, h in heads]

    log_d = [jnp.where(mask, cf_c[i] - cf_r[i] + li_r[i], -jnp.inf) for i in range(len(heads))]
    log_inter = [cf_c[i] + m_sts[i] for i in range(len(heads))]
    m_row = [jnp.maximum(log_inter[i], jnp.max(log_d[i], axis=-1, keepdims=True)) for i in range(len(heads))]
    s = [s_raw[i] * jnp.exp(log_d[i] - m_row[i]) for i in range(len(heads))]
    w_inter = [jnp.exp(log_inter[i] - m_row[i]) for i in range(len(heads))]
    sv = [jnp.dot(s[i].astype(BF16), vs[i].astype(BF16), preferred_element_type=F32) for i in range(len(heads))]
    den = [jnp.sum(s[i], axis=-1, keepdims=True)
           + w_inter[i] * jnp.sum(qs[i] * n_sts[i], axis=-1, keepdims=True) for i in range(len(heads))]
    for i, (bi, h) in enumerate(heads):
        num = sv[i] + w_inter[i] * cqs[i]
        o_ref[0, bi, :, h * dv:(h + 1) * dv] = num / jnp.maximum(jnp.abs(den[i]), jnp.exp(-m_row[i]))

    log_w = [tot[i] - cf_c[i] + li_c[i] for i in range(len(heads))]
    m_new = [jnp.maximum(tot[i] + m_sts[i], jnp.max(log_w[i], axis=0, keepdims=True)) for i in range(len(heads))]
    decay = [jnp.exp(tot[i] + m_sts[i] - m_new[i]) for i in range(len(heads))]
    w = [jnp.exp(log_w[i] - m_new[i]) for i in range(len(heads))]
    kv = [lax.dot_general((w[i] * vs[i]).astype(BF16), kbs[i], tn, preferred_element_type=F32)
          for i in range(len(heads))]
    for i in range(len(heads)):
        c_scr[i] = decay[i] * c_sts[i] + kv[i]
        n_scr[i] = decay[i] * n_sts[i] + jnp.sum(w[i] * ks[i], axis=0, keepdims=True)
        m_scr[i] = jnp.broadcast_to(m_new[i], (1, LANES))


def _mlstm(p3, gate_bias, lc, col_q, col_k, col_v, col_g):
    b, s, _ = p3.shape
    nh, dk, dv, cl = MLSTM_HEADS, MLSTM_QK_DIM, MLSTM_V_DIM, MLSTM_CHUNK
    nc = s // cl
    nctx = lc // cl
    ng = 4 * nh
    g_rows = jnp.transpose(p3[:, :, col_g:col_g + ng].reshape(b, nc, cl, ng), (0, 1, 3, 2))
    bias = gate_bias.astype(F32).reshape(ng)

    def chunk(d, c):
        bwd = jnp.where(c < nctx, nctx - 1 - c, nc - 1 - (c - nctx))
        return jnp.where(d == 0, c, bwd)

    return pl.pallas_call(
        _mlstm_kernel,
        grid=(2, nc),
        in_specs=[
            pl.BlockSpec((b, cl, nh * dk), lambda d, c: (0, chunk(d, c), col_q // (nh * dk))),
            pl.BlockSpec((b, cl, nh * dk), lambda d, c: (0, chunk(d, c), col_k // (nh * dk))),
            pl.BlockSpec((b, cl, nh * dv), lambda d, c: (0, chunk(d, c), col_v // (nh * dv))),
            pl.BlockSpec((b, cl, LANES), lambda d, c: (0, chunk(d, c), col_g // LANES)),
            pl.BlockSpec((b, 1, ng, cl), lambda d, c: (0, chunk(d, c), 0, 0)),
            pl.BlockSpec((1, LANES), lambda d, c: (0, 0)),
            pl.BlockSpec((ng, 1), lambda d, c: (0, 0)),
        ],
        out_specs=pl.BlockSpec((1, b, cl, nh * dv), lambda d, c: (d, 0, chunk(d, c), 0)),
        out_shape=jax.ShapeDtypeStruct((2, b, s, nh * dv), F32),
        scratch_shapes=[pltpu.VMEM((b * nh, dv, dk), F32), pltpu.VMEM((b * nh, 1, dk), F32),
                        pltpu.VMEM((b * nh, 1, LANES), F32)],
        compiler_params=_cparams(("arbitrary", "arbitrary")),
        name="mlstm",
    )(p3, p3, p3, p3, g_rows, jnp.pad(bias, (0, LANES - ng)).reshape(1, LANES), bias.reshape(ng, 1))


def _gelu_tanh(x):
    return 0.5 * x * (1.0 + jnp.tanh(math.sqrt(2.0 / math.pi) * (x + 0.044715 * (x * x * x))))


def _merge_kernel(ys_ref, att_ref, hf_ref, hb_ref, og_ref, gl_ref, wglu_ref, bglu_ref, ng_ref,
                  wbs_ref, wba_ref, wbm_ref, y_ref):
    d = y_ref.shape[-1]
    z = _gelu_tanh(ys_ref[...])
    s5 = z * jax.nn.sigmoid(jnp.dot(z.astype(BF16), wglu_ref[...], preferred_element_type=F32) + bglu_ref[...])
    hsum = hf_ref[0] + hb_ref[0]
    dv = MLSTM_V_DIM
    parts = []
    for h in range(MLSTM_HEADS):
        blk = hsum[:, h * dv:(h + 1) * dv]
        parts.append(blk * lax.rsqrt(jnp.mean(blk * blk, axis=-1, keepdims=True) + EPS))
    ml = jnp.concatenate(parts, axis=-1) * ng_ref[...] * jax.nn.sigmoid(og_ref[...])
    y = gl_ref[:, 0:d].astype(F32) * jnp.dot(s5.astype(BF16), wbs_ref[...], preferred_element_type=F32)
    y = y + gl_ref[:, d:2 * d].astype(F32) * jnp.dot(att_ref[...], wba_ref[...], preferred_element_type=F32)
    y = y + gl_ref[:, 2 * d:3 * d].astype(F32) * jnp.dot(ml.astype(BF16), wbm_ref[...], preferred_element_type=F32)
    y_ref[...] = y.astype(y_ref.dtype)


def _merge(ys, att, hfb, p_main, gates, wglu, bglu, ng, wbs, wba, wbm, col_o):
    t, ws = ys.shape
    wa = att.shape[1]
    wm = hfb.shape[-1]
    d = wbs.shape[1]
    tm = _pick(t, (256, 128))
    full = lambda shape: pl.BlockSpec(shape, lambda i: (0,) * len(shape))
    return pl.pallas_call(
        _merge_kernel,
        grid=(t // tm,),
        in_specs=[
            pl.BlockSpec((tm, ws), lambda i: (i, 0)),
            pl.BlockSpec((tm, wa), lambda i: (i, 0)),
            pl.BlockSpec((1, tm, wm), lambda i: (0, i, 0)),
            pl.BlockSpec((1, tm, wm), lambda i: (1, i, 0)),
            pl.BlockSpec((tm, wm), lambda i: (i, col_o // wm)),
            pl.BlockSpec((tm, 3 * d), lambda i: (i, 0)),
            full((ws, ws)), full((1, ws)), full((1, wm)),
            full((ws, d)), full((wa, d)), full((wm, d)),
        ],
        out_specs=pl.BlockSpec((tm, d), lambda i: (i, 0)),
        out_shape=jax.ShapeDtypeStruct((t, d), BF16),
        compiler_params=_cparams(("parallel",)),
        name="branch_merge",
    )(ys, att, hfb, hfb, p_main, gates, wglu, bglu.reshape(1, ws), ng.reshape(1, wm), wbs, wba, wbm)


def _out_proj_kernel(y_ref, w_ref, x_ref, mod_ref, o_ref, *, lc, tm):
    out = jnp.dot(y_ref[0], w_ref[...], preferred_element_type=F32)
    gate = _row_select(mod_ref, 2, _is_ctx_rows(tm, lc))
    o_ref[0] = x_ref[0] + gate * out


def _out_proj(y3, w_out, xa, mod, lc):
    b, s, d = xa.shape
    tm = _pick(s, (544, 256, 128))
    return pl.pallas_call(
        functools.partial(_out_proj_kernel, lc=lc, tm=tm),
        grid=(b, s // tm),
        in_specs=[
            pl.BlockSpec((1, tm, d), lambda i, j: (i, j, 0)),
            pl.BlockSpec((d, d), lambda i, j: (0, 0)),
            pl.BlockSpec((1, tm, d), lambda i, j: (i, j, 0)),
            pl.BlockSpec((1, 2, 6, d), lambda i, j: (i, 0, 0, 0)),
        ],
        out_specs=pl.BlockSpec((1, tm, d), lambda i, j: (i, j, 0)),
        out_shape=jax.ShapeDtypeStruct((b, s, d), F32),
        compiler_params=_cparams(("parallel", "parallel")),
        name="out_proj_residual",
    )(y3, w_out, xa, mod)


def _norm_router_kernel(x_ref, g_ref, mod_ref, wr_ref, br_ref, h_ref, idx_ref, w_ref, *, lc, tm):
    _norm_and_route(x_ref[0], g_ref, mod_ref, wr_ref, br_ref, h_ref, idx_ref, w_ref, _is_ctx_rows(tm, lc))


def _norm_router(xa, g, mod, w_router, b_router, lc):
    b, s, d = xa.shape
    tm = _pick(s, (544, 256, 128))
    ne = w_router.shape[1]
    return pl.pallas_call(
        functools.partial(_norm_router_kernel, lc=lc, tm=tm),
        grid=(b, s // tm),
        in_specs=[
            pl.BlockSpec((1, tm, d), lambda i, j: (i, j, 0)),
            pl.BlockSpec((1, d), lambda i, j: (0, 0)),
            pl.BlockSpec((1, 2, 6, d), lambda i, j: (i, 0, 0, 0)),
            pl.BlockSpec((d, ne), lambda i, j: (0, 0)),
            pl.BlockSpec((1, ne), lambda i, j: (0, 0)),
        ],
        out_specs=[
            pl.BlockSpec((1, tm, d // LANES, LANES), lambda i, j: (i, j, 0, 0)),
            pl.BlockSpec((1, tm, TOP_K), lambda i, j: (i, j, 0)),
            pl.BlockSpec((1, tm, TOP_K), lambda i, j: (i, j, 0)),
        ],
        out_shape=[
            jax.ShapeDtypeStruct((b, s, d // LANES, LANES), F32),
            jax.ShapeDtypeStruct((b, s, TOP_K), jnp.int32),
            jax.ShapeDtypeStruct((b, s, TOP_K), F32),
        ],
        compiler_params=_cparams(("parallel", "parallel")),
        name="norm_router",
    )(xa, g.reshape(1, d), mod, w_router, b_router.reshape(1, ne))


def _pair_split_matrix():
    half = MXU_COLS // 2
    n = jnp.arange(MXU_COLS)
    src = jnp.where(n < half, 2 * n, 2 * (n - half) + 1)
    return (jnp.arange(MXU_COLS)[:, None] == src[None, :]).astype(BF16)


def _pair_split_index(width):
    half = MXU_COLS // 2
    n = jnp.arange(width)
    blk, r = n // MXU_COLS, n % MXU_COLS
    return blk * MXU_COLS + jnp.where(r < half, 2 * r, 2 * (r - half) + 1)


def _split_pairs_kernel(w_ref, p_ref, after_ref, o_ref):
    del after_ref
    for cb in range(w_ref.shape[-1] // MXU_COLS):
        sl = slice(cb * MXU_COLS, (cb + 1) * MXU_COLS)
        o_ref[0, :, sl] = jnp.dot(w_ref[0, :, sl].astype(BF16), p_ref[...],
                                  preferred_element_type=F32).astype(o_ref.dtype)


def _split_pairs(w_all, first, count, after):
    _, d, n = w_all.shape
    tc = _pick(n, (2 * MXU_COLS, MXU_COLS))
    return pl.pallas_call(
        _split_pairs_kernel,
        grid=(count, n // tc),
        in_specs=[
            pl.BlockSpec((1, d, tc), lambda e, i: (first + e, 0, i)),
            pl.BlockSpec((MXU_COLS, MXU_COLS), lambda e, i: (0, 0)),
            pl.BlockSpec((8, LANES), lambda e, i: (0, 0)),
        ],
        out_specs=pl.BlockSpec((1, d, tc), lambda e, i: (e, 0, i)),
        out_shape=jax.ShapeDtypeStruct((count, d, n), BF16),
        compiler_params=_cparams(("parallel", "parallel")),
        name="split_gate_linear",
    )(w_all, _pair_split_matrix(), after)


def _expert_kernel(te_ref, nu_ref, tok_ref, tok_next_ref, h_hbm, *refs):
    npart = EXPERT_WEIGHT_PARTS
    wgl_refs, bgl_ref = refs[0:npart], refs[npart]
    wd_refs, bd_ref = refs[npart + 1:2 * npart + 1], refs[2 * npart + 1]
    o_ref, wd_scr, xbuf, sem = refs[2 * npart + 2:2 * npart + 6]
    i = pl.program_id(0)
    half = MXU_COLS // 2
    tm = o_ref.shape[0]
    nseg = h_hbm.shape[1]
    n_used = nu_ref[0]
    used = i < n_used
    slot = lax.rem(i, 2)
    new_expert = jnp.logical_or(i == 0, te_ref[i] != te_ref[jnp.maximum(i - 1, 0)])
    wc = wd_refs[0].shape[2]

    def start_rows(idx_ref, dst_slot):
        for r in range(tm):
            dst = xbuf.at[pl.ds((dst_slot * tm + r) * nseg, nseg)]
            pltpu.make_async_copy(h_hbm.at[idx_ref[0, 0, r]], dst, sem.at[dst_slot]).start()

    def wait_rows(dst_slot):
        dst = xbuf.at[pl.ds(dst_slot * tm * nseg, tm * nseg)]
        pltpu.make_async_copy(dst, dst, sem.at[dst_slot]).wait()

    @pl.when(i == 0)
    def _():
        start_rows(tok_ref, 0)

    @pl.when(jnp.logical_and(used, new_expert))
    def _():
        for part in range(npart):
            wd_scr[:, part * wc:(part + 1) * wc] = wd_refs[part][0].astype(BF16)

    @pl.when(used)
    def _():
        start_rows(tok_next_ref, 1 - slot)
        wait_rows(slot)

    @pl.when(used)
    def _():
        base = slot * (tm * nseg)
        x = jnp.concatenate([xbuf[pl.ds(base + j, tm, stride=nseg), :] for j in range(nseg)],
                            axis=1).astype(BF16)
        gu = jnp.concatenate([jnp.dot(x, w[0], preferred_element_type=F32) for w in wgl_refs], axis=1)
        gu = gu + bgl_ref[0]
        nblk = gu.shape[1] // MXU_COLS
        gate = jnp.concatenate([gu[:, cb * MXU_COLS:cb * MXU_COLS + half] for cb in range(nblk)], axis=1)
        lin = jnp.concatenate([gu[:, cb * MXU_COLS + half:(cb + 1) * MXU_COLS] for cb in range(nblk)], axis=1)
        gate = jnp.minimum(gate, SWIGLU_LIMIT)
        lin = jnp.clip(lin, -SWIGLU_LIMIT, SWIGLU_LIMIT)
        act = gate * jax.nn.sigmoid(SWIGLU_ALPHA * gate) * (lin + 1.0)
        y = jnp.dot(act.astype(BF16), wd_scr[...], preferred_element_type=F32) + bd_ref[0]
        o_ref[...] = y.astype(o_ref.dtype)

    @pl.when(i == n_used - 1)
    def _():
        wait_rows(1 - slot)

    @pl.when(jnp.logical_not(used))
    def _():
        o_ref[...] = jnp.zeros_like(o_ref)


def _experts(h_rows, slot_tok, tile_expert, n_used, wgl, bgl, wd, bd, first):
    _, nseg, lanes = h_rows.shape
    d = nseg * lanes
    n_tiles, tm = slot_tok.shape
    _, _, ff2 = wgl.shape
    ff = ff2 // 2
    npart = EXPERT_WEIGHT_PARTS
    col_part = lambda part: (lambda i, te, nu: (te[i], 0, part))
    wgl_part = lambda part: (lambda i, te, nu: (te[i] - first, 0, part))
    grid_spec = pltpu.PrefetchScalarGridSpec(
        num_scalar_prefetch=2,
        grid=(n_tiles,),
        in_specs=(
            [pl.BlockSpec((1, 1, tm), lambda i, te, nu: (i, 0, 0), memory_space=pltpu.SMEM),
             pl.BlockSpec((1, 1, tm), lambda i, te, nu: (jnp.minimum(i + 1, n_tiles - 1), 0, 0),
                          memory_space=pltpu.SMEM),
             pl.BlockSpec(memory_space=pl.ANY)]
            + [pl.BlockSpec((1, d, ff2 // npart), wgl_part(part)) for part in range(npart)]
            + [pl.BlockSpec((1, 1, ff2), lambda i, te, nu: (te[i], 0, 0))]
            + [pl.BlockSpec((1, ff, d // npart), col_part(part)) for part in range(npart)]
            + [pl.BlockSpec((1, 1, d), lambda i, te, nu: (te[i], 0, 0))]),
        out_specs=pl.BlockSpec((tm, d), lambda i, te, nu: (i, 0)),
        scratch_shapes=[pltpu.VMEM((ff, d), BF16),
                        pltpu.VMEM((2 * tm * nseg, lanes), F32),
                        pltpu.SemaphoreType.DMA((2,))],
    )
    tok3 = slot_tok[:, None, :]
    return pl.pallas_call(
        _expert_kernel,
        grid_spec=grid_spec,
        out_shape=jax.ShapeDtypeStruct((n_tiles * tm, d), BF16),
        compiler_params=_cparams(("arbitrary",)),
        name="experts",
    )(tile_expert, n_used, tok3, tok3, h_rows, *([wgl] * npart), bgl, *([wd] * npart), bd)


def _route(top_idx, n_slots):
    tm = EXPERT_TILE
    e = top_idx.reshape(-1)
    n = e.shape[0]
    onehot = (e[:, None] == jnp.arange(N_EXPERTS, dtype=jnp.int32)[None, :]).astype(jnp.int32)
    counts = jnp.sum(onehot, axis=0)
    rank = jnp.take_along_axis(jnp.cumsum(onehot, axis=0) - onehot, e[:, None], axis=1)[:, 0]
    padded = ((counts + tm - 1) // tm) * tm
    ends = jnp.cumsum(padded)
    dest = (ends - padded)[e] + rank
    n_used = (ends[-1] // tm).astype(jnp.int32)
    tile_start = jnp.arange(n_slots // tm, dtype=jnp.int32) * tm
    tile_expert = jnp.sum((ends[None, :] <= tile_start[:, None]).astype(jnp.int32), axis=1)
    last_used = jnp.sum((ends <= (n_used - 1) * tm).astype(jnp.int32))
    tile_expert = jnp.minimum(jnp.where(tile_start < ends[-1], tile_expert, last_used), N_EXPERTS - 1)
    src = jnp.zeros((n_slots,), jnp.int32).at[dest].set(jnp.arange(n, dtype=jnp.int32) // TOP_K)
    return src, dest, tile_expert.astype(jnp.int32), n_used.reshape(1)


def _combine_kernel(y0_ref, y1_ref, y2_ref, y3_ref, tw_ref, x_ref, mod_ref, fg_ref, o_ref, *, lc, tm, final):
    tw = tw_ref[0]
    moe = tw[:, 0:1] * y0_ref[0, 0].astype(F32)
    for k, y_ref in enumerate((y1_ref, y2_ref, y3_ref), start=1):
        moe = moe + tw[:, k:k + 1] * y_ref[0, 0].astype(F32)
    if final:
        xn = x_ref[0] + mod_ref[0, 1, 5:6, :] * moe
        xn = xn * lax.rsqrt(jnp.mean(xn * xn, axis=-1, keepdims=True) + EPS) * fg_ref[...]
    else:
        xn = x_ref[0] + _row_select(mod_ref, 5, _is_ctx_rows(tm, lc)) * moe
    o_ref[0] = xn


def _combine(yg, top_w, xa, mod, final_g, lc, final):
    b, s, d = xa.shape
    assert TOP_K == 4
    if final:
        tm = _pick(lc, (256, 128))
        assert (s - lc) % tm == 0
        off, rows = lc // tm, s - lc
    else:
        tm = _pick(s, (544, 256, 128))
        off, rows = 0, s
    y_specs = [pl.BlockSpec((1, 1, tm, d), functools.partial(lambda i, j, k: (k, i, j + off, 0), k=k))
               for k in range(TOP_K)]
    return pl.pallas_call(
        functools.partial(_combine_kernel, lc=lc, tm=tm, final=final),
        grid=(b, rows // tm),
        in_specs=y_specs + [
            pl.BlockSpec((1, tm, TOP_K), lambda i, j: (i, j + off, 0)),
            pl.BlockSpec((1, tm, d), lambda i, j: (i, j + off, 0)),
            pl.BlockSpec((1, 2, 6, d), lambda i, j: (i, 0, 0, 0)),
            pl.BlockSpec((1, d), lambda i, j: (0, 0)),
        ],
        out_specs=pl.BlockSpec((1, tm, d), lambda i, j: (i, j, 0)),
        out_shape=jax.ShapeDtypeStruct((b, rows, d), F32),
        compiler_params=_cparams(("parallel", "parallel")),
        name="moe_combine",
    )(yg, yg, yg, yg, top_w, xa, mod, final_g.reshape(1, d))


def kernel(x, c, ctx, c_ctx, w_ada, b_ada, norm_g, w_in, ssm_a_re, ssm_a_im, ssm_log_dt, ssm_b_re, ssm_b_im, ssm_c_re, ssm_c_im, ssm_d, ssm_w_glu, ssm_b_glu, attn_q_gain, attn_k_gain, mlstm_gate_bias, mlstm_norm_g, w_branch_ssm, w_branch_attn, w_branch_mlstm, w_out, w_router, b_router, w_gate_up, b_gate_up, w_down, b_down, final_g):
    b, lx, d = x.shape
    lc = ctx.shape[1]
    s = lc + lx
    t = b * s
    depth = w_in.shape[0]
    assert b + 1 <= 8 and lc % MLSTM_CHUNK == 0 and lx % MLSTM_CHUNK == 0

    ssm_w = ssm_d.shape[1]
    attn_w = w_branch_attn.shape[1]
    kv_w = ATTN_KV_HEADS * ATTN_HEAD_DIM
    mqk_w = MLSTM_HEADS * MLSTM_QK_DIM
    mv_w = MLSTM_HEADS * MLSTM_V_DIM
    n_gates = 4 * MLSTM_HEADS
    col_q = ssm_w
    col_k = col_q + attn_w
    col_v = col_k + kv_w
    col_mq = col_v + kv_w
    col_mk = col_mq + mqk_w
    col_mv = col_mk + mqk_w
    col_mo = col_mv + mv_w
    col_mg = col_mo + mv_w
    col_bg = col_mg + n_gates
    main_pad = (-(col_mg + n_gates)) % 256

    xa = jnp.concatenate([ctx, x], axis=1)
    cond8 = jnp.zeros((8, d), F32).at[0].set(c_ctx).at[1:1 + b].set(c)
    mods = _ada_modulation(cond8, w_ada, b_ada).reshape(depth, 8, 6, d)
    cos, sin = _rope_tables(lc, lx)
    ne, ff2 = w_gate_up.shape[1], w_gate_up.shape[3]
    after = jnp.zeros((8, LANES), F32)
    bgl_all = b_gate_up.reshape(depth * ne, ff2)[:, _pair_split_index(ff2)][:, None, :]
    wd_all = w_down.reshape(depth * ne, ff2 // 2, d)
    bd_all = b_down.reshape(depth * ne, 1, d)

    for l in range(depth):
        last = l == depth - 1
        mod = jnp.stack([jnp.broadcast_to(mods[l, 0], (b, 6, d)), mods[l, 1:1 + b]], axis=1)

        h = _norm_mod(xa, norm_g[l, 0], mod, lc, 0).reshape(t, d)
        w_main = jnp.concatenate(
            [w_in[l, :, :col_bg], jnp.zeros((d, main_pad), F32)], axis=1).astype(BF16)
        p_main = _gemm(h, w_main, F32)
        gates = _gemm(h, w_in[l, :, col_bg:].astype(BF16), BF16, act="sigmoid")
        p3 = p_main.reshape(b, s, -1)

        ops = _s5_operators(ssm_a_re[l], ssm_a_im[l], ssm_log_dt[l], ssm_b_re[l], ssm_b_im[l],
                            ssm_c_re[l], ssm_c_im[l], ssm_d[l])
        ys = _s5_mix(p3, ops, ssm_w, lc)

        qh, kh, vh = _qk_prep(p3, cos, sin, attn_q_gain[l], attn_k_gain[l], col_q, col_k, col_v)
        att = _attention(qh, kh, vh, lc)

        hfb = _mlstm(p3, mlstm_gate_bias[l], lc, col_mq, col_mk, col_mv, col_mg)

        y = _merge(ys.reshape(t, ssm_w), att.reshape(t, attn_w), hfb.reshape(2, t, mv_w), p_main, gates,
                   ssm_w_glu[l].astype(BF16), ssm_b_glu[l], mlstm_norm_g[l],
                   w_branch_ssm[l].astype(BF16), w_branch_attn[l].astype(BF16),
                   w_branch_mlstm[l].astype(BF16), col_mo)
        xa = _out_proj(y.reshape(b, s, d), w_out[l].astype(BF16), xa, mod, lc)

        h2, top_idx, top_w = _norm_router(xa, norm_g[l, 1], mod, w_router[l], b_router[l], lc)
        n_slots = t * TOP_K + N_EXPERTS * EXPERT_TILE
        src, dest, tile_expert, n_used = _route(top_idx, n_slots)
        wgl = _split_pairs(w_gate_up.reshape(depth * ne, d, ff2), l * ne, ne, after)
        ysorted = _experts(h2.reshape(t, d // LANES, LANES), src.reshape(-1, EXPERT_TILE),
                           tile_expert + l * ne, n_used, wgl, bgl_all, wd_all, bd_all, l * ne)
        after = ysorted[:8, :LANES].astype(F32)
        yg = ysorted.at[dest.reshape(t, TOP_K).T].get(mode="promise_in_bounds").reshape(TOP_K, b, s, d)
        xa = _combine(yg, top_w, xa, mod, final_g, lc, last)

    return xa
```

```python
import functools
import math

import jax
import jax.numpy as jnp
from jax import lax
from jax.experimental import pallas as pl
from jax.experimental.pallas import tpu as pltpu

F32 = jnp.float32
BF16 = jnp.bfloat16

GRID_W = 64
EPS = 1e-6
SSM_GROUP = 16
SSM_STATE = 64
SSM_CHUNK = 8
SSM_PACK = 8
ATTN_HEAD_DIM = 128
ATTN_KV_HEADS = 2
ATTN_Q_PER_KV = 4
ATTN_HEADS_PER_PASS = 2
ROPE_THETA = 10000.0
MLSTM_HEADS = 4
MLSTM_QK_DIM = 64
MLSTM_V_DIM = 128
MLSTM_CHUNK = 64
N_EXPERTS = 32
TOP_K = 4
SWIGLU_LIMIT = 7.0
SWIGLU_ALPHA = 1.702

LANES = 128
MXU_COLS = 256
VMEM_LIMIT_BYTES = 56 * 1024 * 1024
EXPERT_TILE = 256
EXPERT_WEIGHT_PARTS = 4

def _cparams(sem):
    return pltpu.CompilerParams(dimension_semantics=sem, vmem_limit_bytes=VMEM_LIMIT_BYTES)


def _pick(n, cands):
    for c in cands:
        if n % c == 0:
            return c
    raise ValueError(f"no tile for {n} in {cands}")


def _ada_kernel(c_ref, w_ref, b_ref, o_ref):
    c = c_ref[...]
    s = c * jax.nn.sigmoid(c)
    o_ref[0] = jnp.dot(s, w_ref[0], preferred_element_type=F32) + b_ref[0]


def _ada_modulation(cond8, w_ada, b_ada):
    nl, d, n = w_ada.shape
    tn = _pick(n, (1024, 512, 256, 128))
    return pl.pallas_call(
        _ada_kernel,
        grid=(nl, n // tn),
        in_specs=[
            pl.BlockSpec((8, d), lambda l, j: (0, 0)),
            pl.BlockSpec((1, d, tn), lambda l, j: (l, 0, j)),
            pl.BlockSpec((1, 1, tn), lambda l, j: (l, 0, j)),
        ],
        out_specs=pl.BlockSpec((1, 8, tn), lambda l, j: (l, 0, j)),
        out_shape=jax.ShapeDtypeStruct((nl, 8, n), F32),
        compiler_params=_cparams(("parallel", "parallel")),
        name="ada_modulation",
    )(cond8, w_ada, b_ada.reshape(nl, 1, n))


def _row_select(mod_ref, k, is_ctx):
    return jnp.where(is_ctx, mod_ref[0, 0, k:k + 1, :], mod_ref[0, 1, k:k + 1, :])


def _modulated_norm(x, g, mod_ref, k_shift, is_ctx):
    ms = jnp.mean(x * x, axis=-1, keepdims=True)
    y = x * lax.rsqrt(ms + EPS) * g
    return y * (1.0 + _row_select(mod_ref, k_shift + 1, is_ctx)) + _row_select(mod_ref, k_shift, is_ctx)


def _is_ctx_rows(tm, lc):
    pos = pl.program_id(1) * tm + lax.broadcasted_iota(jnp.int32, (tm, 1), 0)
    return pos < lc


def _norm_mod_kernel(x_ref, g_ref, mod_ref, o_ref, *, lc, tm, k_shift):
    h = _modulated_norm(x_ref[0], g_ref[...], mod_ref, k_shift, _is_ctx_rows(tm, lc))
    o_ref[0] = h.astype(o_ref.dtype)


def _norm_mod(xa, g, mod, lc, k_shift):
    b, s, d = xa.shape
    tm = _pick(s, (544, 256, 128))
    return pl.pallas_call(
        functools.partial(_norm_mod_kernel, lc=lc, tm=tm, k_shift=k_shift),
        grid=(b, s // tm),
        in_specs=[
            pl.BlockSpec((1, tm, d), lambda i, j: (i, j, 0)),
            pl.BlockSpec((1, d), lambda i, j: (0, 0)),
            pl.BlockSpec((1, 2, 6, d), lambda i, j: (i, 0, 0, 0)),
        ],
        out_specs=pl.BlockSpec((1, tm, d), lambda i, j: (i, j, 0)),
        out_shape=jax.ShapeDtypeStruct((b, s, d), BF16),
        compiler_params=_cparams(("parallel", "parallel")),
        name="norm_mod",
    )(xa, g.reshape(1, d), mod)


def _norm_and_route(x, g_ref, mod_ref, wr_ref, br_ref, h_ref, idx_ref, w_ref, is_ctx):
    h = _modulated_norm(x, g_ref[...], mod_ref, 3, is_ctx)
    for j in range(h.shape[1] // LANES):
        h_ref[0, :, j, :] = h[:, j * LANES:(j + 1) * LANES]
    logits = jnp.dot(h, wr_ref[...], preferred_element_type=F32,
                     precision=lax.Precision.HIGHEST) + br_ref[...]
    lane = lax.broadcasted_iota(jnp.int32, logits.shape, 1).astype(F32)
    vals, idxs = [], []
    cur = logits
    for _ in range(TOP_K):
        mx = jnp.max(cur, axis=-1, keepdims=True)
        ix = jnp.min(jnp.where(cur == mx, lane, float(N_EXPERTS)), axis=-1, keepdims=True)
        vals.append(mx)
        idxs.append(ix)
        cur = jnp.where(lane == ix, -jnp.inf, cur)
    k_iota = lax.broadcasted_iota(jnp.int32, (logits.shape[0], TOP_K), 1)
    top = jnp.zeros((logits.shape[0], TOP_K), F32)
    top_i = jnp.zeros((logits.shape[0], TOP_K), F32)
    for k in range(TOP_K):
        top = jnp.where(k_iota == k, vals[k], top)
        top_i = jnp.where(k_iota == k, idxs[k], top_i)
    e = jnp.exp(top - vals[0])
    w_ref[0] = e / jnp.sum(e, axis=-1, keepdims=True)
    idx_ref[0] = top_i.astype(jnp.int32)


def _gemm_kernel(a_ref, w_ref, o_ref, *, act):
    acc = jnp.dot(a_ref[...], w_ref[...], preferred_element_type=F32)
    if act == "sigmoid":
        acc = jax.nn.sigmoid(acc)
    o_ref[...] = acc.astype(o_ref.dtype)


def _gemm(a, w, out_dtype, act=None, tn_cands=(1920, 1024, 512, 256, 128)):
    t, k = a.shape
    n = w.shape[1]
    tm = _pick(t, (1024, 768, 512, 256, 128))
    tn = _pick(n, tn_cands)
    return pl.pallas_call(
        functools.partial(_gemm_kernel, act=act),
        grid=(t // tm, n // tn),
        in_specs=[
            pl.BlockSpec((tm, k), lambda i, j: (i, 0)),
            pl.BlockSpec((k, tn), lambda i, j: (0, j)),
        ],
        out_specs=pl.BlockSpec((tm, tn), lambda i, j: (i, j)),
        out_shape=jax.ShapeDtypeStruct((t, n), out_dtype),
        compiler_params=_cparams(("parallel", "parallel")),
        name="gemm_" + (act or "plain"),
    )(a, w)


def _s5_operators(a_re, a_im, log_dt, b_re, b_im, c_re, c_im, d_skip):
    tc = SSM_CHUNK
    hp = lax.Precision.HIGHEST
    gs = SSM_PACK
    nsg = a_re.shape[1] // gs
    eye = jnp.eye(gs, dtype=F32)
    k_idx = jnp.arange(tc + 1, dtype=F32)
    lag = jnp.arange(tc)[None, :] - jnp.arange(tc)[:, None]
    ms, bcs, ccs, a16s = [], [], [], []
    for dirn in (0, 1):
        lam = lax.complex(a_re[dirn].astype(F32), a_im[dirn].astype(F32))
        dt = jnp.exp(log_dt[dirn].astype(F32))[:, None]
        pw = jnp.exp((lam * dt)[None] * k_idx[:, None, None])
        a_bar = pw[1]
        b_bar = ((a_bar - 1.0) / lam)[..., None] * lax.complex(b_re[dirn].astype(F32), b_im[dirn].astype(F32))
        c_mat = lax.complex(c_re[dirn].astype(F32), c_im[dirn].astype(F32))
        k_lag = jnp.real(jnp.einsum("ghp,kgp,gpj->kghj", c_mat, pw[:tc], b_bar, precision=hp))
        if dirn == 0:
            dist, valid = lag, lag >= 0
            end_pow = (tc - 1) - jnp.arange(tc)
            out_pow = jnp.arange(tc) + 1
        else:
            dist, valid = -lag, lag <= 0
            end_pow = jnp.arange(tc)
            out_pow = tc - jnp.arange(tc)
        kt = jnp.transpose(k_lag, (0, 1, 3, 2)).reshape(tc, nsg, gs, SSM_GROUP, SSM_GROUP)
        bd = (kt[:, :, :, :, None, :] * eye[None, None, :, None, :, None]).reshape(tc, nsg, LANES, LANES)
        blocks = jnp.where(valid[:, :, None, None, None], bd[jnp.clip(dist, 0, tc - 1)], 0.0)
        ms.append(jnp.transpose(blocks, (2, 0, 3, 1, 4)).reshape(nsg, tc * LANES, tc * LANES))
        bc = pw[end_pow][:, :, :, None] * b_bar[None]
        bcs.append(jnp.transpose(bc, (1, 0, 3, 2)))
        cw = c_mat[None] * pw[out_pow][:, :, None, :]
        ccs.append(jnp.transpose(cw, (1, 3, 0, 2)))
        a16s.append(pw[tc])

    def pack_rows(x):
        x = x.reshape((nsg, gs) + x.shape[1:])
        x = jnp.transpose(x, (0, 2, 3, 1, 4))[:, :, None]
        x = x * eye[None, None, :, None, :, None]
        return x.reshape(nsg, tc * gs * SSM_GROUP, gs * SSM_STATE)

    def pack_cols(x):
        x = x.reshape((nsg, gs) + x.shape[1:])
        x = jnp.transpose(x, (0, 2, 3, 1, 4))[:, None]
        x = x * eye[None, :, None, None, :, None]
        return x.reshape(nsg, gs * SSM_STATE, tc * gs * SSM_GROUP)

    m = ms[0] + ms[1]
    bc = jnp.concatenate([pack_rows(jnp.real(bcs[0])), pack_rows(jnp.real(bcs[1])),
                          pack_rows(jnp.imag(bcs[0])), pack_rows(jnp.imag(bcs[1]))], axis=-1)
    cc = jnp.concatenate([pack_cols(jnp.real(ccs[0])), pack_cols(jnp.real(ccs[1])),
                          pack_cols(-jnp.imag(ccs[0])), pack_cols(-jnp.imag(ccs[1]))], axis=1)
    a_f, a_b = a16s[0].reshape(nsg, gs * SSM_STATE), a16s[1].reshape(nsg, gs * SSM_STATE)
    a_chunk = jnp.stack([jnp.concatenate([jnp.real(a_f), jnp.real(a_b)], axis=-1),
                         jnp.concatenate([jnp.imag(a_f), jnp.imag(a_b)], axis=-1)], axis=1)
    dtile = jnp.tile(d_skip.astype(F32).reshape(nsg, 1, gs * SSM_GROUP), (1, 1, tc))
    return m.astype(BF16), bc.astype(BF16), cc.astype(BF16), a_chunk.astype(F32), dtile


def _s5_kernel(u_ref, m_ref, bc_ref, cc_ref, a_ref, d_ref, y_ref, vre_scr, vim_scr, hf_re, hb_re, hf_im, hb_im,
               *, nchunk, nctx):
    tc = SSM_CHUNK
    w = hf_re.shape[1]
    u = jnp.concatenate([u_ref[0, pl.ds(i, nchunk, stride=tc), :] for i in range(tc)], axis=1)
    ub = u.astype(BF16)
    v = jnp.dot(ub, bc_ref[0], preferred_element_type=F32)
    vre_scr[...] = v[:, 0:2 * w]
    vim_scr[...] = v[:, 2 * w:4 * w]
    ar = a_ref[0, 0:1, :]
    ai = a_ref[0, 1:2, :]
    hre = jnp.zeros((1, 2 * w), F32)
    him = jnp.zeros((1, 2 * w), F32)
    for s in range(nchunk):
        cf = s
        cb = (nctx - 1 - s) if s < nctx else (nchunk - 1 - (s - nctx))
        hf_re[cf:cf + 1, :] = hre[:, 0:w]
        hb_re[cb:cb + 1, :] = hre[:, w:2 * w]
        hf_im[cf:cf + 1, :] = him[:, 0:w]
        hb_im[cb:cb + 1, :] = him[:, w:2 * w]
        vre = jnp.concatenate([vre_scr[cf:cf + 1, 0:w], vre_scr[cb:cb + 1, w:2 * w]], axis=1)
        vim = jnp.concatenate([vim_scr[cf:cf + 1, 0:w], vim_scr[cb:cb + 1, w:2 * w]], axis=1)
        hre, him = ar * hre - ai * him + vre, ar * him + ai * hre + vim
    hp = jnp.concatenate([hf_re[...], hb_re[...], hf_im[...], hb_im[...]], axis=1)
    y = jnp.dot(ub, m_ref[0], preferred_element_type=F32)
    y = y + jnp.dot(hp.astype(BF16), cc_ref[0], preferred_element_type=F32)
    y = y + u * d_ref[0]
    for i in range(tc):
        y_ref[0, pl.ds(i, nchunk, stride=tc), :] = y[:, i * LANES:(i + 1) * LANES]


def _s5_mix(p3, ops, width, lc):
    m, bc, cc, a_chunk, dtile = ops
    b, s, _ = p3.shape
    tc = SSM_CHUNK
    nchunk = s // tc
    cw = SSM_PACK * SSM_GROUP
    assert cw == LANES and width % cw == 0 and s % tc == 0 and lc % tc == 0
    k = tc * cw
    sw = SSM_PACK * SSM_STATE
    return pl.pallas_call(
        functools.partial(_s5_kernel, nchunk=nchunk, nctx=lc // tc),
        grid=(width // cw, b),
        in_specs=[
            pl.BlockSpec((1, s, cw), lambda g, i: (i, 0, g)),
            pl.BlockSpec((1, k, k), lambda g, i: (g, 0, 0)),
            pl.BlockSpec((1, k, 4 * sw), lambda g, i: (g, 0, 0)),
            pl.BlockSpec((1, 4 * sw, k), lambda g, i: (g, 0, 0)),
            pl.BlockSpec((1, 2, 2 * sw), lambda g, i: (g, 0, 0)),
            pl.BlockSpec((1, 1, k), lambda g, i: (g, 0, 0)),
        ],
        out_specs=pl.BlockSpec((1, s, cw), lambda g, i: (i, 0, g)),
        out_shape=jax.ShapeDtypeStruct((b, s, width), F32),
        scratch_shapes=[pltpu.VMEM((nchunk, 2 * sw), F32)] * 2 + [pltpu.VMEM((nchunk, sw), F32)] * 4,
        compiler_params=_cparams(("parallel", "parallel")),
        name="s5_mix",
    )(p3, m, bc, cc, a_chunk, dtile)


def _rope_tables(lc, lx):
    n = ATTN_HEAD_DIM // 4
    freq = ROPE_THETA ** (-jnp.arange(n, dtype=F32) / n)
    t = jnp.arange(lx, dtype=jnp.int32)
    rows = (t // GRID_W).astype(F32)[:, None] * freq[None, :]
    cols = (t % GRID_W).astype(F32)[:, None] * freq[None, :]
    cos = jnp.concatenate([jnp.cos(rows), jnp.cos(rows), jnp.cos(cols), jnp.cos(cols)], axis=-1)
    sin = jnp.concatenate([-jnp.sin(rows), jnp.sin(rows), -jnp.sin(cols), jnp.sin(cols)], axis=-1)
    cos = jnp.concatenate([jnp.ones((lc, ATTN_HEAD_DIM), F32), cos], axis=0)
    sin = jnp.concatenate([jnp.zeros((lc, ATTN_HEAD_DIM), F32), sin], axis=0)
    return cos, sin


def _qk_prep_kernel(q_ref, k_ref, v_ref, cos_ref, sin_ref, qg_ref, kg_ref, qo_ref, ko_ref, vo_ref):
    hd = ATTN_HEAD_DIM
    cos = cos_ref[...]
    sin = sin_ref[...]
    lane = lax.broadcasted_iota(jnp.int32, cos.shape, 1)
    first = (lane % (hd // 2)) < (hd // 4)

    def prep(x, g):
        y = x * lax.rsqrt(jnp.mean(x * x, axis=-1, keepdims=True) + EPS) * g
        partner = jnp.where(first, pltpu.roll(y, hd - hd // 4, 1), pltpu.roll(y, hd // 4, 1))
        return y * cos + partner * sin

    for h in range(ATTN_Q_PER_KV):
        sl = slice(h * hd, (h + 1) * hd)
        qo_ref[0, :, sl] = prep(q_ref[0, :, sl], qg_ref[...]).astype(qo_ref.dtype)
    ko_ref[0] = prep(k_ref[0], kg_ref[...]).astype(ko_ref.dtype)
    vo_ref[0] = v_ref[0].astype(vo_ref.dtype)


def _qk_prep(p3, cos, sin, q_gain, k_gain, col_q, col_k, col_v):
    b, s, _ = p3.shape
    hd = ATTN_HEAD_DIM
    qw = ATTN_Q_PER_KV * hd
    tq = _pick(s, (544, 256, 128))
    return pl.pallas_call(
        _qk_prep_kernel,
        grid=(b, s // tq, ATTN_KV_HEADS),
        in_specs=[
            pl.BlockSpec((1, tq, qw), lambda i, j, kv: (i, j, col_q // qw + kv)),
            pl.BlockSpec((1, tq, hd), lambda i, j, kv: (i, j, col_k // hd + kv)),
            pl.BlockSpec((1, tq, hd), lambda i, j, kv: (i, j, col_v // hd + kv)),
            pl.BlockSpec((tq, hd), lambda i, j, kv: (j, 0)),
            pl.BlockSpec((tq, hd), lambda i, j, kv: (j, 0)),
            pl.BlockSpec((1, hd), lambda i, j, kv: (0, 0)),
            pl.BlockSpec((1, hd), lambda i, j, kv: (0, 0)),
        ],
        out_specs=[
            pl.BlockSpec((1, tq, qw), lambda i, j, kv: (i, j, kv)),
            pl.BlockSpec((1, tq, hd), lambda i, j, kv: (i, j, kv)),
            pl.BlockSpec((1, tq, hd), lambda i, j, kv: (i, j, kv)),
        ],
        out_shape=[
            jax.ShapeDtypeStruct((b, s, ATTN_KV_HEADS * qw), BF16),
            jax.ShapeDtypeStruct((b, s, ATTN_KV_HEADS * hd), BF16),
            jax.ShapeDtypeStruct((b, s, ATTN_KV_HEADS * hd), BF16),
        ],
        compiler_params=_cparams(("parallel", "parallel", "parallel")),
        name="qk_prep",
    )(p3, p3, p3, cos, sin, q_gain.reshape(1, hd), k_gain.reshape(1, hd))


def _attn_kernel(q_ref, k_ref, v_ref, o_ref, *, lc, tq):
    hd = ATTN_HEAD_DIM
    c = (hd ** -0.5) * math.log2(math.e)

    def run(kk, vv):
        hs = ATTN_HEADS_PER_PASS

        def scores(g):
            q = jnp.concatenate([q_ref[0, :, (g * hs + a) * hd:(g * hs + a + 1) * hd] for a in range(hs)], axis=0)
            return lax.dot_general(q, kk, (((1,), (1,)), ((), ())), preferred_element_type=F32)

        v_ext = jnp.concatenate([vv, jnp.ones((vv.shape[0], LANES), BF16)], axis=1)

        n_pass = ATTN_Q_PER_KV // hs
        s_next = scores(0)
        for g in range(n_pass):
            s = s_next
            if g + 1 < n_pass:
                s_next = scores(g + 1)
            m = jnp.max(s, axis=-1, keepdims=True)
            p = jnp.exp2((s - m) * c).astype(BF16)
            oe = jnp.dot(p, v_ext, preferred_element_type=F32)
            o = (oe[:, 0:hd] / oe[:, hd:hd + 1]).astype(o_ref.dtype)
            for a in range(hs):
                o_ref[0, :, (g * hs + a) * hd:(g * hs + a + 1) * hd] = o[a * tq:(a + 1) * tq, :]

    is_ctx_tile = pl.program_id(2) * tq < lc

    @pl.when(jnp.logical_not(is_ctx_tile))
    def _():
        run(k_ref[0], v_ref[0])

    @pl.when(is_ctx_tile)
    def _():
        run(k_ref[0, :lc, :], v_ref[0, :lc, :])


def _attention(qh, kh, vh, lc):
    b, s, _ = qh.shape
    hd = ATTN_HEAD_DIM
    qw = ATTN_Q_PER_KV * hd
    tq = _pick(lc, (256, 128))
    assert s % tq == 0
    return pl.pallas_call(
        functools.partial(_attn_kernel, lc=lc, tq=tq),
        grid=(b, ATTN_KV_HEADS, s // tq),
        in_specs=[
            pl.BlockSpec((1, tq, qw), lambda i, kv, j: (i, j, kv)),
            pl.BlockSpec((1, s, hd), lambda i, kv, j: (i, 0, kv)),
            pl.BlockSpec((1, s, hd), lambda i, kv, j: (i, 0, kv)),
        ],
        out_specs=pl.BlockSpec((1, tq, qw), lambda i, kv, j: (i, j, kv)),
        out_shape=jax.ShapeDtypeStruct((b, s, ATTN_KV_HEADS * qw), BF16),
        compiler_params=_cparams(("parallel", "parallel", "parallel")),
        name="attention",
    )(qh, kh, vh)


def _split3(x):
    hi = x.astype(BF16)
    r = x - hi.astype(F32)
    mid = r.astype(BF16)
    lo = (r - mid.astype(F32)).astype(BF16)
    return hi, mid, lo


def _log_sigmoid(x):
    return jnp.minimum(x, 0.0) - jnp.log(1.0 + jnp.exp(-jnp.abs(x)))


def _mlstm_kernel(q_ref, k_ref, v_ref, gc_ref, gr_ref, bc_ref, br_ref, o_ref, c_scr, n_scr, m_scr):
    nh, dk, dv, cl = MLSTM_HEADS, MLSTM_QK_DIM, MLSTM_V_DIM, MLSTM_CHUNK
    nb = q_ref.shape[0]
    d = pl.program_id(0)
    c = pl.program_id(1)

    @pl.when(c == 0)
    def _():
        c_scr[...] = jnp.zeros_like(c_scr)
        n_scr[...] = jnp.zeros_like(n_scr)
        m_scr[...] = jnp.zeros_like(m_scr)

    row = lax.broadcasted_iota(jnp.int32, (cl, cl), 0)
    col = lax.broadcasted_iota(jnp.int32, (cl, cl), 1)
    sign = jnp.where(d == 0, 1, -1)
    mask = sign * (row - col) >= 0
    tm = jnp.where(mask, 1.0, 0.0).astype(BF16)
    tmt = jnp.where(sign * (col - row) >= 0, 1.0, 0.0).astype(BF16)

    def pick_c(a, k):
        return jnp.where(d == 0, a[:, k:k + 1], a[:, 2 * nh + k:2 * nh + k + 1])

    def pick_r(a, k):
        return jnp.where(d == 0, a[k:k + 1, :], a[2 * nh + k:2 * nh + k + 1, :])

    nt = (((1,), (1,)), ((), ()))
    tn = (((0,), (0,)), ((), ()))
    heads = [(bi, h) for bi in range(nb) for h in range(nh)]

    gcs = [gc_ref[bi] + bc_ref[...] for bi in range(nb)]
    grs = [gr_ref[bi, 0] + br_ref[...] for bi in range(nb)]
    lf_cs = [_log_sigmoid(g) for g in gcs]
    lf_rs = [_log_sigmoid(g) for g in grs]
    cum_cs = [sum(jnp.dot(tm, part, preferred_element_type=F32) for part in _split3(x)) for x in lf_cs]
    cum_rs = [sum(jnp.dot(part, tmt, preferred_element_type=F32) for part in _split3(x)) for x in lf_rs]
    totals = [jnp.sum(x, axis=0, keepdims=True) for x in lf_cs]

    qs = [q_ref[bi][:, h * dk:(h + 1) * dk] * (dk ** -0.5) for bi, h in heads]
    ks = [k_ref[bi][:, h * dk:(h + 1) * dk] for bi, h in heads]
    vs = [v_ref[bi][:, h * dv:(h + 1) * dv] for bi, h in heads]
    qbs = [x.astype(BF16) for x in qs]
    kbs = [x.astype(BF16) for x in ks]
    m_sts = [m_scr[i][:, 0:1] for i in range(len(heads))]
    c_sts = [c_scr[i] for i in range(len(heads))]
    n_sts = [n_scr[i] for i in range(len(heads))]

    s_raw = [lax.dot_general(qbs[i], kbs[i], nt, preferred_element_type=F32) for i in range(len(heads))]
    cqs = [lax.dot_general(qbs[i], c_sts[i].astype(BF16), nt, preferred_element_type=F32)
           for i in range(len(heads))]

    cf_c = [pick_c(cum_cs[bi], nh + h) for bi, h in heads]
    cf_r = [pick_r(cum_rs[bi], nh + h) for bi, h in heads]
    li_c = [pick_c(gcs[bi], h) for bi, h in heads]
    li_r = [pick_r(grs[bi], h) for bi, h in heads]
    tot = [pick_c(totals[bi], nh + h) for bi, h in heads]

    log_d = [jnp.where(mask, cf_c[i] - cf_r[i] + li_r[i], -jnp.inf) for i in range(len(heads))]
    log_inter = [cf_c[i] + m_sts[i] for i in range(len(heads))]
    m_row = [jnp.maximum(log_inter[i], jnp.max(log_d[i], axis=-1, keepdims=True)) for i in range(len(heads))]
    s = [s_raw[i] * jnp.exp(log_d[i] - m_row[i]) for i in range(len(heads))]
    w_inter = [jnp.exp(log_inter[i] - m_row[i]) for i in range(len(heads))]
    sv = [jnp.dot(s[i].astype(BF16), vs[i].astype(BF16), preferred_element_type=F32) for i in range(len(heads))]
    den = [jnp.sum(s[i], axis=-1, keepdims=True)
           + w_inter[i] * jnp.sum(qs[i] * n_sts[i], axis=-1, keepdims=True) for i in range(len(heads))]
    for i, (bi, h) in enumerate(heads):
        num = sv[i] + w_inter[i] * cqs[i]
        o_ref[0, bi, :, h * dv:(h + 1) * dv] = num / jnp.maximum(jnp.abs(den[i]), jnp.exp(-m_row[i]))

    log_w = [tot[i] - cf_c[i] + li_c[i] for i in range(len(heads))]
    m_new = [jnp.maximum(tot[i] + m_sts[i], jnp.max(log_w[i], axis=0, keepdims=True)) for i in range(len(heads))]
    decay = [jnp.exp(tot[i] + m_sts[i] - m_new[i]) for i in range(len(heads))]
    w = [jnp.exp(log_w[i] - m_new[i]) for i in range(len(heads))]
    kv = [lax.dot_general((w[i] * vs[i]).astype(BF16), kbs[i], tn, preferred_element_type=F32)
          for i in range(len(heads))]
    for i in range(len(heads)):
        c_scr[i] = decay[i] * c_sts[i] + kv[i]
        n_scr[i] = decay[i] * n_sts[i] + jnp.sum(w[i] * ks[i], axis=0, keepdims=True)
        m_scr[i] = jnp.broadcast_to(m_new[i], (1, LANES))


def _mlstm(p3, gate_bias, lc, col_q, col_k, col_v, col_g):
    b, s, _ = p3.shape
    nh, dk, dv, cl = MLSTM_HEADS, MLSTM_QK_DIM, MLSTM_V_DIM, MLSTM_CHUNK
    nc = s // cl
    nctx = lc // cl
    ng = 4 * nh
    g_rows = jnp.transpose(p3[:, :, col_g:col_g + ng].reshape(b, nc, cl, ng), (0, 1, 3, 2))
    bias = gate_bias.astype(F32).reshape(ng)

    def chunk(d, c):
        bwd = jnp.where(c < nctx, nctx - 1 - c, nc - 1 - (c - nctx))
        return jnp.where(d == 0, c, bwd)

    return pl.pallas_call(
        _mlstm_kernel,
        grid=(2, nc),
        in_specs=[
            pl.BlockSpec((b, cl, nh * dk), lambda d, c: (0, chunk(d, c), col_q // (nh * dk))),
            pl.BlockSpec((b, cl, nh * dk), lambda d, c: (0, chunk(d, c), col_k // (nh * dk))),
            pl.BlockSpec((b, cl, nh * dv), lambda d, c: (0, chunk(d, c), col_v // (nh * dv))),
            pl.BlockSpec((b, cl, LANES), lambda d, c: (0, chunk(d, c), col_g // LANES)),
            pl.BlockSpec((b, 1, ng, cl), lambda d, c: (0, chunk(d, c), 0, 0)),
            pl.BlockSpec((1, LANES), lambda d, c: (0, 0)),
            pl.BlockSpec((ng, 1), lambda d, c: (0, 0)),
        ],
        out_specs=pl.BlockSpec((1, b, cl, nh * dv), lambda d, c: (d, 0, chunk(d, c), 0)),
        out_shape=jax.ShapeDtypeStruct((2, b, s, nh * dv), F32),
        scratch_shapes=[pltpu.VMEM((b * nh, dv, dk), F32), pltpu.VMEM((b * nh, 1, dk), F32),
                        pltpu.VMEM((b * nh, 1, LANES), F32)],
        compiler_params=_cparams(("arbitrary", "arbitrary")),
        name="mlstm",
    )(p3, p3, p3, p3, g_rows, jnp.pad(bias, (0, LANES - ng)).reshape(1, LANES), bias.reshape(ng, 1))


def _gelu_tanh(x):
    return 0.5 * x * (1.0 + jnp.tanh(math.sqrt(2.0 / math.pi) * (x + 0.044715 * (x * x * x))))


def _merge_kernel(ys_ref, att_ref, hf_ref, hb_ref, og_ref, gl_ref, wglu_ref, bglu_ref, ng_ref,
                  wbs_ref, wba_ref, wbm_ref, y_ref):
    d = y_ref.shape[-1]
    z = _gelu_tanh(ys_ref[...])
    s5 = z * jax.nn.sigmoid(jnp.dot(z.astype(BF16), wglu_ref[...], preferred_element_type=F32) + bglu_ref[...])
    hsum = hf_ref[0] + hb_ref[0]
    dv = MLSTM_V_DIM
    parts = []
    for h in range(MLSTM_HEADS):
        blk = hsum[:, h * dv:(h + 1) * dv]
        parts.append(blk * lax.rsqrt(jnp.mean(blk * blk, axis=-1, keepdims=True) + EPS))
    ml = jnp.concatenate(parts, axis=-1) * ng_ref[...] * jax.nn.sigmoid(og_ref[...])
    y = gl_ref[:, 0:d].astype(F32) * jnp.dot(s5.astype(BF16), wbs_ref[...], preferred_element_type=F32)
    y = y + gl_ref[:, d:2 * d].astype(F32) * jnp.dot(att_ref[...], wba_ref[...], preferred_element_type=F32)
    y = y + gl_ref[:, 2 * d:3 * d].astype(F32) * jnp.dot(ml.astype(BF16), wbm_ref[...], preferred_element_type=F32)
    y_ref[...] = y.astype(y_ref.dtype)


def _merge(ys, att, hfb, p_main, gates, wglu, bglu, ng, wbs, wba, wbm, col_o):
    t, ws = ys.shape
    wa = att.shape[1]
    wm = hfb.shape[-1]
    d = wbs.shape[1]
    tm = _pick(t, (256, 128))
    full = lambda shape: pl.BlockSpec(shape, lambda i: (0,) * len(shape))
    return pl.pallas_call(
        _merge_kernel,
        grid=(t // tm,),
        in_specs=[
            pl.BlockSpec((tm, ws), lambda i: (i, 0)),
            pl.BlockSpec((tm, wa), lambda i: (i, 0)),
            pl.BlockSpec((1, tm, wm), lambda i: (0, i, 0)),
            pl.BlockSpec((1, tm, wm), lambda i: (1, i, 0)),
            pl.BlockSpec((tm, wm), lambda i: (i, col_o // wm)),
            pl.BlockSpec((tm, 3 * d), lambda i: (i, 0)),
            full((ws, ws)), full((1, ws)), full((1, wm)),
            full((ws, d)), full((wa, d)), full((wm, d)),
        ],
        out_specs=pl.BlockSpec((tm, d), lambda i: (i, 0)),
        out_shape=jax.ShapeDtypeStruct((t, d), BF16),
        compiler_params=_cparams(("parallel",)),
        name="branch_merge",
    )(ys, att, hfb, hfb, p_main, gates, wglu, bglu.reshape(1, ws), ng.reshape(1, wm), wbs, wba, wbm)


def _out_proj_kernel(y_ref, w_ref, x_ref, mod_ref, o_ref, *, lc, tm):
    out = jnp.dot(y_ref[0], w_ref[...], preferred_element_type=F32)
    gate = _row_select(mod_ref, 2, _is_ctx_rows(tm, lc))
    o_ref[0] = x_ref[0] + gate * out


def _out_proj(y3, w_out, xa, mod, lc):
    b, s, d = xa.shape
    tm = _pick(s, (544, 256, 128))
    return pl.pallas_call(
        functools.partial(_out_proj_kernel, lc=lc, tm=tm),
        grid=(b, s // tm),
        in_specs=[
            pl.BlockSpec((1, tm, d), lambda i, j: (i, j, 0)),
            pl.BlockSpec((d, d), lambda i, j: (0, 0)),
            pl.BlockSpec((1, tm, d), lambda i, j: (i, j, 0)),
            pl.BlockSpec((1, 2, 6, d), lambda i, j: (i, 0, 0, 0)),
        ],
        out_specs=pl.BlockSpec((1, tm, d), lambda i, j: (i, j, 0)),
        out_shape=jax.ShapeDtypeStruct((b, s, d), F32),
        compiler_params=_cparams(("parallel", "parallel")),
        name="out_proj_residual",
    )(y3, w_out, xa, mod)


def _norm_router_kernel(x_ref, g_ref, mod_ref, wr_ref, br_ref, h_ref, idx_ref, w_ref, *, lc, tm):
    _norm_and_route(x_ref[0], g_ref, mod_ref, wr_ref, br_ref, h_ref, idx_ref, w_ref, _is_ctx_rows(tm, lc))


def _norm_router(xa, g, mod, w_router, b_router, lc):
    b, s, d = xa.shape
    tm = _pick(s, (544, 256, 128))
    ne = w_router.shape[1]
    return pl.pallas_call(
        functools.partial(_norm_router_kernel, lc=lc, tm=tm),
        grid=(b, s // tm),
        in_specs=[
            pl.BlockSpec((1, tm, d), lambda i, j: (i, j, 0)),
            pl.BlockSpec((1, d), lambda i, j: (0, 0)),
            pl.BlockSpec((1, 2, 6, d), lambda i, j: (i, 0, 0, 0)),
            pl.BlockSpec((d, ne), lambda i, j: (0, 0)),
            pl.BlockSpec((1, ne), lambda i, j: (0, 0)),
        ],
        out_specs=[
            pl.BlockSpec((1, tm, d // LANES, LANES), lambda i, j: (i, j, 0, 0)),
            pl.BlockSpec((1, tm, TOP_K), lambda i, j: (i, j, 0)),
            pl.BlockSpec((1, tm, TOP_K), lambda i, j: (i, j, 0)),
        ],
        out_shape=[
            jax.ShapeDtypeStruct((b, s, d // LANES, LANES), F32),
            jax.ShapeDtypeStruct((b, s, TOP_K), jnp.int32),
            jax.ShapeDtypeStruct((b, s, TOP_K), F32),
        ],
        compiler_params=_cparams(("parallel", "parallel")),
        name="norm_router",
    )(xa, g.reshape(1, d), mod, w_router, b_router.reshape(1, ne))


def _pair_split_matrix():
    half = MXU_COLS // 2
    n = jnp.arange(MXU_COLS)
    src = jnp.where(n < half, 2 * n, 2 * (n - half) + 1)
    return (jnp.arange(MXU_COLS)[:, None] == src[None, :]).astype(BF16)


def _pair_split_index(width):
    half = MXU_COLS // 2
    n = jnp.arange(width)
    blk, r = n // MXU_COLS, n % MXU_COLS
    return blk * MXU_COLS + jnp.where(r < half, 2 * r, 2 * (r - half) + 1)


def _split_pairs_kernel(w_ref, p_ref, after_ref, o_ref):
    del after_ref
    for cb in range(w_ref.shape[-1] // MXU_COLS):
        sl = slice(cb * MXU_COLS, (cb + 1) * MXU_COLS)
        o_ref[0, :, sl] = jnp.dot(w_ref[0, :, sl].astype(BF16), p_ref[...],
                                  preferred_element_type=F32).astype(o_ref.dtype)


def _split_pairs(w_all, first, count, after):
    _, d, n = w_all.shape
    tc = _pick(n, (2 * MXU_COLS, MXU_COLS))
    return pl.pallas_call(
        _split_pairs_kernel,
        grid=(count, n // tc),
        in_specs=[
            pl.BlockSpec((1, d, tc), lambda e, i: (first + e, 0, i)),
            pl.BlockSpec((MXU_COLS, MXU_COLS), lambda e, i: (0, 0)),
            pl.BlockSpec((8, LANES), lambda e, i: (0, 0)),
        ],
        out_specs=pl.BlockSpec((1, d, tc), lambda e, i: (e, 0, i)),
        out_shape=jax.ShapeDtypeStruct((count, d, n), BF16),
        compiler_params=_cparams(("parallel", "parallel")),
        name="split_gate_linear",
    )(w_all, _pair_split_matrix(), after)


def _expert_kernel(te_ref, nu_ref, tok_ref, tok_next_ref, h_hbm, *refs):
    npart = EXPERT_WEIGHT_PARTS
    wgl_refs, bgl_ref = refs[0:npart], refs[npart]
    wd_refs, bd_ref = refs[npart + 1:2 * npart + 1], refs[2 * npart + 1]
    o_ref, wd_scr, xbuf, sem = refs[2 * npart + 2:2 * npart + 6]
    i = pl.program_id(0)
    half = MXU_COLS // 2
    tm = o_ref.shape[0]
    nseg = h_hbm.shape[1]
    n_used = nu_ref[0]
    used = i < n_used
    slot = lax.rem(i, 2)
    new_expert = jnp.logical_or(i == 0, te_ref[i] != te_ref[jnp.maximum(i - 1, 0)])
    wc = wd_refs[0].shape[2]

    def start_rows(idx_ref, dst_slot):
        for r in range(tm):
            dst = xbuf.at[pl.ds((dst_slot * tm + r) * nseg, nseg)]
            pltpu.make_async_copy(h_hbm.at[idx_ref[0, 0, r]], dst, sem.at[dst_slot]).start()

    def wait_rows(dst_slot):
        dst = xbuf.at[pl.ds(dst_slot * tm * nseg, tm * nseg)]
        pltpu.make_async_copy(dst, dst, sem.at[dst_slot]).wait()

    @pl.when(i == 0)
    def _():
        start_rows(tok_ref, 0)

    @pl.when(jnp.logical_and(used, new_expert))
    def _():
        for part in range(npart):
            wd_scr[:, part * wc:(part + 1) * wc] = wd_refs[part][0].astype(BF16)

    @pl.when(used)
    def _():
        start_rows(tok_next_ref, 1 - slot)
        wait_rows(slot)

    @pl.when(used)
    def _():
        base = slot * (tm * nseg)
        x = jnp.concatenate([xbuf[pl.ds(base + j, tm, stride=nseg), :] for j in range(nseg)],
                            axis=1).astype(BF16)
        gu = jnp.concatenate([jnp.dot(x, w[0], preferred_element_type=F32) for w in wgl_refs], axis=1)
        gu = gu + bgl_ref[0]
        nblk = gu.shape[1] // MXU_COLS
        gate = jnp.concatenate([gu[:, cb * MXU_COLS:cb * MXU_COLS + half] for cb in range(nblk)], axis=1)
        lin = jnp.concatenate([gu[:, cb * MXU_COLS + half:(cb + 1) * MXU_COLS] for cb in range(nblk)], axis=1)
        gate = jnp.minimum(gate, SWIGLU_LIMIT)
        lin = jnp.clip(lin, -SWIGLU_LIMIT, SWIGLU_LIMIT)
        act = gate * jax.nn.sigmoid(SWIGLU_ALPHA * gate) * (lin + 1.0)
        y = jnp.dot(act.astype(BF16), wd_scr[...], preferred_element_type=F32) + bd_ref[0]
        o_ref[...] = y.astype(o_ref.dtype)

    @pl.when(i == n_used - 1)
    def _():
        wait_rows(1 - slot)

    @pl.when(jnp.logical_not(used))
    def _():
        o_ref[...] = jnp.zeros_like(o_ref)


def _experts(h_rows, slot_tok, tile_expert, n_used, wgl, bgl, wd, bd, first):
    _, nseg, lanes = h_rows.shape
    d = nseg * lanes
    n_tiles, tm = slot_tok.shape
    _, _, ff2 = wgl.shape
    ff = ff2 // 2
    npart = EXPERT_WEIGHT_PARTS
    col_part = lambda part: (lambda i, te, nu: (te[i], 0, part))
    wgl_part = lambda part: (lambda i, te, nu: (te[i] - first, 0, part))
    grid_spec = pltpu.PrefetchScalarGridSpec(
        num_scalar_prefetch=2,
        grid=(n_tiles,),
        in_specs=(
            [pl.BlockSpec((1, 1, tm), lambda i, te, nu: (i, 0, 0), memory_space=pltpu.SMEM),
             pl.BlockSpec((1, 1, tm), lambda i, te, nu: (jnp.minimum(i + 1, n_tiles - 1), 0, 0),
                          memory_space=pltpu.SMEM),
             pl.BlockSpec(memory_space=pl.ANY)]
            + [pl.BlockSpec((1, d, ff2 // npart), wgl_part(part)) for part in range(npart)]
            + [pl.BlockSpec((1, 1, ff2), lambda i, te, nu: (te[i], 0, 0))]
            + [pl.BlockSpec((1, ff, d // npart), col_part(part)) for part in range(npart)]
            + [pl.BlockSpec((1, 1, d), lambda i, te, nu: (te[i], 0, 0))]),
        out_specs=pl.BlockSpec((tm, d), lambda i, te, nu: (i, 0)),
        scratch_shapes=[pltpu.VMEM((ff, d), BF16),
                        pltpu.VMEM((2 * tm * nseg, lanes), F32),
                        pltpu.SemaphoreType.DMA((2,))],
    )
    tok3 = slot_tok[:, None, :]
    return pl.pallas_call(
        _expert_kernel,
        grid_spec=grid_spec,
        out_shape=jax.ShapeDtypeStruct((n_tiles * tm, d), BF16),
        compiler_params=_cparams(("arbitrary",)),
        name="experts",
    )(tile_expert, n_used, tok3, tok3, h_rows, *([wgl] * npart), bgl, *([wd] * npart), bd)


def _route(top_idx, n_slots):
    tm = EXPERT_TILE
    e = top_idx.reshape(-1)
    n = e.shape[0]
    onehot = (e[:, None] == jnp.arange(N_EXPERTS, dtype=jnp.int32)[None, :]).astype(jnp.int32)
    counts = jnp.sum(onehot, axis=0)
    rank = jnp.take_along_axis(jnp.cumsum(onehot, axis=0) - onehot, e[:, None], axis=1)[:, 0]
    padded = ((counts + tm - 1) // tm) * tm
    ends = jnp.cumsum(padded)
    dest = (ends - padded)[e] + rank
    n_used = (ends[-1] // tm).astype(jnp.int32)
    tile_start = jnp.arange(n_slots // tm, dtype=jnp.int32) * tm
    tile_expert = jnp.sum((ends[None, :] <= tile_start[:, None]).astype(jnp.int32), axis=1)
    last_used = jnp.sum((ends <= (n_used - 1) * tm).astype(jnp.int32))
    tile_expert = jnp.minimum(jnp.where(tile_start < ends[-1], tile_expert, last_used), N_EXPERTS - 1)
    order = jnp.argsort(e, stable=True).astype(jnp.int32)
    tile_rank0 = tile_start - (ends - padded)[tile_expert]
    tile_count = jnp.where(tile_start < ends[-1], counts[tile_expert], 0)
    tile_sorted0 = (jnp.cumsum(counts) - counts)[tile_expert] + tile_rank0
    lane = jnp.arange(tm, dtype=jnp.int32)[None, :]
    valid = tile_rank0[:, None] + lane < tile_count[:, None]
    sorted_pos = jnp.clip(tile_sorted0[:, None] + lane, 0, n - 1)
    src = jnp.where(valid, order[sorted_pos] // TOP_K, 0)
    return src, dest, tile_expert.astype(jnp.int32), n_used.reshape(1)


def _combine_kernel(y0_ref, y1_ref, y2_ref, y3_ref, tw_ref, x_ref, mod_ref, fg_ref, o_ref, *, lc, tm, final):
    tw = tw_ref[0]
    moe = tw[:, 0:1] * y0_ref[0, 0].astype(F32)
    for k, y_ref in enumerate((y1_ref, y2_ref, y3_ref), start=1):
        moe = moe + tw[:, k:k + 1] * y_ref[0, 0].astype(F32)
    if final:
        xn = x_ref[0] + mod_ref[0, 1, 5:6, :] * moe
        xn = xn * lax.rsqrt(jnp.mean(xn * xn, axis=-1, keepdims=True) + EPS) * fg_ref[...]
    else:
        xn = x_ref[0] + _row_select(mod_ref, 5, _is_ctx_rows(tm, lc)) * moe
    o_ref[0] = xn


def _combine(yg, top_w, xa, mod, final_g, lc, final):
    b, s, d = xa.shape
    assert TOP_K == 4
    if final:
        tm = _pick(lc, (256, 128))
        assert (s - lc) % tm == 0
        off, rows = lc // tm, s - lc
    else:
        tm = _pick(s, (544, 256, 128))
        off, rows = 0, s
    y_specs = [pl.BlockSpec((1, 1, tm, d), functools.partial(lambda i, j, k: (k, i, j + off, 0), k=k))
               for k in range(TOP_K)]
    return pl.pallas_call(
        functools.partial(_combine_kernel, lc=lc, tm=tm, final=final),
        grid=(b, rows // tm),
        in_specs=y_specs + [
            pl.BlockSpec((1, tm, TOP_K), lambda i, j: (i, j + off, 0)),
            pl.BlockSpec((1, tm, d), lambda i, j: (i, j + off, 0)),
            pl.BlockSpec((1, 2, 6, d), lambda i, j: (i, 0, 0, 0)),
            pl.BlockSpec((1, d), lambda i, j: (0, 0)),
        ],
        out_specs=pl.BlockSpec((1, tm, d), lambda i, j: (i, j, 0)),
        out_shape=jax.ShapeDtypeStruct((b, rows, d), F32),
        compiler_params=_cparams(("parallel", "parallel")),
        name="moe_combine",
    )(yg, yg, yg, yg, top_w, xa, mod, final_g.reshape(1, d))


def kernel(x, c, ctx, c_ctx, w_ada, b_ada, norm_g, w_in, ssm_a_re, ssm_a_im, ssm_log_dt, ssm_b_re, ssm_b_im, ssm_c_re, ssm_c_im, ssm_d, ssm_w_glu, ssm_b_glu, attn_q_gain, attn_k_gain, mlstm_gate_bias, mlstm_norm_g, w_branch_ssm, w_branch_attn, w_branch_mlstm, w_out, w_router, b_router, w_gate_up, b_gate_up, w_down, b_down, final_g):
    b, lx, d = x.shape
    lc = ctx.shape[1]
    s = lc + lx
    t = b * s
    depth = w_in.shape[0]
    assert b + 1 <= 8 and lc % MLSTM_CHUNK == 0 and lx % MLSTM_CHUNK == 0

    ssm_w = ssm_d.shape[1]
    attn_w = w_branch_attn.shape[1]
    kv_w = ATTN_KV_HEADS * ATTN_HEAD_DIM
    mqk_w = MLSTM_HEADS * MLSTM_QK_DIM
    mv_w = MLSTM_HEADS * MLSTM_V_DIM
    n_gates = 4 * MLSTM_HEADS
    col_q = ssm_w
    col_k = col_q + attn_w
    col_v = col_k + kv_w
    col_mq = col_v + kv_w
    col_mk = col_mq + mqk_w
    col_mv = col_mk + mqk_w
    col_mo = col_mv + mv_w
    col_mg = col_mo + mv_w
    col_bg = col_mg + n_gates
    main_pad = (-(col_mg + n_gates)) % 256

    xa = jnp.concatenate([ctx, x], axis=1)
    cond8 = jnp.zeros((8, d), F32).at[0].set(c_ctx).at[1:1 + b].set(c)
    mods = _ada_modulation(cond8, w_ada, b_ada).reshape(depth, 8, 6, d)
    cos, sin = _rope_tables(lc, lx)
    ne, ff2 = w_gate_up.shape[1], w_gate_up.shape[3]
    after = jnp.zeros((8, LANES), F32)
    bgl_all = b_gate_up.reshape(depth * ne, ff2)[:, _pair_split_index(ff2)][:, None, :]
    wd_all = w_down.reshape(depth * ne, ff2 // 2, d)
    bd_all = b_down.reshape(depth * ne, 1, d)

    for l in range(depth):
        last = l == depth - 1
        mod = jnp.stack([jnp.broadcast_to(mods[l, 0], (b, 6, d)), mods[l, 1:1 + b]], axis=1)

        h = _norm_mod(xa, norm_g[l, 0], mod, lc, 0).reshape(t, d)
        w_main = jnp.concatenate(
            [w_in[l, :, :col_bg], jnp.zeros((d, main_pad), F32)], axis=1).astype(BF16)
        p_main = _gemm(h, w_main, F32)
        gates = _gemm(h, w_in[l, :, col_bg:].astype(BF16), BF16, act="sigmoid")
        p3 = p_main.reshape(b, s, -1)

        ops = _s5_operators(ssm_a_re[l], ssm_a_im[l], ssm_log_dt[l], ssm_b_re[l], ssm_b_im[l],
                            ssm_c_re[l], ssm_c_im[l], ssm_d[l])
        ys = _s5_mix(p3, ops, ssm_w, lc)

        qh, kh, vh = _qk_prep(p3, cos, sin, attn_q_gain[l], attn_k_gain[l], col_q, col_k, col_v)
        att = _attention(qh, kh, vh, lc)

        hfb = _mlstm(p3, mlstm_gate_bias[l], lc, col_mq, col_mk, col_mv, col_mg)

        y = _merge(ys.reshape(t, ssm_w), att.reshape(t, attn_w), hfb.reshape(2, t, mv_w), p_main, gates,
                   ssm_w_glu[l].astype(BF16), ssm_b_glu[l], mlstm_norm_g[l],
                   w_branch_ssm[l].astype(BF16), w_branch_attn[l].astype(BF16),
                   w_branch_mlstm[l].astype(BF16), col_mo)
        xa = _out_proj(y.reshape(b, s, d), w_out[l].astype(BF16), xa, mod, lc)

        h2, top_idx, top_w = _norm_router(xa, norm_g[l, 1], mod, w_router[l], b_router[l], lc)
        n_slots = t * TOP_K + N_EXPERTS * EXPERT_TILE
        src, dest, tile_expert, n_used = _route(top_idx, n_slots)
        wgl = _split_pairs(w_gate_up.reshape(depth * ne, d, ff2), l * ne, ne, after)
        ysorted = _experts(h2.reshape(t, d // LANES, LANES), src.reshape(-1, EXPERT_TILE),
                           tile_expert + l * ne, n_used, wgl, bgl_all, wd_all, bd_all, l * ne)
        after = ysorted[:8, :LANES].astype(F32)
        yg = ysorted.at[dest.reshape(t, TOP_K).T].get(mode="promise_in_bounds").reshape(TOP_K, b, s, d)
        xa = _combine(yg, top_w, xa, mod, final_g, lc, last)

    return xa
```

```python
import functools
import math

import jax
import jax.numpy as jnp
from jax import lax
from jax.experimental import pallas as pl
from jax.experimental.pallas import tpu as pltpu

F32 = jnp.float32
BF16 = jnp.bfloat16

GRID_W = 64
EPS = 1e-6
SSM_GROUP = 16
SSM_STATE = 64
SSM_CHUNK = 8
SSM_PACK = 8
ATTN_HEAD_DIM = 128
ATTN_KV_HEADS = 2
ATTN_Q_PER_KV = 4
ATTN_HEADS_PER_PASS = 2
ROPE_THETA = 10000.0
MLSTM_HEADS = 4
MLSTM_QK_DIM = 64
MLSTM_V_DIM = 128
MLSTM_CHUNK = 64
N_EXPERTS = 32
TOP_K = 4
SWIGLU_LIMIT = 7.0
SWIGLU_ALPHA = 1.702

LANES = 128
MXU_COLS = 256
VMEM_LIMIT_BYTES = 56 * 1024 * 1024
EXPERT_TILE = 256
EXPERT_WEIGHT_PARTS = 4

def _cparams(sem):
    return pltpu.CompilerParams(dimension_semantics=sem, vmem_limit_bytes=VMEM_LIMIT_BYTES)


def _pick(n, cands):
    for c in cands:
        if n % c == 0:
            return c
    raise ValueError(f"no tile for {n} in {cands}")


def _ada_kernel(c_ref, w_ref, b_ref, o_ref):
    c = c_ref[...]
    s = c * jax.nn.sigmoid(c)
    o_ref[0] = jnp.dot(s, w_ref[0], preferred_element_type=F32) + b_ref[0]


def _ada_modulation(cond8, w_ada, b_ada):
    nl, d, n = w_ada.shape
    tn = _pick(n, (1024, 512, 256, 128))
    return pl.pallas_call(
        _ada_kernel,
        grid=(nl, n // tn),
        in_specs=[
            pl.BlockSpec((8, d), lambda l, j: (0, 0)),
            pl.BlockSpec((1, d, tn), lambda l, j: (l, 0, j)),
            pl.BlockSpec((1, 1, tn), lambda l, j: (l, 0, j)),
        ],
        out_specs=pl.BlockSpec((1, 8, tn), lambda l, j: (l, 0, j)),
        out_shape=jax.ShapeDtypeStruct((nl, 8, n), F32),
        compiler_params=_cparams(("parallel", "parallel")),
        name="ada_modulation",
    )(cond8, w_ada, b_ada.reshape(nl, 1, n))


def _row_select(mod_ref, k, is_ctx):
    return jnp.where(is_ctx, mod_ref[0, 0, k:k + 1, :], mod_ref[0, 1, k:k + 1, :])


def _modulated_norm(x, g, mod_ref, k_shift, is_ctx):
    ms = jnp.mean(x * x, axis=-1, keepdims=True)
    y = x * lax.rsqrt(ms + EPS) * g
    return y * (1.0 + _row_select(mod_ref, k_shift + 1, is_ctx)) + _row_select(mod_ref, k_shift, is_ctx)


def _is_ctx_rows(tm, lc):
    pos = pl.program_id(1) * tm + lax.broadcasted_iota(jnp.int32, (tm, 1), 0)
    return pos < lc


def _norm_mod_kernel(x_ref, g_ref, mod_ref, o_ref, *, lc, tm, k_shift):
    h = _modulated_norm(x_ref[0], g_ref[...], mod_ref, k_shift, _is_ctx_rows(tm, lc))
    o_ref[0] = h.astype(o_ref.dtype)


def _norm_mod(xa, g, mod, lc, k_shift):
    b, s, d = xa.shape
    tm = _pick(s, (544, 256, 128))
    return pl.pallas_call(
        functools.partial(_norm_mod_kernel, lc=lc, tm=tm, k_shift=k_shift),
        grid=(b, s // tm),
        in_specs=[
            pl.BlockSpec((1, tm, d), lambda i, j: (i, j, 0)),
            pl.BlockSpec((1, d), lambda i, j: (0, 0)),
            pl.BlockSpec((1, 2, 6, d), lambda i, j: (i, 0, 0, 0)),
        ],
        out_specs=pl.BlockSpec((1, tm, d), lambda i, j: (i, j, 0)),
        out_shape=jax.ShapeDtypeStruct((b, s, d), BF16),
        compiler_params=_cparams(("parallel", "parallel")),
        name="norm_mod",
    )(xa, g.reshape(1, d), mod)


def _norm_and_route(x, g_ref, mod_ref, wr_ref, br_ref, h_ref, idx_ref, w_ref, is_ctx):
    h = _modulated_norm(x, g_ref[...], mod_ref, 3, is_ctx)
    for j in range(h.shape[1] // LANES):
        h_ref[0, :, j, :] = h[:, j * LANES:(j + 1) * LANES]
    logits = jnp.dot(h, wr_ref[...], preferred_element_type=F32,
                     precision=lax.Precision.HIGHEST) + br_ref[...]
    lane = lax.broadcasted_iota(jnp.int32, logits.shape, 1).astype(F32)
    vals, idxs = [], []
    cur = logits
    for _ in range(TOP_K):
        mx = jnp.max(cur, axis=-1, keepdims=True)
        ix = jnp.min(jnp.where(cur == mx, lane, float(N_EXPERTS)), axis=-1, keepdims=True)
        vals.append(mx)
        idxs.append(ix)
        cur = jnp.where(lane == ix, -jnp.inf, cur)
    k_iota = lax.broadcasted_iota(jnp.int32, (logits.shape[0], TOP_K), 1)
    top = jnp.zeros((logits.shape[0], TOP_K), F32)
    top_i = jnp.zeros((logits.shape[0], TOP_K), F32)
    for k in range(TOP_K):
        top = jnp.where(k_iota == k, vals[k], top)
        top_i = jnp.where(k_iota == k, idxs[k], top_i)
    e = jnp.exp(top - vals[0])
    w_ref[0] = e / jnp.sum(e, axis=-1, keepdims=True)
    idx_ref[0] = top_i.astype(jnp.int32)


def _gemm_kernel(a_ref, w_ref, o_ref, *, act):
    acc = jnp.dot(a_ref[...], w_ref[...], preferred_element_type=F32)
    if act == "sigmoid":
        acc = jax.nn.sigmoid(acc)
    o_ref[...] = acc.astype(o_ref.dtype)


def _gemm(a, w, out_dtype, act=None, tn_cands=(1920, 1024, 512, 256, 128)):
    t, k = a.shape
    n = w.shape[1]
    tm = _pick(t, (1024, 768, 512, 256, 128))
    tn = _pick(n, tn_cands)
    return pl.pallas_call(
        functools.partial(_gemm_kernel, act=act),
        grid=(t // tm, n // tn),
        in_specs=[
            pl.BlockSpec((tm, k), lambda i, j: (i, 0)),
            pl.BlockSpec((k, tn), lambda i, j: (0, j)),
        ],
        out_specs=pl.BlockSpec((tm, tn), lambda i, j: (i, j)),
        out_shape=jax.ShapeDtypeStruct((t, n), out_dtype),
        compiler_params=_cparams(("parallel", "parallel")),
        name="gemm_" + (act or "plain"),
    )(a, w)


def _s5_operators(a_re, a_im, log_dt, b_re, b_im, c_re, c_im, d_skip):
    tc = SSM_CHUNK
    hp = lax.Precision.HIGHEST
    gs = SSM_PACK
    nsg = a_re.shape[1] // gs
    eye = jnp.eye(gs, dtype=F32)
    k_idx = jnp.arange(tc + 1, dtype=F32)
    lag = jnp.arange(tc)[None, :] - jnp.arange(tc)[:, None]
    ms, bcs, ccs, a16s = [], [], [], []
    for dirn in (0, 1):
        lam = lax.complex(a_re[dirn].astype(F32), a_im[dirn].astype(F32))
        dt = jnp.exp(log_dt[dirn].astype(F32))[:, None]
        pw = jnp.exp((lam * dt)[None] * k_idx[:, None, None])
        a_bar = pw[1]
        b_bar = ((a_bar - 1.0) / lam)[..., None] * lax.complex(b_re[dirn].astype(F32), b_im[dirn].astype(F32))
        c_mat = lax.complex(c_re[dirn].astype(F32), c_im[dirn].astype(F32))
        k_lag = jnp.real(jnp.einsum("ghp,kgp,gpj->kghj", c_mat, pw[:tc], b_bar, precision=hp))
        if dirn == 0:
            dist, valid = lag, lag >= 0
            end_pow = (tc - 1) - jnp.arange(tc)
            out_pow = jnp.arange(tc) + 1
        else:
            dist, valid = -lag, lag <= 0
            end_pow = jnp.arange(tc)
            out_pow = tc - jnp.arange(tc)
        kt = jnp.transpose(k_lag, (0, 1, 3, 2)).reshape(tc, nsg, gs, SSM_GROUP, SSM_GROUP)
        bd = (kt[:, :, :, :, None, :] * eye[None, None, :, None, :, None]).reshape(tc, nsg, LANES, LANES)
        blocks = jnp.where(valid[:, :, None, None, None], bd[jnp.clip(dist, 0, tc - 1)], 0.0)
        ms.append(jnp.transpose(blocks, (2, 0, 3, 1, 4)).reshape(nsg, tc * LANES, tc * LANES))
        bc = pw[end_pow][:, :, :, None] * b_bar[None]
        bcs.append(jnp.transpose(bc, (1, 0, 3, 2)))
        cw = c_mat[None] * pw[out_pow][:, :, None, :]
        ccs.append(jnp.transpose(cw, (1, 3, 0, 2)))
        a16s.append(pw[tc])

    def pack_rows(x):
        x = x.reshape((nsg, gs) + x.shape[1:])
        x = jnp.transpose(x, (0, 2, 3, 1, 4))[:, :, None]
        x = x * eye[None, None, :, None, :, None]
        return x.reshape(nsg, tc * gs * SSM_GROUP, gs * SSM_STATE)

    def pack_cols(x):
        x = x.reshape((nsg, gs) + x.shape[1:])
        x = jnp.transpose(x, (0, 2, 3, 1, 4))[:, None]
        x = x * eye[None, :, None, None, :, None]
        return x.reshape(nsg, gs * SSM_STATE, tc * gs * SSM_GROUP)

    m = ms[0] + ms[1]
    bc = jnp.concatenate([pack_rows(jnp.real(bcs[0])), pack_rows(jnp.real(bcs[1])),
                          pack_rows(jnp.imag(bcs[0])), pack_rows(jnp.imag(bcs[1]))], axis=-1)
    cc = jnp.concatenate([pack_cols(jnp.real(ccs[0])), pack_cols(jnp.real(ccs[1])),
                          pack_cols(-jnp.imag(ccs[0])), pack_cols(-jnp.imag(ccs[1]))], axis=1)
    a_f, a_b = a16s[0].reshape(nsg, gs * SSM_STATE), a16s[1].reshape(nsg, gs * SSM_STATE)
    a_chunk = jnp.stack([jnp.concatenate([jnp.real(a_f), jnp.real(a_b)], axis=-1),
                         jnp.concatenate([jnp.imag(a_f), jnp.imag(a_b)], axis=-1)], axis=1)
    dtile = jnp.tile(d_skip.astype(F32).reshape(nsg, 1, gs * SSM_GROUP), (1, 1, tc))
    return m.astype(BF16), bc.astype(BF16), cc.astype(BF16), a_chunk.astype(F32), dtile


def _s5_kernel(u_ref, m_ref, bc_ref, cc_ref, a_ref, d_ref, y_ref, vre_scr, vim_scr, hf_re, hb_re, hf_im, hb_im,
               *, nchunk, nctx):
    tc = SSM_CHUNK
    w = hf_re.shape[1]
    u = jnp.concatenate([u_ref[0, pl.ds(i, nchunk, stride=tc), :] for i in range(tc)], axis=1)
    ub = u.astype(BF16)
    v = jnp.dot(ub, bc_ref[0], preferred_element_type=F32)
    vre_scr[...] = v[:, 0:2 * w]
    vim_scr[...] = v[:, 2 * w:4 * w]
    ar = a_ref[0, 0:1, :]
    ai = a_ref[0, 1:2, :]
    hre = jnp.zeros((1, 2 * w), F32)
    him = jnp.zeros((1, 2 * w), F32)
    for s in range(nchunk):
        cf = s
        cb = (nctx - 1 - s) if s < nctx else (nchunk - 1 - (s - nctx))
        hf_re[cf:cf + 1, :] = hre[:, 0:w]
        hb_re[cb:cb + 1, :] = hre[:, w:2 * w]
        hf_im[cf:cf + 1, :] = him[:, 0:w]
        hb_im[cb:cb + 1, :] = him[:, w:2 * w]
        vre = jnp.concatenate([vre_scr[cf:cf + 1, 0:w], vre_scr[cb:cb + 1, w:2 * w]], axis=1)
        vim = jnp.concatenate([vim_scr[cf:cf + 1, 0:w], vim_scr[cb:cb + 1, w:2 * w]], axis=1)
        hre, him = ar * hre - ai * him + vre, ar * him + ai * hre + vim
    hp = jnp.concatenate([hf_re[...], hb_re[...], hf_im[...], hb_im[...]], axis=1)
    y = jnp.dot(ub, m_ref[0], preferred_element_type=F32)
    y = y + jnp.dot(hp.astype(BF16), cc_ref[0], preferred_element_type=F32)
    y = y + u * d_ref[0]
    for i in range(tc):
        y_ref[0, pl.ds(i, nchunk, stride=tc), :] = y[:, i * LANES:(i + 1) * LANES]


def _s5_mix(p3, ops, width, lc):
    m, bc, cc, a_chunk, dtile = ops
    b, s, _ = p3.shape
    tc = SSM_CHUNK
    nchunk = s // tc
    cw = SSM_PACK * SSM_GROUP
    assert cw == LANES and width % cw == 0 and s % tc == 0 and lc % tc == 0
    k = tc * cw
    sw = SSM_PACK * SSM_STATE
    return pl.pallas_call(
        functools.partial(_s5_kernel, nchunk=nchunk, nctx=lc // tc),
        grid=(width // cw, b),
        in_specs=[
            pl.BlockSpec((1, s, cw), lambda g, i: (i, 0, g)),
            pl.BlockSpec((1, k, k), lambda g, i: (g, 0, 0)),
            pl.BlockSpec((1, k, 4 * sw), lambda g, i: (g, 0, 0)),
            pl.BlockSpec((1, 4 * sw, k), lambda g, i: (g, 0, 0)),
            pl.BlockSpec((1, 2, 2 * sw), lambda g, i: (g, 0, 0)),
            pl.BlockSpec((1, 1, k), lambda g, i: (g, 0, 0)),
        ],
        out_specs=pl.BlockSpec((1, s, cw), lambda g, i: (i, 0, g)),
        out_shape=jax.ShapeDtypeStruct((b, s, width), F32),
        scratch_shapes=[pltpu.VMEM((nchunk, 2 * sw), F32)] * 2 + [pltpu.VMEM((nchunk, sw), F32)] * 4,
        compiler_params=_cparams(("parallel", "parallel")),
        name="s5_mix",
    )(p3, m, bc, cc, a_chunk, dtile)


def _rope_tables(lc, lx):
    n = ATTN_HEAD_DIM // 4
    freq = ROPE_THETA ** (-jnp.arange(n, dtype=F32) / n)
    t = jnp.arange(lx, dtype=jnp.int32)
    rows = (t // GRID_W).astype(F32)[:, None] * freq[None, :]
    cols = (t % GRID_W).astype(F32)[:, None] * freq[None, :]
    cos = jnp.concatenate([jnp.cos(rows), jnp.cos(rows), jnp.cos(cols), jnp.cos(cols)], axis=-1)
    sin = jnp.concatenate([-jnp.sin(rows), jnp.sin(rows), -jnp.sin(cols), jnp.sin(cols)], axis=-1)
    cos = jnp.concatenate([jnp.ones((lc, ATTN_HEAD_DIM), F32), cos], axis=0)
    sin = jnp.concatenate([jnp.zeros((lc, ATTN_HEAD_DIM), F32), sin], axis=0)
    return cos, sin


def _qk_prep_kernel(q_ref, k_ref, v_ref, cos_ref, sin_ref, qg_ref, kg_ref, qo_ref, ko_ref, vo_ref):
    hd = ATTN_HEAD_DIM
    cos = cos_ref[...]
    sin = sin_ref[...]
    lane = lax.broadcasted_iota(jnp.int32, cos.shape, 1)
    first = (lane % (hd // 2)) < (hd // 4)

    def prep(x, g):
        y = x * lax.rsqrt(jnp.mean(x * x, axis=-1, keepdims=True) + EPS) * g
        partner = jnp.where(first, pltpu.roll(y, hd - hd // 4, 1), pltpu.roll(y, hd // 4, 1))
        return y * cos + partner * sin

    for h in range(ATTN_Q_PER_KV):
        sl = slice(h * hd, (h + 1) * hd)
        qo_ref[0, :, sl] = prep(q_ref[0, :, sl], qg_ref[...]).astype(qo_ref.dtype)
    ko_ref[0] = prep(k_ref[0], kg_ref[...]).astype(ko_ref.dtype)
    vo_ref[0] = v_ref[0].astype(vo_ref.dtype)


def _qk_prep(p3, cos, sin, q_gain, k_gain, col_q, col_k, col_v):
    b, s, _ = p3.shape
    hd = ATTN_HEAD_DIM
    qw = ATTN_Q_PER_KV * hd
    tq = _pick(s, (544, 256, 128))
    return pl.pallas_call(
        _qk_prep_kernel,
        grid=(b, s // tq, ATTN_KV_HEADS),
        in_specs=[
            pl.BlockSpec((1, tq, qw), lambda i, j, kv: (i, j, col_q // qw + kv)),
            pl.BlockSpec((1, tq, hd), lambda i, j, kv: (i, j, col_k // hd + kv)),
            pl.BlockSpec((1, tq, hd), lambda i, j, kv: (i, j, col_v // hd + kv)),
            pl.BlockSpec((tq, hd), lambda i, j, kv: (j, 0)),
            pl.BlockSpec((tq, hd), lambda i, j, kv: (j, 0)),
            pl.BlockSpec((1, hd), lambda i, j, kv: (0, 0)),
            pl.BlockSpec((1, hd), lambda i, j, kv: (0, 0)),
        ],
        out_specs=[
            pl.BlockSpec((1, tq, qw), lambda i, j, kv: (i, j, kv)),
            pl.BlockSpec((1, tq, hd), lambda i, j, kv: (i, j, kv)),
            pl.BlockSpec((1, tq, hd), lambda i, j, kv: (i, j, kv)),
        ],
        out_shape=[
            jax.ShapeDtypeStruct((b, s, ATTN_KV_HEADS * qw), BF16),
            jax.ShapeDtypeStruct((b, s, ATTN_KV_HEADS * hd), BF16),
            jax.ShapeDtypeStruct((b, s, ATTN_KV_HEADS * hd), BF16),
        ],
        compiler_params=_cparams(("parallel", "parallel", "parallel")),
        name="qk_prep",
    )(p3, p3, p3, cos, sin, q_gain.reshape(1, hd), k_gain.reshape(1, hd))


def _attn_kernel(q_ref, k_ref, v_ref, o_ref, *, lc, tq):
    hd = ATTN_HEAD_DIM
    c = (hd ** -0.5) * math.log2(math.e)

    def run(kk, vv):
        hs = ATTN_HEADS_PER_PASS

        def scores(g):
            q = jnp.concatenate([q_ref[0, :, (g * hs + a) * hd:(g * hs + a + 1) * hd] for a in range(hs)], axis=0)
            return lax.dot_general(q, kk, (((1,), (1,)), ((), ())), preferred_element_type=F32)

        v_ext = jnp.concatenate([vv, jnp.ones((vv.shape[0], LANES), BF16)], axis=1)

        n_pass = ATTN_Q_PER_KV // hs
        s_next = scores(0)
        for g in range(n_pass):
            s = s_next
            if g + 1 < n_pass:
                s_next = scores(g + 1)
            m = jnp.max(s, axis=-1, keepdims=True)
            p = jnp.exp2((s - m) * c).astype(BF16)
            oe = jnp.dot(p, v_ext, preferred_element_type=F32)
            o = (oe[:, 0:hd] / oe[:, hd:hd + 1]).astype(o_ref.dtype)
            for a in range(hs):
                o_ref[0, :, (g * hs + a) * hd:(g * hs + a + 1) * hd] = o[a * tq:(a + 1) * tq, :]

    is_ctx_tile = pl.program_id(2) * tq < lc

    @pl.when(jnp.logical_not(is_ctx_tile))
    def _():
        run(k_ref[0], v_ref[0])

    @pl.when(is_ctx_tile)
    def _():
        run(k_ref[0, :lc, :], v_ref[0, :lc, :])


def _attention(qh, kh, vh, lc):
    b, s, _ = qh.shape
    hd = ATTN_HEAD_DIM
    qw = ATTN_Q_PER_KV * hd
    tq = _pick(lc, (256, 128))
    assert s % tq == 0
    return pl.pallas_call(
        functools.partial(_attn_kernel, lc=lc, tq=tq),
        grid=(b, ATTN_KV_HEADS, s // tq),
        in_specs=[
            pl.BlockSpec((1, tq, qw), lambda i, kv, j: (i, j, kv)),
            pl.BlockSpec((1, s, hd), lambda i, kv, j: (i, 0, kv)),
            pl.BlockSpec((1, s, hd), lambda i, kv, j: (i, 0, kv)),
        ],
        out_specs=pl.BlockSpec((1, tq, qw), lambda i, kv, j: (i, j, kv)),
        out_shape=jax.ShapeDtypeStruct((b, s, ATTN_KV_HEADS * qw), BF16),
        compiler_params=_cparams(("parallel", "parallel", "parallel")),
        name="attention",
    )(qh, kh, vh)


def _split3(x):
    hi = x.astype(BF16)
    r = x - hi.astype(F32)
    mid = r.astype(BF16)
    lo = (r - mid.astype(F32)).astype(BF16)
    return hi, mid, lo


def _log_sigmoid(x):
    return jnp.minimum(x, 0.0) - jnp.log(1.0 + jnp.exp(-jnp.abs(x)))


def _mlstm_kernel(q_ref, k_ref, v_ref, gc_ref, gr_ref, bc_ref, br_ref, o_ref, c_scr, n_scr, m_scr):
    nh, dk, dv, cl = MLSTM_HEADS, MLSTM_QK_DIM, MLSTM_V_DIM, MLSTM_CHUNK
    nb = q_ref.shape[0]
    d = pl.program_id(0)
    c = pl.program_id(1)

    @pl.when(c == 0)
    def _():
        c_scr[...] = jnp.zeros_like(c_scr)
        n_scr[...] = jnp.zeros_like(n_scr)
        m_scr[...] = jnp.zeros_like(m_scr)

    row = lax.broadcasted_iota(jnp.int32, (cl, cl), 0)
    col = lax.broadcasted_iota(jnp.int32, (cl, cl), 1)
    sign = jnp.where(d == 0, 1, -1)
    mask = sign * (row - col) >= 0
    tm = jnp.where(mask, 1.0, 0.0).astype(BF16)
    tmt = jnp.where(sign * (col - row) >= 0, 1.0, 0.0).astype(BF16)

    def pick_c(a, k):
        return jnp.where(d == 0, a[:, k:k + 1], a[:, 2 * nh + k:2 * nh + k + 1])

    def pick_r(a, k):
        return jnp.where(d == 0, a[k:k + 1, :], a[2 * nh + k:2 * nh + k + 1, :])

    nt = (((1,), (1,)), ((), ()))
    tn = (((0,), (0,)), ((), ()))
    heads = [(bi, h) for bi in range(nb) for h in range(nh)]

    gcs = [gc_ref[bi] + bc_ref[...] for bi in range(nb)]
    grs = [gr_ref[bi, 0] + br_ref[...] for bi in range(nb)]
    lf_cs = [_log_sigmoid(g) for g in gcs]
    lf_rs = [_log_sigmoid(g) for g in grs]
    cum_cs = [sum(jnp.dot(tm, part, preferred_element_type=F32) for part in _split3(x)) for x in lf_cs]
    cum_rs = [sum(jnp.dot(part, tmt, preferred_element_type=F32) for part in _split3(x)) for x in lf_rs]
    totals = [jnp.sum(x, axis=0, keepdims=True) for x in lf_cs]

    qs = [q_ref[bi][:, h * dk:(h + 1) * dk] * (dk ** -0.5) for bi, h in heads]
    ks = [k_ref[bi][:, h * dk:(h + 1) * dk] for bi, h in heads]
    vs = [v_ref[bi][:, h * dv:(h + 1) * dv] for bi, h in heads]
    qbs = [x.astype(BF16) for x in qs]
    kbs = [x.astype(BF16) for x in ks]
    m_sts = [m_scr[i][:, 0:1] for i in range(len(heads))]
    c_sts = [c_scr[i] for i in range(len(heads))]
    n_sts = [n_scr[i] for i in range(len(heads))]

    s_raw = [lax.dot_general(qbs[i], kbs[i], nt, preferred_element_type=F32) for i in range(len(heads))]
    cqs = [lax.dot_general(qbs[i], c_sts[i].astype(BF16), nt, preferred_element_type=F32)
           for i in range(len(heads))]

    cf_c = [pick_c(cum_cs[bi], nh + h) for bi, h in heads]
    cf_r = [pick_r(cum_rs[bi], nh + h) for bi, h in heads]
    li_c = [pick_c(gcs[bi], h) for bi, h in heads]
    li_r = [pick_r(grs[bi], h) for bi, h in heads]
    tot = [pick_c(totals[bi], nh + h) for bi, h in heads]

    log_d = [jnp.where(mask, cf_c[i] - cf_r[i] + li_r[i], -jnp.inf) for i in range(len(heads))]
    log_inter = [cf_c[i] + m_sts[i] for i in range(len(heads))]
    m_row = [jnp.maximum(log_inter[i], jnp.max(log_d[i], axis=-1, keepdims=True)) for i in range(len(heads))]
    s = [s_raw[i] * jnp.exp(log_d[i] - m_row[i]) for i in range(len(heads))]
    w_inter = [jnp.exp(log_inter[i] - m_row[i]) for i in range(len(heads))]
    sv = [jnp.dot(s[i].astype(BF16), vs[i].astype(BF16), preferred_element_type=F32) for i in range(len(heads))]
    den = [jnp.sum(s[i], axis=-1, keepdims=True)
           + w_inter[i] * jnp.sum(qs[i] * n_sts[i], axis=-1, keepdims=True) for i in range(len(heads))]
    for i, (bi, h) in enumerate(heads):
        num = sv[i] + w_inter[i] * cqs[i]
        o_ref[0, bi, :, h * dv:(h + 1) * dv] = num / jnp.maximum(jnp.abs(den[i]), jnp.exp(-m_row[i]))

    log_w = [tot[i] - cf_c[i] + li_c[i] for i in range(len(heads))]
    m_new = [jnp.maximum(tot[i] + m_sts[i], jnp.max(log_w[i], axis=0, keepdims=True)) for i in range(len(heads))]
    decay = [jnp.exp(tot[i] + m_sts[i] - m_new[i]) for i in range(len(heads))]
    w = [jnp.exp(log_w[i] - m_new[i]) for i in range(len(heads))]
    kv = [lax.dot_general((w[i] * vs[i]).astype(BF16), kbs[i], tn, preferred_element_type=F32)
          for i in range(len(heads))]
    for i in range(len(heads)):
        c_scr[i] = decay[i] * c_sts[i] + kv[i]
        n_scr[i] = decay[i] * n_sts[i] + jnp.sum(w[i] * ks[i], axis=0, keepdims=True)
        m_scr[i] = jnp.broadcast_to(m_new[i], (1, LANES))


def _mlstm(p3, gate_bias, lc, col_q, col_k, col_v, col_g):
    b, s, _ = p3.shape
    nh, dk, dv, cl = MLSTM_HEADS, MLSTM_QK_DIM, MLSTM_V_DIM, MLSTM_CHUNK
    nc = s // cl
    nctx = lc // cl
    ng = 4 * nh
    g_rows = jnp.transpose(p3[:, :, col_g:col_g + ng].reshape(b, nc, cl, ng), (0, 1, 3, 2))
    bias = gate_bias.astype(F32).reshape(ng)

    def chunk(d, c):
        bwd = jnp.where(c < nctx, nctx - 1 - c, nc - 1 - (c - nctx))
        return jnp.where(d == 0, c, bwd)

    return pl.pallas_call(
        _mlstm_kernel,
        grid=(2, nc),
        in_specs=[
            pl.BlockSpec((b, cl, nh * dk), lambda d, c: (0, chunk(d, c), col_q // (nh * dk))),
            pl.BlockSpec((b, cl, nh * dk), lambda d, c: (0, chunk(d, c), col_k // (nh * dk))),
            pl.BlockSpec((b, cl, nh * dv), lambda d, c: (0, chunk(d, c), col_v // (nh * dv))),
            pl.BlockSpec((b, cl, LANES), lambda d, c: (0, chunk(d, c), col_g // LANES)),
            pl.BlockSpec((b, 1, ng, cl), lambda d, c: (0, chunk(d, c), 0, 0)),
            pl.BlockSpec((1, LANES), lambda d, c: (0, 0)),
            pl.BlockSpec((ng, 1), lambda d, c: (0, 0)),
        ],
        out_specs=pl.BlockSpec((1, b, cl, nh * dv), lambda d, c: (d, 0, chunk(d, c), 0)),
        out_shape=jax.ShapeDtypeStruct((2, b, s, nh * dv), F32),
        scratch_shapes=[pltpu.VMEM((b * nh, dv, dk), F32), pltpu.VMEM((b * nh, 1, dk), F32),
                        pltpu.VMEM((b * nh, 1, LANES), F32)],
        compiler_params=_cparams(("arbitrary", "arbitrary")),
        name="mlstm",
    )(p3, p3, p3, p3, g_rows, jnp.pad(bias, (0, LANES - ng)).reshape(1, LANES), bias.reshape(ng, 1))


def _gelu_tanh(x):
    return 0.5 * x * (1.0 + jnp.tanh(math.sqrt(2.0 / math.pi) * (x + 0.044715 * (x * x * x))))


def _merge_kernel(ys_ref, att_ref, hf_ref, hb_ref, og_ref, gl_ref, wglu_ref, bglu_ref, ng_ref,
                  wbs_ref, wba_ref, wbm_ref, y_ref):
    d = y_ref.shape[-1]
    z = _gelu_tanh(ys_ref[...])
    s5 = z * jax.nn.sigmoid(jnp.dot(z.astype(BF16), wglu_ref[...], preferred_element_type=F32) + bglu_ref[...])
    hsum = hf_ref[0] + hb_ref[0]
    dv = MLSTM_V_DIM
    parts = []
    for h in range(MLSTM_HEADS):
        blk = hsum[:, h * dv:(h + 1) * dv]
        parts.append(blk * lax.rsqrt(jnp.mean(blk * blk, axis=-1, keepdims=True) + EPS))
    ml = jnp.concatenate(parts, axis=-1) * ng_ref[...] * jax.nn.sigmoid(og_ref[...])
    y = gl_ref[:, 0:d].astype(F32) * jnp.dot(s5.astype(BF16), wbs_ref[...], preferred_element_type=F32)
    y = y + gl_ref[:, d:2 * d].astype(F32) * jnp.dot(att_ref[...], wba_ref[...], preferred_element_type=F32)
    y = y + gl_ref[:, 2 * d:3 * d].astype(F32) * jnp.dot(ml.astype(BF16), wbm_ref[...], preferred_element_type=F32)
    y_ref[...] = y.astype(y_ref.dtype)


def _merge(ys, att, hfb, p_main, gates, wglu, bglu, ng, wbs, wba, wbm, col_o):
    t, ws = ys.shape
    wa = att.shape[1]
    wm = hfb.shape[-1]
    d = wbs.shape[1]
    tm = _pick(t, (256, 128))
    full = lambda shape: pl.BlockSpec(shape, lambda i: (0,) * len(shape))
    return pl.pallas_call(
        _merge_kernel,
        grid=(t // tm,),
        in_specs=[
            pl.BlockSpec((tm, ws), lambda i: (i, 0)),
            pl.BlockSpec((tm, wa), lambda i: (i, 0)),
            pl.BlockSpec((1, tm, wm), lambda i: (0, i, 0)),
            pl.BlockSpec((1, tm, wm), lambda i: (1, i, 0)),
            pl.BlockSpec((tm, wm), lambda i: (i, col_o // wm)),
            pl.BlockSpec((tm, 3 * d), lambda i: (i, 0)),
            full((ws, ws)), full((1, ws)), full((1, wm)),
            full((ws, d)), full((wa, d)), full((wm, d)),
        ],
        out_specs=pl.BlockSpec((tm, d), lambda i: (i, 0)),
        out_shape=jax.ShapeDtypeStruct((t, d), BF16),
        compiler_params=_cparams(("parallel",)),
        name="branch_merge",
    )(ys, att, hfb, hfb, p_main, gates, wglu, bglu.reshape(1, ws), ng.reshape(1, wm), wbs, wba, wbm)


def _out_proj_kernel(y_ref, w_ref, x_ref, mod_ref, o_ref, *, lc, tm):
    out = jnp.dot(y_ref[0], w_ref[...], preferred_element_type=F32)
    gate = _row_select(mod_ref, 2, _is_ctx_rows(tm, lc))
    o_ref[0] = x_ref[0] + gate * out


def _out_proj(y3, w_out, xa, mod, lc):
    b, s, d = xa.shape
    tm = _pick(s, (544, 256, 128))
    return pl.pallas_call(
        functools.partial(_out_proj_kernel, lc=lc, tm=tm),
        grid=(b, s // tm),
        in_specs=[
            pl.BlockSpec((1, tm, d), lambda i, j: (i, j, 0)),
            pl.BlockSpec((d, d), lambda i, j: (0, 0)),
            pl.BlockSpec((1, tm, d), lambda i, j: (i, j, 0)),
            pl.BlockSpec((1, 2, 6, d), lambda i, j: (i, 0, 0, 0)),
        ],
        out_specs=pl.BlockSpec((1, tm, d), lambda i, j: (i, j, 0)),
        out_shape=jax.ShapeDtypeStruct((b, s, d), F32),
        compiler_params=_cparams(("parallel", "parallel")),
        name="out_proj_residual",
    )(y3, w_out, xa, mod)


def _norm_router_kernel(x_ref, g_ref, mod_ref, wr_ref, br_ref, h_ref, idx_ref, w_ref, *, lc, tm):
    _norm_and_route(x_ref[0], g_ref, mod_ref, wr_ref, br_ref, h_ref, idx_ref, w_ref, _is_ctx_rows(tm, lc))


def _norm_router(xa, g, mod, w_router, b_router, lc):
    b, s, d = xa.shape
    tm = _pick(s, (544, 256, 128))
    ne = w_router.shape[1]
    return pl.pallas_call(
        functools.partial(_norm_router_kernel, lc=lc, tm=tm),
        grid=(b, s // tm),
        in_specs=[
            pl.BlockSpec((1, tm, d), lambda i, j: (i, j, 0)),
            pl.BlockSpec((1, d), lambda i, j: (0, 0)),
            pl.BlockSpec((1, 2, 6, d), lambda i, j: (i, 0, 0, 0)),
            pl.BlockSpec((d, ne), lambda i, j: (0, 0)),
            pl.BlockSpec((1, ne), lambda i, j: (0, 0)),
        ],
        out_specs=[
            pl.BlockSpec((1, tm, d // LANES, LANES), lambda i, j: (i, j, 0, 0)),
            pl.BlockSpec((1, tm, TOP_K), lambda i, j: (i, j, 0)),
            pl.BlockSpec((1, tm, TOP_K), lambda i, j: (i, j, 0)),
        ],
        out_shape=[
            jax.ShapeDtypeStruct((b, s, d // LANES, LANES), F32),
            jax.ShapeDtypeStruct((b, s, TOP_K), jnp.int32),
            jax.ShapeDtypeStruct((b, s, TOP_K), F32),
        ],
        compiler_params=_cparams(("parallel", "parallel")),
        name="norm_router",
    )(xa, g.reshape(1, d), mod, w_router, b_router.reshape(1, ne))


def _pair_split_matrix():
    half = MXU_COLS // 2
    n = jnp.arange(MXU_COLS)
    src = jnp.where(n < half, 2 * n, 2 * (n - half) + 1)
    return (jnp.arange(MXU_COLS)[:, None] == src[None, :]).astype(BF16)


def _pair_split_index(width):
    half = MXU_COLS // 2
    n = jnp.arange(width)
    blk, r = n // MXU_COLS, n % MXU_COLS
    return blk * MXU_COLS + jnp.where(r < half, 2 * r, 2 * (r - half) + 1)


def _split_pairs_kernel(w_ref, p_ref, after_ref, o_ref):
    del after_ref
    for cb in range(w_ref.shape[-1] // MXU_COLS):
        sl = slice(cb * MXU_COLS, (cb + 1) * MXU_COLS)
        o_ref[0, :, sl] = jnp.dot(w_ref[0, :, sl].astype(BF16), p_ref[...],
                                  preferred_element_type=F32).astype(o_ref.dtype)


def _split_pairs(w_all, first, count, after):
    _, d, n = w_all.shape
    tc = _pick(n, (2 * MXU_COLS, MXU_COLS))
    return pl.pallas_call(
        _split_pairs_kernel,
        grid=(count, n // tc),
        in_specs=[
            pl.BlockSpec((1, d, tc), lambda e, i: (first + e, 0, i)),
            pl.BlockSpec((MXU_COLS, MXU_COLS), lambda e, i: (0, 0)),
            pl.BlockSpec((8, LANES), lambda e, i: (0, 0)),
        ],
        out_specs=pl.BlockSpec((1, d, tc), lambda e, i: (e, 0, i)),
        out_shape=jax.ShapeDtypeStruct((count, d, n), BF16),
        compiler_params=_cparams(("parallel", "parallel")),
        name="split_gate_linear",
    )(w_all, _pair_split_matrix(), after)


def _expert_kernel(te_ref, nu_ref, tok_ref, tok_next_ref, h_hbm, *refs):
    npart = EXPERT_WEIGHT_PARTS
    wgl_refs, bgl_ref = refs[0:npart], refs[npart]
    wd_refs, bd_ref = refs[npart + 1:2 * npart + 1], refs[2 * npart + 1]
    o_ref, wd_scr, xbuf, sem = refs[2 * npart + 2:2 * npart + 6]
    i = pl.program_id(0)
    half = MXU_COLS // 2
    tm = o_ref.shape[0]
    nseg = h_hbm.shape[1]
    n_used = nu_ref[0]
    used = i < n_used
    slot = lax.rem(i, 2)
    new_expert = jnp.logical_or(i == 0, te_ref[i] != te_ref[jnp.maximum(i - 1, 0)])
    wc = wd_refs[0].shape[2]

    def start_rows(idx_ref, dst_slot):
        for r in range(tm):
            dst = xbuf.at[pl.ds((dst_slot * tm + r) * nseg, nseg)]
            pltpu.make_async_copy(h_hbm.at[idx_ref[0, 0, r]], dst, sem.at[dst_slot]).start()

    def wait_rows(dst_slot):
        dst = xbuf.at[pl.ds(dst_slot * tm * nseg, tm * nseg)]
        pltpu.make_async_copy(dst, dst, sem.at[dst_slot]).wait()

    @pl.when(i == 0)
    def _():
        start_rows(tok_ref, 0)

    @pl.when(jnp.logical_and(used, new_expert))
    def _():
        for part in range(npart):
            wd_scr[:, part * wc:(part + 1) * wc] = wd_refs[part][0].astype(BF16)

    @pl.when(used)
    def _():
        start_rows(tok_next_ref, 1 - slot)
        wait_rows(slot)

    @pl.when(used)
    def _():
        base = slot * (tm * nseg)
        x = jnp.concatenate([xbuf[pl.ds(base + j, tm, stride=nseg), :] for j in range(nseg)],
                            axis=1).astype(BF16)
        gu = jnp.concatenate([jnp.dot(x, w[0], preferred_element_type=F32) for w in wgl_refs], axis=1)
        gu = gu + bgl_ref[0]
        nblk = gu.shape[1] // MXU_COLS
        gate = jnp.concatenate([gu[:, cb * MXU_COLS:cb * MXU_COLS + half] for cb in range(nblk)], axis=1)
        lin = jnp.concatenate([gu[:, cb * MXU_COLS + half:(cb + 1) * MXU_COLS] for cb in range(nblk)], axis=1)
        gate = jnp.minimum(gate, SWIGLU_LIMIT)
        lin = jnp.clip(lin, -SWIGLU_LIMIT, SWIGLU_LIMIT)
        act = gate * jax.nn.sigmoid(SWIGLU_ALPHA * gate) * (lin + 1.0)
        y = jnp.dot(act.astype(BF16), wd_scr[...], preferred_element_type=F32) + bd_ref[0]
        o_ref[...] = y.astype(o_ref.dtype)

    @pl.when(i == n_used - 1)
    def _():
        wait_rows(1 - slot)

    @pl.when(jnp.logical_not(used))
    def _():
        o_ref[...] = jnp.zeros_like(o_ref)


def _experts(h_rows, slot_tok, tile_expert, n_used, wgl, bgl, wd, bd, first):
    _, nseg, lanes = h_rows.shape
    d = nseg * lanes
    n_tiles, tm = slot_tok.shape
    _, _, ff2 = wgl.shape
    ff = ff2 // 2
    npart = EXPERT_WEIGHT_PARTS
    col_part = lambda part: (lambda i, te, nu: (te[i], 0, part))
    wgl_part = lambda part: (lambda i, te, nu: (te[i] - first, 0, part))
    grid_spec = pltpu.PrefetchScalarGridSpec(
        num_scalar_prefetch=2,
        grid=(n_tiles,),
        in_specs=(
            [pl.BlockSpec((1, 1, tm), lambda i, te, nu: (i, 0, 0), memory_space=pltpu.SMEM),
             pl.BlockSpec((1, 1, tm), lambda i, te, nu: (jnp.minimum(i + 1, n_tiles - 1), 0, 0),
                          memory_space=pltpu.SMEM),
             pl.BlockSpec(memory_space=pl.ANY)]
            + [pl.BlockSpec((1, d, ff2 // npart), wgl_part(part)) for part in range(npart)]
            + [pl.BlockSpec((1, 1, ff2), lambda i, te, nu: (te[i], 0, 0))]
            + [pl.BlockSpec((1, ff, d // npart), col_part(part)) for part in range(npart)]
            + [pl.BlockSpec((1, 1, d), lambda i, te, nu: (te[i], 0, 0))]),
        out_specs=pl.BlockSpec((tm, d), lambda i, te, nu: (i, 0)),
        scratch_shapes=[pltpu.VMEM((ff, d), BF16),
                        pltpu.VMEM((2 * tm * nseg, lanes), F32),
                        pltpu.SemaphoreType.DMA((2,))],
    )
    tok3 = slot_tok[:, None, :]
    return pl.pallas_call(
        _expert_kernel,
        grid_spec=grid_spec,
        out_shape=jax.ShapeDtypeStruct((n_tiles * tm, d), BF16),
        compiler_params=_cparams(("arbitrary",)),
        name="experts",
    )(tile_expert, n_used, tok3, tok3, h_rows, *([wgl] * npart), bgl, *([wd] * npart), bd)


def _route(top_idx, n_slots):
    tm = EXPERT_TILE
    e = top_idx.reshape(-1)
    n = e.shape[0]
    onehot = (e[:, None] == jnp.arange(N_EXPERTS, dtype=jnp.int32)[None, :]).astype(jnp.int32)
    counts = jnp.sum(onehot, axis=0)
    rank = jnp.take_along_axis(jnp.cumsum(onehot, axis=0) - onehot, e[:, None], axis=1)[:, 0]
    padded = ((counts + tm - 1) // tm) * tm
    ends = jnp.cumsum(padded)
    dest = (ends - padded)[e] + rank
    n_used = (ends[-1] // tm).astype(jnp.int32)
    tile_start = jnp.arange(n_slots // tm, dtype=jnp.int32) * tm
    tile_expert = jnp.sum((ends[None, :] <= tile_start[:, None]).astype(jnp.int32), axis=1)
    last_used = jnp.sum((ends <= (n_used - 1) * tm).astype(jnp.int32))
    tile_expert = jnp.minimum(jnp.where(tile_start < ends[-1], tile_expert, last_used), N_EXPERTS - 1)
    order = jnp.argsort(e, stable=True).astype(jnp.int32)
    tile_rank0 = tile_start - (ends - padded)[tile_expert]
    tile_count = jnp.where(tile_start < ends[-1], counts[tile_expert], 0)
    tile_sorted0 = (jnp.cumsum(counts) - counts)[tile_expert] + tile_rank0
    lane = jnp.arange(tm, dtype=jnp.int32)[None, :]
    valid = tile_rank0[:, None] + lane < tile_count[:, None]
    sorted_pos = jnp.clip(tile_sorted0[:, None] + lane, 0, n - 1)
    src = jnp.where(valid, order[sorted_pos] // TOP_K, 0)
    return src, dest, tile_expert.astype(jnp.int32), n_used.reshape(1)


def _combine_kernel(y0_ref, y1_ref, y2_ref, y3_ref, tw_ref, x_ref, mod_ref, g_ref, *rest, lc, tm, final):
    tw = tw_ref[0]
    moe = tw[:, 0:1] * y0_ref[0, 0].astype(F32)
    for k, y_ref in enumerate((y1_ref, y2_ref, y3_ref), start=1):
        moe = moe + tw[:, k:k + 1] * y_ref[0, 0].astype(F32)
    if final:
        (o_ref,) = rest
        xn = x_ref[0] + mod_ref[0, 1, 5:6, :] * moe
        o_ref[0] = xn * lax.rsqrt(jnp.mean(xn * xn, axis=-1, keepdims=True) + EPS) * g_ref[...]
    else:
        next_mod_ref, o_ref, h_ref = rest
        is_ctx = _is_ctx_rows(tm, lc)
        xn = x_ref[0] + _row_select(mod_ref, 5, is_ctx) * moe
        o_ref[0] = xn
        h_ref[0] = _modulated_norm(xn, g_ref[...], next_mod_ref, 0, is_ctx).astype(h_ref.dtype)


def _combine(yg, top_w, xa, mod, g, lc, final, next_mod=None):
    b, s, d = xa.shape
    assert TOP_K == 4
    if final:
        tm = _pick(lc, (256, 128))
        assert (s - lc) % tm == 0
        off, rows = lc // tm, s - lc
    else:
        tm = _pick(s, (544, 256, 128))
        off, rows = 0, s
    y_specs = [pl.BlockSpec((1, 1, tm, d), functools.partial(lambda i, j, k: (k, i, j + off, 0), k=k))
               for k in range(TOP_K)]
    mod_spec = pl.BlockSpec((1, 2, 6, d), lambda i, j: (i, 0, 0, 0))
    row_out = pl.BlockSpec((1, tm, d), lambda i, j: (i, j, 0))
    in_specs = y_specs + [
        pl.BlockSpec((1, tm, TOP_K), lambda i, j: (i, j + off, 0)),
        pl.BlockSpec((1, tm, d), lambda i, j: (i, j + off, 0)),
        mod_spec,
        pl.BlockSpec((1, d), lambda i, j: (0, 0)),
    ]
    args = [yg, yg, yg, yg, top_w, xa, mod, g.reshape(1, d)]
    if final:
        out_specs, out_shape = row_out, jax.ShapeDtypeStruct((b, rows, d), F32)
    else:
        in_specs.append(mod_spec)
        args.append(next_mod)
        out_specs = [row_out, row_out]
        out_shape = [jax.ShapeDtypeStruct((b, rows, d), F32), jax.ShapeDtypeStruct((b, rows, d), BF16)]
    return pl.pallas_call(
        functools.partial(_combine_kernel, lc=lc, tm=tm, final=final),
        grid=(b, rows // tm),
        in_specs=in_specs,
        out_specs=out_specs,
        out_shape=out_shape,
        compiler_params=_cparams(("parallel", "parallel")),
        name="moe_combine",
    )(*args)


def kernel(x, c, ctx, c_ctx, w_ada, b_ada, norm_g, w_in, ssm_a_re, ssm_a_im, ssm_log_dt, ssm_b_re, ssm_b_im, ssm_c_re, ssm_c_im, ssm_d, ssm_w_glu, ssm_b_glu, attn_q_gain, attn_k_gain, mlstm_gate_bias, mlstm_norm_g, w_branch_ssm, w_branch_attn, w_branch_mlstm, w_out, w_router, b_router, w_gate_up, b_gate_up, w_down, b_down, final_g):
    b, lx, d = x.shape
    lc = ctx.shape[1]
    s = lc + lx
    t = b * s
    depth = w_in.shape[0]
    assert b + 1 <= 8 and lc % MLSTM_CHUNK == 0 and lx % MLSTM_CHUNK == 0

    ssm_w = ssm_d.shape[1]
    attn_w = w_branch_attn.shape[1]
    kv_w = ATTN_KV_HEADS * ATTN_HEAD_DIM
    mqk_w = MLSTM_HEADS * MLSTM_QK_DIM
    mv_w = MLSTM_HEADS * MLSTM_V_DIM
    n_gates = 4 * MLSTM_HEADS
    col_q = ssm_w
    col_k = col_q + attn_w
    col_v = col_k + kv_w
    col_mq = col_v + kv_w
    col_mk = col_mq + mqk_w
    col_mv = col_mk + mqk_w
    col_mo = col_mv + mv_w
    col_mg = col_mo + mv_w
    col_bg = col_mg + n_gates
    main_pad = (-(col_mg + n_gates)) % 256

    xa = jnp.concatenate([ctx, x], axis=1)
    cond8 = jnp.zeros((8, d), F32).at[0].set(c_ctx).at[1:1 + b].set(c)
    mods = _ada_modulation(cond8, w_ada, b_ada).reshape(depth, 8, 6, d)
    cos, sin = _rope_tables(lc, lx)
    ne, ff2 = w_gate_up.shape[1], w_gate_up.shape[3]
    after = jnp.zeros((8, LANES), F32)
    bgl_all = b_gate_up.reshape(depth * ne, ff2)[:, _pair_split_index(ff2)][:, None, :]
    wd_all = w_down.reshape(depth * ne, ff2 // 2, d)
    bd_all = b_down.reshape(depth * ne, 1, d)

    def mod_table(l):
        return jnp.stack([jnp.broadcast_to(mods[l, 0], (b, 6, d)), mods[l, 1:1 + b]], axis=1)

    h = _norm_mod(xa, norm_g[0, 0], mod_table(0), lc, 0)
    for l in range(depth):
        last = l == depth - 1
        mod = mod_table(l)

        h = h.reshape(t, d)
        w_main = jnp.concatenate(
            [w_in[l, :, :col_bg], jnp.zeros((d, main_pad), F32)], axis=1).astype(BF16)
        p_main = _gemm(h, w_main, F32)
        gates = _gemm(h, w_in[l, :, col_bg:].astype(BF16), BF16, act="sigmoid")
        p3 = p_main.reshape(b, s, -1)

        ops = _s5_operators(ssm_a_re[l], ssm_a_im[l], ssm_log_dt[l], ssm_b_re[l], ssm_b_im[l],
                            ssm_c_re[l], ssm_c_im[l], ssm_d[l])
        ys = _s5_mix(p3, ops, ssm_w, lc)

        qh, kh, vh = _qk_prep(p3, cos, sin, attn_q_gain[l], attn_k_gain[l], col_q, col_k, col_v)
        att = _attention(qh, kh, vh, lc)

        hfb = _mlstm(p3, mlstm_gate_bias[l], lc, col_mq, col_mk, col_mv, col_mg)

        y = _merge(ys.reshape(t, ssm_w), att.reshape(t, attn_w), hfb.reshape(2, t, mv_w), p_main, gates,
                   ssm_w_glu[l].astype(BF16), ssm_b_glu[l], mlstm_norm_g[l],
                   w_branch_ssm[l].astype(BF16), w_branch_attn[l].astype(BF16),
                   w_branch_mlstm[l].astype(BF16), col_mo)
        xa = _out_proj(y.reshape(b, s, d), w_out[l].astype(BF16), xa, mod, lc)

        h2, top_idx, top_w = _norm_router(xa, norm_g[l, 1], mod, w_router[l], b_router[l], lc)
        n_slots = t * TOP_K + N_EXPERTS * EXPERT_TILE
        src, dest, tile_expert, n_used = _route(top_idx, n_slots)
        wgl = _split_pairs(w_gate_up.reshape(depth * ne, d, ff2), l * ne, ne, after)
        ysorted = _experts(h2.reshape(t, d // LANES, LANES), src.reshape(-1, EXPERT_TILE),
                           tile_expert + l * ne, n_used, wgl, bgl_all, wd_all, bd_all, l * ne)
        after = ysorted[:8, :LANES].astype(F32)
        yg = ysorted.at[dest.reshape(t, TOP_K).T].get(mode="promise_in_bounds").reshape(TOP_K, b, s, d)
        if last:
            xa = _combine(yg, top_w, xa, mod, final_g, lc, True)
        else:
            xa, h = _combine(yg, top_w, xa, mod, norm_g[l + 1, 0], lc, False, mod_table(l + 1))

    return xa
```
